```python
import math
import jax, jax.numpy as jnp
from jax import lax
import numpy as np


D_MODEL = 1024
BATCH = 8
SEQ = 4096
DEPTH = 2

RMS_EPS = 1e-6
ROPE_THETA = 10000.0
QUERY_BLOCK = 128
MLA_HEADS = 8
MLA_Q_LORA = 256
MLA_KV_LORA = 128
MLA_NOPE_DIM = 64
MLA_ROPE_DIM = 32
MLA_V_DIM = 64
FOX_HEADS = 8
FOX_HEAD_DIM = 64
FOX_FORGET_BIAS_INIT = 2.0
MOBA_HEADS = 8
MOBA_HEAD_DIM = 64
MOBA_BLOCK = 256
MOBA_TOPK = 3
MOBA_QUERY_CHUNK = 32
N_BRANCH = 3
BRANCH_WIDTH = FOX_HEADS * FOX_HEAD_DIM
DENSE_FF = 2816
N_EXPERTS = 8
TOP_K = 2
EXPERT_FF = 3584
MOE_ROW_BLOCK = 256
N_DENSE = (DEPTH + 1) // 2
N_MOE = DEPTH // 2

FOX_W = FOX_HEADS * FOX_HEAD_DIM
MOBA_W = MOBA_HEADS * MOBA_HEAD_DIM
GATE_COLS = N_BRANCH * D_MODEL
IN_SPLIT_SIZES = [MLA_Q_LORA, MLA_KV_LORA, MLA_ROPE_DIM,
                  FOX_W, FOX_W, FOX_W, FOX_HEADS,
                  MOBA_W, MOBA_W, MOBA_W, GATE_COLS]
IN_COLS = sum(IN_SPLIT_SIZES)
IN_SPLIT_POINTS = [int(v) for v in np.cumsum(IN_SPLIT_SIZES)[:-1]]

kernel_name = "hybrid_mla_fox_moba_moe_block"


def rms_norm(x, g):
    xf = x.astype(jnp.float32)
    y = xf * lax.rsqrt(jnp.mean(xf * xf, axis=-1, keepdims=True) + RMS_EPS)
    return (y * g.astype(jnp.float32)).astype(x.dtype)


def rope(x, positions):
    d = x.shape[-1]
    half = d // 2
    inv_freq = jnp.exp(-math.log(ROPE_THETA) * jnp.arange(half, dtype=jnp.float32) * 2.0 / d)
    ang = positions.astype(jnp.float32)[..., None] * inv_freq
    cos = jnp.cos(ang)[:, :, None, :]
    sin = jnp.sin(ang)[:, :, None, :]
    xf = x.astype(jnp.float32)
    x1, x2 = xf[..., :half], xf[..., half:]
    return jnp.concatenate([x1 * cos - x2 * sin, x2 * cos + x1 * sin], axis=-1).astype(x.dtype)


def causal_attention(q, k, v, scale, cum_log_f=None):
    B, S, H, dk = q.shape
    dv = v.shape[-1]
    nb = S // QUERY_BLOCK
    q_blocks = q.reshape(B, nb, QUERY_BLOCK, H, dk).swapaxes(0, 1)
    kpos = jnp.arange(S)
    idx = jnp.arange(nb)
    if cum_log_f is None:
        xs = (idx, q_blocks)
        ck = None
    else:
        xs = (idx, q_blocks, cum_log_f.reshape(B, nb, QUERY_BLOCK, H).swapaxes(0, 1))
        ck = cum_log_f.transpose(0, 2, 1)

    def block(args):
        i, qi = args[0], args[1]
        s = jnp.einsum('bqhd,bkhd->bhqk', qi, k).astype(jnp.float32) * scale
        if ck is not None:
            cq = args[2].transpose(0, 2, 1)
            s = s + (cq[..., :, None] - ck[..., None, :])
        qpos = i * QUERY_BLOCK + jnp.arange(QUERY_BLOCK)
        s = jnp.where(kpos[None, :] <= qpos[:, None], s, -jnp.inf)
        p = jax.nn.softmax(s, axis=-1).astype(v.dtype)
        return jnp.einsum('bhqk,bkhd->bqhd', p, v)

    out = lax.map(block, xs)
    return out.swapaxes(0, 1).reshape(B, S, H * dv)


def moba_attention(q, k, v):
    B, S, H, d = q.shape
    nblk = -(-S // MOBA_BLOCK)
    s_pad = nblk * MOBA_BLOCK
    pad = ((0, 0), (0, s_pad - S), (0, 0), (0, 0))
    kb = jnp.pad(k, pad).transpose(0, 2, 1, 3).reshape(B, H, nblk, MOBA_BLOCK, d)
    vb = jnp.pad(v, pad).transpose(0, 2, 1, 3).reshape(B, H, nblk, MOBA_BLOCK, d)
    k_mean = jnp.mean(kb.astype(jnp.float32), axis=3)
    n_sel = min(MOBA_TOPK, nblk)
    nq = S // MOBA_QUERY_CHUNK
    q_chunks = q.reshape(B, nq, MOBA_QUERY_CHUNK, H, d).transpose(1, 0, 3, 2, 4)
    b_idx = jnp.arange(B)[:, None, None, None]
    h_idx = jnp.arange(H)[None, :, None, None]
    scale = d ** -0.5
    blk_ids = jnp.arange(nblk)

    def chunk(args):
        i, qi = args
        qpos = i * MOBA_QUERY_CHUNK + jnp.arange(MOBA_QUERY_CHUNK)
        own = (i * MOBA_QUERY_CHUNK) // MOBA_BLOCK
        gate = jnp.einsum('bhqd,bhnd->bhqn', qi.astype(jnp.float32), k_mean)
        gate = jnp.where(blk_ids < own, gate, -jnp.inf)
        top_val, top_idx = lax.top_k(gate, n_sel)
        sel_ok = jnp.isfinite(top_val)
        k_sel = kb[b_idx, h_idx, top_idx]
        v_sel = vb[b_idx, h_idx, top_idx]
        s_sel = jnp.einsum('bhqd,bhqnkd->bhqnk', qi, k_sel).astype(jnp.float32) * scale
        s_sel = jnp.where(sel_ok[..., None], s_sel, -jnp.inf)
        s_sel = s_sel.reshape(B, H, MOBA_QUERY_CHUNK, n_sel * MOBA_BLOCK)
        k_own = lax.dynamic_index_in_dim(kb, own, axis=2, keepdims=False)
        v_own = lax.dynamic_index_in_dim(vb, own, axis=2, keepdims=False)
        s_own = jnp.einsum('bhqd,bhkd->bhqk', qi, k_own).astype(jnp.float32) * scale
        kpos = own * MOBA_BLOCK + jnp.arange(MOBA_BLOCK)
        s_own = jnp.where(kpos[None, :] <= qpos[:, None], s_own, -jnp.inf)
        p = jax.nn.softmax(jnp.concatenate([s_sel, s_own], axis=-1), axis=-1).astype(v.dtype)
        p_sel = p[..., :n_sel * MOBA_BLOCK].reshape(B, H, MOBA_QUERY_CHUNK, n_sel, MOBA_BLOCK)
        p_own = p[..., n_sel * MOBA_BLOCK:]
        return (jnp.einsum('bhqnk,bhqnkd->bhqd', p_sel, v_sel)
                + jnp.einsum('bhqk,bhkd->bhqd', p_own, v_own))

    out = lax.map(chunk, (jnp.arange(nq), q_chunks))
    return out.transpose(1, 0, 3, 2, 4).reshape(B, S, H * d)


def token_mixers(h, positions, w_in, g_q_lat, g_kv_lat, w_uq, w_ukv, b_forget, w_branch, w_out):
    B, S, _ = h.shape
    proj = h @ w_in
    (c_q, c_kv, k_pe, fq, fk, fv, f_logit, mq, mk, mv, gate_logits) = jnp.split(
        proj, IN_SPLIT_POINTS, axis=-1)

    q = (rms_norm(c_q, g_q_lat) @ w_uq).reshape(B, S, MLA_HEADS, MLA_NOPE_DIM + MLA_ROPE_DIM)
    q = jnp.concatenate([q[..., :MLA_NOPE_DIM], rope(q[..., MLA_NOPE_DIM:], positions)], axis=-1)
    kv = (rms_norm(c_kv, g_kv_lat) @ w_ukv).reshape(B, S, MLA_HEADS, MLA_NOPE_DIM + MLA_V_DIM)
    k_rot = rope(k_pe[:, :, None, :], positions)
    k = jnp.concatenate([kv[..., :MLA_NOPE_DIM],
                         jnp.broadcast_to(k_rot, (B, S, MLA_HEADS, MLA_ROPE_DIM))], axis=-1)
    y_mla = causal_attention(q, k, kv[..., MLA_NOPE_DIM:], (MLA_NOPE_DIM + MLA_ROPE_DIM) ** -0.5)

    log_f = jax.nn.log_sigmoid(f_logit.astype(jnp.float32) + b_forget.astype(jnp.float32))
    cum_log_f = jnp.cumsum(log_f, axis=1)
    heads = lambda t: t.reshape(B, S, FOX_HEADS, FOX_HEAD_DIM)
    y_fox = causal_attention(heads(fq), heads(fk), heads(fv), FOX_HEAD_DIM ** -0.5, cum_log_f)

    mheads = lambda t: t.reshape(B, S, MOBA_HEADS, MOBA_HEAD_DIM)
    y_moba = moba_attention(rope(mheads(mq), positions), rope(mheads(mk), positions), mheads(mv))

    branches = jnp.stack([y_mla, y_fox, y_moba], axis=2)
    gates = jax.nn.sigmoid(gate_logits.reshape(B, S, N_BRANCH, D_MODEL))
    merged = jnp.sum(gates * jnp.einsum('bsnc,ncd->bsnd', branches, w_branch), axis=2)
    return merged @ w_out


def dense_swiglu(h, w_gu, w_down):
    g, u = jnp.split(h @ w_gu, 2, axis=-1)
    return (jax.nn.silu(g) * u) @ w_down


def moe_swiglu(h, w_router, w_gu, w_down):
    B, S, D = h.shape
    T = B * S
    x = h.reshape(T, D)
    logits = (x @ w_router).astype(jnp.float32)
    top_logit, top_e = lax.top_k(logits, TOP_K)
    gate_w = jax.nn.softmax(top_logit, axis=-1)
    A = T * TOP_K
    flat_e = top_e.reshape(A)
    order = jnp.argsort(flat_e, stable=True)
    sorted_e = flat_e[order]
    sorted_tok = order // TOP_K
    sorted_w = gate_w.reshape(A)[order]
    counts = jnp.bincount(flat_e, length=N_EXPERTS)
    padded = (counts + MOE_ROW_BLOCK - 1) // MOE_ROW_BLOCK * MOE_ROW_BLOCK
    pad_end = jnp.cumsum(padded)
    pad_start = pad_end - padded
    start = jnp.cumsum(counts) - counts
    dest = pad_start[sorted_e] + jnp.arange(A) - start[sorted_e]
    n_rows = (-(-A // MOE_ROW_BLOCK) + N_EXPERTS) * MOE_ROW_BLOCK
    n_blocks = n_rows // MOE_ROW_BLOCK
    row_tok = jnp.zeros((n_rows,), jnp.int32).at[dest].set(sorted_tok.astype(jnp.int32))
    row_w = jnp.zeros((n_rows,), jnp.float32).at[dest].set(sorted_w)
    block_e = jnp.minimum(
        jnp.searchsorted(pad_end, jnp.arange(n_blocks) * MOE_ROW_BLOCK, side='right'),
        N_EXPERTS - 1)
    x_rows = x[row_tok].reshape(n_blocks, MOE_ROW_BLOCK, D)

    def expert_block(args):
        xb, e = args
        g, u = jnp.split(xb @ w_gu[e], 2, axis=-1)
        return (jax.nn.silu(g) * u) @ w_down[e]

    y_rows = lax.map(expert_block, (x_rows, block_e)).reshape(n_rows, D)
    y = jnp.zeros((T, D), y_rows.dtype).at[row_tok].add(
        y_rows * row_w[:, None].astype(y_rows.dtype))
    return y.reshape(B, S, D)


def setup_inputs(seed: int = 0) -> dict:
    key = jax.random.key(seed)
    ks = jax.random.split(key, 17)
    f32 = jnp.float32

    def dense(k, shape, fan_in):
        return jax.random.normal(k, shape, f32) * fan_in ** -0.5

    def gain(k, shape):
        return 1.0 + 0.02 * jax.random.normal(k, shape, f32)

    return {
        "x": jax.random.normal(ks[0], (BATCH, SEQ, D_MODEL), f32),
        "positions": jnp.broadcast_to(jnp.arange(SEQ, dtype=jnp.int32)[None, :], (BATCH, SEQ)),
        "g_mix": gain(ks[1], (DEPTH, D_MODEL)),
        "w_in": dense(ks[2], (DEPTH, D_MODEL, IN_COLS), D_MODEL),
        "g_q_lat": gain(ks[3], (DEPTH, MLA_Q_LORA)),
        "g_kv_lat": gain(ks[4], (DEPTH, MLA_KV_LORA)),
        "w_uq": dense(ks[5], (DEPTH, MLA_Q_LORA, MLA_HEADS * (MLA_NOPE_DIM + MLA_ROPE_DIM)), MLA_Q_LORA),
        "w_ukv": dense(ks[6], (DEPTH, MLA_KV_LORA, MLA_HEADS * (MLA_NOPE_DIM + MLA_V_DIM)), MLA_KV_LORA),
        "b_forget": FOX_FORGET_BIAS_INIT + 0.1 * jax.random.normal(ks[7], (DEPTH, FOX_HEADS), f32),
        "w_branch": dense(ks[8], (DEPTH, N_BRANCH, BRANCH_WIDTH, D_MODEL), BRANCH_WIDTH),
        "w_out": dense(ks[9], (DEPTH, D_MODEL, D_MODEL), D_MODEL),
        "g_ffn": gain(ks[10], (DEPTH, D_MODEL)),
        "w_dense_gu": dense(ks[11], (N_DENSE, D_MODEL, 2 * DENSE_FF), D_MODEL),
        "w_dense_down": dense(ks[12], (N_DENSE, DENSE_FF, D_MODEL), DENSE_FF),
        "w_router": dense(ks[13], (N_MOE, D_MODEL, N_EXPERTS), D_MODEL),
        "w_exp_gu": dense(ks[14], (N_MOE, N_EXPERTS, D_MODEL, 2 * EXPERT_FF), D_MODEL),
        "w_exp_down": dense(ks[15], (N_MOE, N_EXPERTS, EXPERT_FF, D_MODEL), EXPERT_FF),
        "g_final": gain(ks[16], (D_MODEL,)),
    }


def reference(x, positions, g_mix, w_in, g_q_lat, g_kv_lat, w_uq, w_ukv, b_forget, w_branch,
              w_out, g_ffn, w_dense_gu, w_dense_down, w_router, w_exp_gu, w_exp_down, g_final):
    for l in range(DEPTH):
        x = x + token_mixers(rms_norm(x, g_mix[l]), positions, w_in[l], g_q_lat[l], g_kv_lat[l],
                             w_uq[l], w_ukv[l], b_forget[l], w_branch[l], w_out[l])
        h = rms_norm(x, g_ffn[l])
        if l % 2 == 0:
            x = x + dense_swiglu(h, w_dense_gu[l // 2], w_dense_down[l // 2])
        else:
            x = x + moe_swiglu(h, w_router[l // 2], w_exp_gu[l // 2], w_exp_down[l // 2])
    return rms_norm(x, g_final)
```

```python
import functools
import math

import jax
import jax.numpy as jnp
import numpy as np
from jax import lax
from jax.experimental import pallas as pl
from jax.experimental.pallas import tpu as pltpu

F32 = jnp.float32
BF16 = jnp.bfloat16
NEG_INF = float("-inf")

D_MODEL = 1024
RMS_EPS = 1e-6
ROPE_THETA = 10000.0
HEADS = 8
HEAD_DIM = 64
MLA_Q_LORA = 256
MLA_KV_LORA = 128
MLA_ROPE_DIM = 32
BRANCH_WIDTH = HEADS * HEAD_DIM
MOBA_BLOCK = 256
MOBA_TOPK = 3
DENSE_FF = 2816
N_EXPERTS = 8
EXPERT_FF = 3584

LANES = 128
VMEM_LIMIT = 48 * 1024 * 1024

BIG_COLS = 6 * BRANCH_WIDTH + 3 * D_MODEL
BIG_TN = 512
MQ_TILE, MK_TILE = 3, 4
SMALL_COLS = MLA_Q_LORA + MLA_KV_LORA + 2 * LANES


def _cparams(sem):
    return pltpu.CompilerParams(dimension_semantics=sem, vmem_limit_bytes=VMEM_LIMIT)


def _rms(x, g):
    return x * lax.rsqrt(jnp.mean(x * x, axis=-1, keepdims=True) + RMS_EPS) * g


def _rope_table_kernel(pos_ref, f_ref, mc_ref, m1_ref, m2_ref, c_ref, s1_ref, s2_ref):
    ang = pos_ref[...].astype(F32) * f_ref[...]
    cos = jnp.cos(ang)
    sin = jnp.sin(ang)
    mc = mc_ref[...]
    c_ref[...] = cos * mc + (1.0 - mc)
    s1_ref[...] = sin * m1_ref[...]
    s2_ref[...] = sin * m2_ref[...]


def _rope_tables(pos, freq, mc, m1, m2, tm=1024):
    T = pos.shape[0]
    row = pl.BlockSpec((tm, 1), lambda i: (i, 0))
    pat = pl.BlockSpec((1, LANES), lambda i: (0, 0))
    out = pl.BlockSpec((tm, LANES), lambda i: (i, 0))
    shp = jax.ShapeDtypeStruct((T, LANES), F32)
    return pl.pallas_call(
        _rope_table_kernel, grid=(T // tm,),
        in_specs=[row, pat, pat, pat, pat], out_specs=[out, out, out],
        out_shape=[shp, shp, shp], compiler_params=_cparams(("parallel",)),
        name="rope_tables",
    )(pos, freq, mc, m1, m2)


def _rope_patterns(group, x1_lo, half):
    d = 2 * half
    inv_freq = jnp.exp(-math.log(ROPE_THETA) * jnp.arange(half, dtype=F32) * 2.0 / d)
    lane = np.arange(LANES) % group
    in_x1 = (lane >= x1_lo) & (lane < x1_lo + half)
    in_x2 = (lane >= x1_lo + half) & (lane < x1_lo + d)
    k = np.where(in_x1, lane - x1_lo, np.where(in_x2, lane - x1_lo - half, 0))
    freq = jnp.where(jnp.asarray(in_x1 | in_x2), inv_freq[k], 0.0)[None, :].astype(F32)
    mc = jnp.asarray((in_x1 | in_x2).astype(np.float32))[None, :]
    m1 = jnp.asarray(-(in_x1.astype(np.float32)))[None, :]
    m2 = jnp.asarray(in_x2.astype(np.float32))[None, :]
    return freq, mc, m1, m2


def _apply_rope(x, c, s1, s2, half):
    n = x.shape[-1]
    reps = n // LANES
    c, s1, s2 = (jnp.tile(t, (1, reps)) if reps > 1 else t for t in (c, s1, s2))
    return x * c + pltpu.roll(x, n - half, 1) * s1 + pltpu.roll(x, half, 1) * s2


def _inproj_kernel(x_ref, g_ref, w_ref, c_ref, s1_ref, s2_ref, o_ref, h_ref):
    j = pl.program_id(1)

    @pl.when(j == 0)
    def _():
        h_ref[...] = _rms(x_ref[...], g_ref[...]).astype(BF16)

    acc = jnp.dot(h_ref[...], w_ref[...], preferred_element_type=F32)
    is_rope = jnp.logical_or(j == MQ_TILE, j == MK_TILE)

    @pl.when(is_rope)
    def _():
        o_ref[...] = _apply_rope(acc, c_ref[...], s1_ref[...], s2_ref[...], HEAD_DIM // 2).astype(BF16)

    @pl.when(jnp.logical_not(is_rope))
    def _():
        o_ref[...] = acc.astype(BF16)


def _inproj(x, g, w_big, tabs, tm=512):
    T = x.shape[0]
    tab = pl.BlockSpec((tm, LANES), lambda i, j: (i, 0))
    return pl.pallas_call(
        _inproj_kernel, grid=(T // tm, BIG_COLS // BIG_TN),
        in_specs=[pl.BlockSpec((tm, D_MODEL), lambda i, j: (i, 0)),
                  pl.BlockSpec((1, D_MODEL), lambda i, j: (0, 0)),
                  pl.BlockSpec((D_MODEL, BIG_TN), lambda i, j: (0, j)),
                  tab, tab, tab],
        out_specs=pl.BlockSpec((tm, BIG_TN), lambda i, j: (i, j)),
        out_shape=jax.ShapeDtypeStruct((T, BIG_COLS), BF16),
        scratch_shapes=[pltpu.VMEM((tm, D_MODEL), BF16)],
        compiler_params=_cparams(("parallel", "arbitrary")),
        name="inproj",
    )(x, g, w_big, *tabs)


def _mla_prep_kernel(x_ref, g_ref, ws_ref, gq_ref, gkv_ref, wq_ref, wk_ref, wv_ref,
                     c_ref, s1_ref, s2_ref, q_ref, k_ref, v_ref, fl_ref, *, scale):
    h = _rms(x_ref[...], g_ref[...]).astype(BF16)
    small = jnp.dot(h, ws_ref[...], preferred_element_type=F32)
    c_q = small[:, :MLA_Q_LORA]
    c_kv = small[:, MLA_Q_LORA:MLA_Q_LORA + MLA_KV_LORA]
    k_pe = small[:, MLA_Q_LORA + MLA_KV_LORA:MLA_Q_LORA + MLA_KV_LORA + LANES]
    fl_ref[...] = small[:, SMALL_COLS - LANES:SMALL_COLS - LANES + HEADS]
    c, s1, s2 = c_ref[...], s1_ref[...], s2_ref[...]
    half = MLA_ROPE_DIM // 2
    qn = _rms(c_q, gq_ref[...]).astype(BF16)
    q = jnp.dot(qn, wq_ref[...], preferred_element_type=F32) * scale
    q_ref[...] = _apply_rope(q, c, s1, s2, half).astype(BF16)
    kvn = _rms(c_kv, gkv_ref[...]).astype(BF16)
    k_nope = jnp.dot(kvn, wk_ref[...], preferred_element_type=F32)
    k_rot = _apply_rope(k_pe, c, s1, s2, half)
    k_ref[...] = (k_nope + jnp.tile(k_rot, (1, HEADS))).astype(BF16)
    v_ref[...] = jnp.dot(kvn, wv_ref[...], preferred_element_type=F32).astype(BF16)


def _mla_prep(x, g, w_small, g_q, g_kv, wq, wk, wv, tabs, scale, tm=512):
    T = x.shape[0]
    full = lambda shape: pl.BlockSpec(shape, lambda i: (0,) * len(shape))
    row = lambda n: pl.BlockSpec((tm, n), lambda i: (i, 0))
    qk = HEADS * LANES
    return pl.pallas_call(
        functools.partial(_mla_prep_kernel, scale=scale), grid=(T // tm,),
        in_specs=[row(D_MODEL), full((1, D_MODEL)), full((D_MODEL, SMALL_COLS)),
                  full((1, MLA_Q_LORA)), full((1, MLA_KV_LORA)),
                  full((MLA_Q_LORA, qk)), full((MLA_KV_LORA, qk)), full((MLA_KV_LORA, BRANCH_WIDTH)),
                  row(LANES), row(LANES), row(LANES)],
        out_specs=[row(qk), row(qk), row(BRANCH_WIDTH), row(HEADS)],
        out_shape=[jax.ShapeDtypeStruct((T, qk), BF16), jax.ShapeDtypeStruct((T, qk), BF16),
                   jax.ShapeDtypeStruct((T, BRANCH_WIDTH), BF16), jax.ShapeDtypeStruct((T, HEADS), F32)],
        compiler_params=_cparams(("parallel",)),
        name="mla_prep",
    )(x, g, w_small, g_q, g_kv, wq, wk, wv, *tabs)


def _cumlogf_kernel(fl_ref, b_ref, c_ref):
    z = fl_ref[0] + b_ref[...]
    x = jnp.minimum(z, 0.0) - jnp.log1p(jnp.exp(-jnp.abs(z)))
    n = x.shape[-1]
    lane = lax.broadcasted_iota(jnp.int32, x.shape, 1)
    d = 1
    while d < n:
        x = x + jnp.where(lane >= d, pltpu.roll(x, d, 1), 0.0)
        d *= 2
    c_ref[0] = x


def _cumlogf(fl_t, b_col):
    B, H, S = fl_t.shape
    return pl.pallas_call(
        _cumlogf_kernel, grid=(B,),
        in_specs=[pl.BlockSpec((1, H, S), lambda b: (b, 0, 0)), pl.BlockSpec((H, 1), lambda b: (0, 0))],
        out_specs=pl.BlockSpec((1, H, S), lambda b: (b, 0, 0)),
        out_shape=jax.ShapeDtypeStruct((B, H, S), F32),
        compiler_params=_cparams(("parallel",)),
        name="cumlogf",
    )(fl_t, b_col)


def _split3(c):
    hi = c.astype(BF16)
    r = c - hi.astype(F32)
    mid = r.astype(BF16)
    lo = (r - mid.astype(F32)).astype(BF16)
    return hi.astype(F32), mid.astype(F32), lo.astype(F32)


def _fox_prep_kernel(q_ref, k_ref, c_ref, qo_ref, ko_ref):
    tm = q_ref.shape[0]
    lane = lax.broadcasted_iota(jnp.int32, (tm, LANES), 1)
    c = c_ref[...]
    for hp in range(HEADS // 2):
        q2 = q_ref[:, hp * LANES:(hp + 1) * LANES].astype(F32)
        k2 = k_ref[:, hp * LANES:(hp + 1) * LANES].astype(F32)
        for hh in range(2):
            h = 2 * hp + hh
            hi, mid, lo = _split3(c[:, h:h + 1])
            aug_c = jnp.where(lane == HEAD_DIM, hi, jnp.where(lane == HEAD_DIM + 1, mid,
                              jnp.where(lane == HEAD_DIM + 2, lo, 0.0)))
            ones_a = jnp.where((lane >= HEAD_DIM) & (lane < HEAD_DIM + 3), 1.0, 0.0)
            qh = q2 if hh == 0 else pltpu.roll(q2, HEAD_DIM, 1)
            kh = k2 if hh == 0 else pltpu.roll(k2, HEAD_DIM, 1)
            q_aug = jnp.where(lane < HEAD_DIM, qh, ones_a + pltpu.roll(aug_c, 3, 1))
            k_aug = jnp.where(lane < HEAD_DIM, kh, pltpu.roll(ones_a, 3, 1) - aug_c)
            qo_ref[:, h * LANES:(h + 1) * LANES] = q_aug.astype(BF16)
            ko_ref[:, h * LANES:(h + 1) * LANES] = k_aug.astype(BF16)


def _fox_prep(big, c_rows, tm=512):
    T = big.shape[0]
    nb = BRANCH_WIDTH // BRANCH_WIDTH
    qk = HEADS * LANES
    return pl.pallas_call(
        _fox_prep_kernel, grid=(T // tm,),
        in_specs=[pl.BlockSpec((tm, BRANCH_WIDTH), lambda i: (i, 0)),
                  pl.BlockSpec((tm, BRANCH_WIDTH), lambda i: (i, nb)),
                  pl.BlockSpec((tm, HEADS), lambda i: (i, 0))],
        out_specs=[pl.BlockSpec((tm, qk), lambda i: (i, 0)), pl.BlockSpec((tm, qk), lambda i: (i, 0))],
        out_shape=[jax.ShapeDtypeStruct((T, qk), BF16), jax.ShapeDtypeStruct((T, qk), BF16)],
        compiler_params=_cparams(("parallel",)),
        name="fox_prep",
    )(big, big, c_rows)


def _nt_dot(a, b):
    return lax.dot_general(a, b, (((1,), (1,)), ((), ())), preferred_element_type=F32)


def _softmax_step(st, m, l, acc, vt):
    m_new = jnp.maximum(m, jnp.max(st, axis=0, keepdims=True))
    alpha = jnp.exp(m - m_new)
    p = jnp.exp(st - m_new)
    l = alpha * l + jnp.sum(p, axis=0, keepdims=True)
    acc = alpha * acc + jnp.dot(vt, p.astype(BF16), preferred_element_type=F32)
    return m_new, l, acc


def _causal_mask(st, tk, tq, k0, q0):
    kpos = k0 + lax.broadcasted_iota(jnp.int32, (tk, tq), 0)
    qpos = q0 + lax.broadcasted_iota(jnp.int32, (tk, tq), 1)
    return jnp.where(kpos <= qpos, st, NEG_INF)


def _flash_kernel(q_ref, k_ref, vt_ref, o_ref, *, tq, tk):
    i = pl.program_id(2)
    r = tq // tk
    outs = []
    for hh in range(2):
        q = q_ref[0, :, hh * LANES:(hh + 1) * LANES]

        def step(j, carry, masked):
            st = _nt_dot(k_ref[0, j, :, hh * LANES:(hh + 1) * LANES], q)
            if masked:
                st = _causal_mask(st, tk, tq, j * tk, i * tq)
            return _softmax_step(st, *carry, vt_ref[0, j, hh * HEAD_DIM:(hh + 1) * HEAD_DIM, :])

        carry = (jnp.full((1, tq), NEG_INF, F32), jnp.zeros((1, tq), F32), jnp.zeros((HEAD_DIM, tq), F32))
        carry = lax.fori_loop(0, i * r, functools.partial(step, masked=False), carry)
        for d in range(r):
            carry = step(i * r + d, carry, True)
        _, l, acc = carry
        outs.append((acc / l).T)
    o_ref[0] = jnp.concatenate(outs, axis=1).astype(BF16)


def _flash(q, k, vt, tq, tk):
    B, S, _ = q.shape
    nk = S // tk
    return pl.pallas_call(
        functools.partial(_flash_kernel, tq=tq, tk=tk), grid=(B, HEADS // 2, S // tq),
        in_specs=[pl.BlockSpec((1, tq, 2 * LANES), lambda b, h, i: (b, i, h)),
                  pl.BlockSpec((1, nk, tk, 2 * LANES), lambda b, h, i: (b, 0, 0, h)),
                  pl.BlockSpec((1, nk, LANES, tk), lambda b, h, i: (b, 0, h, 0))],
        out_specs=pl.BlockSpec((1, tq, LANES), lambda b, h, i: (b, i, h)),
        out_shape=jax.ShapeDtypeStruct((B, S, BRANCH_WIDTH), BF16),
        compiler_params=_cparams(("parallel", "parallel", "arbitrary")),
        name="flash",
    )(q, k, vt)


def _moba_kernel(q_ref, k_ref, vt_ref, o_ref, kmean_ref, bias_ref, *, nblk):
    i = pl.program_id(2)
    blk = MOBA_BLOCK

    @pl.when(i == 0)
    def _():
        for n in range(nblk):
            kmean_ref[n:n + 1, :] = jnp.mean(k_ref[0, n].astype(F32), axis=0, keepdims=True)

    lane = lax.broadcasted_iota(jnp.int32, (blk, LANES), 1)
    blk_id = lax.broadcasted_iota(jnp.int32, (nblk, blk), 0)
    q2 = q_ref[0]
    outs = []
    for hh in range(2):
        in_head = (lane >= hh * HEAD_DIM) & (lane < (hh + 1) * HEAD_DIM)
        q = jnp.where(in_head, q2, jnp.zeros_like(q2))

        g = lax.dot_general(kmean_ref[...], q.astype(F32), (((1,), (1,)), ((), ())),
                            precision=lax.Precision.HIGHEST, preferred_element_type=F32)
        g = jnp.where(blk_id < i, g, NEG_INF)
        bias = jnp.full((nblk, blk), NEG_INF, F32)
        for _ in range(MOBA_TOPK):
            mx = jnp.max(g, axis=0, keepdims=True)
            first = jnp.min(jnp.where(g == mx, blk_id, nblk), axis=0, keepdims=True)
            pick = (blk_id == first) & (mx > NEG_INF)
            bias = jnp.where(pick, 0.0, bias)
            g = jnp.where(pick, NEG_INF, g)
        bias_ref[hh] = bias

        st = _causal_mask(_nt_dot(k_ref[0, i], q), blk, blk, 0, 0)
        carry = (jnp.full((1, blk), NEG_INF, F32), jnp.zeros((1, blk), F32), jnp.zeros((HEAD_DIM, blk), F32))
        carry = _softmax_step(st, *carry, vt_ref[0, i, hh * HEAD_DIM:(hh + 1) * HEAD_DIM, :])

        def step(n, carry):
            st = _nt_dot(k_ref[0, n], q) + bias_ref[hh, pl.ds(n, 1), :]
            return _softmax_step(st, *carry, vt_ref[0, n, hh * HEAD_DIM:(hh + 1) * HEAD_DIM, :])

        _, l, acc = lax.fori_loop(0, i, step, carry)
        outs.append((acc / l).T)
    o_ref[0] = jnp.concatenate(outs, axis=1).astype(BF16)


def _moba(q, k, vt):
    B, S, _ = q.shape
    nblk = S // MOBA_BLOCK
    qc = MQ_TILE * BIG_TN // LANES
    kc = MK_TILE * BIG_TN // LANES
    return pl.pallas_call(
        functools.partial(_moba_kernel, nblk=nblk), grid=(B, HEADS // 2, nblk),
        in_specs=[pl.BlockSpec((1, MOBA_BLOCK, LANES), lambda b, h, i: (b, i, qc + h)),
                  pl.BlockSpec((1, nblk, MOBA_BLOCK, LANES), lambda b, h, i: (b, 0, 0, kc + h)),
                  pl.BlockSpec((1, nblk, LANES, MOBA_BLOCK), lambda b, h, i: (b, 0, h, 0))],
        out_specs=pl.BlockSpec((1, MOBA_BLOCK, LANES), lambda b, h, i: (b, i, h)),
        out_shape=jax.ShapeDtypeStruct((B, S, BRANCH_WIDTH), BF16),
        scratch_shapes=[pltpu.VMEM((nblk, LANES), F32), pltpu.VMEM((2, nblk, MOBA_BLOCK), F32)],
        compiler_params=_cparams(("parallel", "parallel", "arbitrary")),
        name="moba",
    )(q, k, vt)


def _merge_kernel(x_ref, ya_ref, yb_ref, yc_ref, ga_ref, gb_ref, gc_ref, wb_ref, wo_ref, o_ref):
    merged = None
    for n, (y_ref, g_ref) in enumerate(((ya_ref, ga_ref), (yb_ref, gb_ref), (yc_ref, gc_ref))):
        proj = jnp.dot(y_ref[...], wb_ref[n], preferred_element_type=F32)
        term = jax.nn.sigmoid(g_ref[...].astype(F32)) * proj
        merged = term if merged is None else merged + term
    o_ref[...] = x_ref[...] + jnp.dot(merged.astype(BF16), wo_ref[...], preferred_element_type=F32)


def _merge(x, y_mla, y_fox, y_moba, big, w_branch, w_out, tm=256):
    T = x.shape[0]
    g0 = 6 * BRANCH_WIDTH // D_MODEL
    row = lambda n: pl.BlockSpec((tm, n), lambda i: (i, 0))
    gate = lambda n: pl.BlockSpec((tm, D_MODEL), lambda i: (i, g0 + n))
    return pl.pallas_call(
        _merge_kernel, grid=(T // tm,),
        in_specs=[row(D_MODEL), row(BRANCH_WIDTH), row(BRANCH_WIDTH), row(BRANCH_WIDTH),
                  gate(0), gate(1), gate(2),
                  pl.BlockSpec((3, BRANCH_WIDTH, D_MODEL), lambda i: (0, 0, 0)),
                  pl.BlockSpec((D_MODEL, D_MODEL), lambda i: (0, 0))],
        out_specs=row(D_MODEL),
        out_shape=jax.ShapeDtypeStruct((T, D_MODEL), F32),
        compiler_params=_cparams(("parallel",)),
        name="merge",
    )(x, y_mla, y_fox, y_moba, big, big, big, w_branch, w_out)


def _dense_ffn_kernel(x_ref, g_ref, wg_ref, wu_ref, wd_ref, o_ref, h_ref, acc_ref):
    f = pl.program_id(1)

    @pl.when(f == 0)
    def _():
        h_ref[...] = _rms(x_ref[...], g_ref[...]).astype(BF16)
        acc_ref[...] = x_ref[...]

    h = h_ref[...]
    gate = jnp.dot(h, wg_ref[...], preferred_element_type=F32)
    up = jnp.dot(h, wu_ref[...], preferred_element_type=F32)
    act = (jax.nn.silu(gate) * up).astype(BF16)
    acc_ref[...] += jnp.dot(act, wd_ref[...], preferred_element_type=F32)

    @pl.when(f == pl.num_programs(1) - 1)
    def _():
        o_ref[...] = acc_ref[...]


def _dense_ffn(x, g, w_gu, w_down, tm=512, tf=256):
    T = x.shape[0]
    nf = DENSE_FF // tf
    return pl.pallas_call(
        _dense_ffn_kernel, grid=(T // tm, nf),
        in_specs=[pl.BlockSpec((tm, D_MODEL), lambda i, f: (i, 0)),
                  pl.BlockSpec((1, D_MODEL), lambda i, f: (0, 0)),
                  pl.BlockSpec((D_MODEL, tf), lambda i, f: (0, f)),
                  pl.BlockSpec((D_MODEL, tf), lambda i, f: (0, nf + f)),
                  pl.BlockSpec((tf, D_MODEL), lambda i, f: (f, 0))],
        out_specs=pl.BlockSpec((tm, D_MODEL), lambda i, f: (i, 0)),
        out_shape=jax.ShapeDtypeStruct((T, D_MODEL), F32),
        scratch_shapes=[pltpu.VMEM((tm, D_MODEL), BF16), pltpu.VMEM((tm, D_MODEL), F32)],
        compiler_params=_cparams(("parallel", "arbitrary")),
        name="dense_ffn",
    )(x, g, w_gu, w_gu, w_down)


def _router_kernel(x_ref, g_ref, wr_ref, h_ref, r_ref):
    h = _rms(x_ref[...], g_ref[...])
    h_ref[...] = h
    logits = jnp.dot(h, wr_ref[...], precision=lax.Precision.HIGHEST, preferred_element_type=F32)
    lane = lax.broadcasted_iota(jnp.int32, logits.shape, 1)
    logits = jnp.where(lane < N_EXPERTS, logits, NEG_INF)
    m1 = jnp.max(logits, axis=-1, keepdims=True)
    i1 = jnp.min(jnp.where(logits == m1, lane, LANES), axis=-1, keepdims=True)
    rest = jnp.where(lane == i1, NEG_INF, logits)
    m2 = jnp.max(rest, axis=-1, keepdims=True)
    i2 = jnp.min(jnp.where(rest == m2, lane, LANES), axis=-1, keepdims=True)
    e2 = jnp.exp(m2 - m1)
    w1 = 1.0 / (1.0 + e2)
    w2 = e2 / (1.0 + e2)
    r_ref[...] = jnp.where(lane == 0, i1.astype(F32), jnp.where(lane == 1, i2.astype(F32),
                           jnp.where(lane == 2, w1, jnp.where(lane == 3, w2, 0.0))))


def _router(x, g, w_router_pad, tm=512):
    T = x.shape[0]
    return pl.pallas_call(
        _router_kernel, grid=(T // tm,),
        in_specs=[pl.BlockSpec((tm, D_MODEL), lambda i: (i, 0)),
                  pl.BlockSpec((1, D_MODEL), lambda i: (0, 0)),
                  pl.BlockSpec((D_MODEL, LANES), lambda i: (0, 0))],
        out_specs=[pl.BlockSpec((tm, D_MODEL), lambda i: (i, 0)), pl.BlockSpec((tm, LANES), lambda i: (i, 0))],
        out_shape=[jax.ShapeDtypeStruct((T, D_MODEL), F32), jax.ShapeDtypeStruct((T, LANES), F32)],
        compiler_params=_cparams(("parallel",)),
        name="router",
    )(x, g, w_router_pad)


def _gather_kernel(idx_ref, src_ref, o_ref, sem, *, bm):
    def row_copy(r):
        return pltpu.make_async_copy(src_ref.at[pl.ds(idx_ref[0, 0, r], 1), :], o_ref.at[pl.ds(r, 1), :], sem)

    def start(r, c):
        row_copy(r).start()
        return c

    def wait(r, c):
        row_copy(r).wait()
        return c

    lax.fori_loop(0, bm, start, 0)
    lax.fori_loop(0, bm, wait, 0)


def _gather_rows(src, idx, bm=256):
    M = idx.shape[0]
    C = src.shape[1]
    idx3 = idx.reshape(M // bm, 1, bm)
    return pl.pallas_call(
        functools.partial(_gather_kernel, bm=bm), grid=(M // bm,),
        in_specs=[pl.BlockSpec((1, 1, bm), lambda i: (i, 0, 0), memory_space=pltpu.SMEM),
                  pl.BlockSpec(memory_space=pl.ANY)],
        out_specs=pl.BlockSpec((bm, C), lambda i: (i, 0)),
        out_shape=jax.ShapeDtypeStruct((M, C), src.dtype),
        scratch_shapes=[pltpu.SemaphoreType.DMA(())],
        compiler_params=_cparams(("arbitrary",)),
        name="gather_rows",
    )(idx3, src)


def _expert_kernel(be_ref, nused_ref, x_ref, wg_ref, wu_ref, wd_ref, o_ref, acc_ref):
    b = pl.program_id(0)
    f = pl.program_id(1)
    used = b < nused_ref[0]

    @pl.when(used)
    def _():
        x = x_ref[...].astype(BF16)
        gate = jnp.dot(x, wg_ref[0], preferred_element_type=F32)
        up = jnp.dot(x, wu_ref[0], preferred_element_type=F32)
        act = (jax.nn.silu(gate) * up).astype(BF16)
        y = jnp.dot(act, wd_ref[0], preferred_element_type=F32)

        @pl.when(f == 0)
        def _():
            acc_ref[...] = y

        @pl.when(f > 0)
        def _():
            acc_ref[...] += y

    @pl.when(f == pl.num_programs(1) - 1)
    def _():
        o_ref[...] = jnp.where(used, acc_ref[...], 0.0)


def _experts(x_rows, block_e, n_used, w_gu, w_down, bm, tf=512):
    n_rows = x_rows.shape[0]
    nf = EXPERT_FF // tf
    nb = n_rows // bm

    def fsel(b, f, nused):
        return jnp.where(b < nused[0], f, nf - 1)

    grid_spec = pltpu.PrefetchScalarGridSpec(
        num_scalar_prefetch=2, grid=(nb, nf),
        in_specs=[pl.BlockSpec((bm, D_MODEL), lambda b, f, be, nu: (b, 0)),
                  pl.BlockSpec((1, D_MODEL, tf), lambda b, f, be, nu: (be[b], 0, fsel(b, f, nu))),
                  pl.BlockSpec((1, D_MODEL, tf), lambda b, f, be, nu: (be[b], 0, nf + fsel(b, f, nu))),
                  pl.BlockSpec((1, tf, D_MODEL), lambda b, f, be, nu: (be[b], fsel(b, f, nu), 0))],
        out_specs=pl.BlockSpec((bm, D_MODEL), lambda b, f, be, nu: (b, 0)),
        scratch_shapes=[pltpu.VMEM((bm, D_MODEL), F32)])
    return pl.pallas_call(
        _expert_kernel, grid_spec=grid_spec,
        out_shape=jax.ShapeDtypeStruct((n_rows, D_MODEL), F32),
        compiler_params=_cparams(("arbitrary", "arbitrary")),
        name="experts",
    )(block_e, n_used, x_rows, w_gu, w_gu, w_down)


def _combine_kernel(x_ref, y_ref, r_ref, g_ref, o_ref):
    r = r_ref[...]
    y = y_ref[...]
    x = x_ref[...] + r[:, 2:3] * y[:, :D_MODEL] + r[:, 3:4] * y[:, D_MODEL:]
    o_ref[...] = _rms(x, g_ref[...])


def _combine(x, y_pairs, route, g_final, tm=256):
    T = x.shape[0]
    return pl.pallas_call(
        _combine_kernel, grid=(T // tm,),
        in_specs=[pl.BlockSpec((tm, D_MODEL), lambda i: (i, 0)),
                  pl.BlockSpec((tm, 2 * D_MODEL), lambda i: (i, 0)),
                  pl.BlockSpec((tm, LANES), lambda i: (i, 0)),
                  pl.BlockSpec((1, D_MODEL), lambda i: (0, 0))],
        out_specs=pl.BlockSpec((tm, D_MODEL), lambda i: (i, 0)),
        out_shape=jax.ShapeDtypeStruct((T, D_MODEL), F32),
        compiler_params=_cparams(("parallel",)),
        name="combine",
    )(x, y_pairs, route, g_final)


def _pad_cols(w, n):
    return jnp.pad(w, ((0, 0), (0, n - w.shape[1])))


def _split_in_weights(w_in):
    sizes = [MLA_Q_LORA, MLA_KV_LORA, MLA_ROPE_DIM, BRANCH_WIDTH, BRANCH_WIDTH, BRANCH_WIDTH, HEADS,
             BRANCH_WIDTH, BRANCH_WIDTH, BRANCH_WIDTH, 3 * D_MODEL]
    pts = np.cumsum(sizes)[:-1]
    c_q, c_kv, k_pe, fq, fk, fv, fl, mq, mk, mv, gates = jnp.split(w_in, pts, axis=1)
    zeros = lambda n: jnp.zeros((D_MODEL, n), w_in.dtype)
    pe_tile = jnp.concatenate([zeros(HEAD_DIM), k_pe, zeros(LANES - HEAD_DIM - MLA_ROPE_DIM)], axis=1)
    w_small = jnp.concatenate([c_q, c_kv, pe_tile, _pad_cols(fl, LANES)], axis=1)
    qscale = HEAD_DIM ** -0.5
    w_big = jnp.concatenate([fq * qscale, fk, fv, mq * qscale, mk, mv, gates], axis=1)
    return w_small.astype(BF16), w_big.astype(BF16)


def _mla_up_weights(w_uq, w_ukv):
    dq = HEAD_DIM + MLA_ROPE_DIM
    wq = w_uq.reshape(MLA_Q_LORA, HEADS, dq)
    wq = jnp.pad(wq, ((0, 0), (0, 0), (0, LANES - dq))).reshape(MLA_Q_LORA, HEADS * LANES)
    wkv = w_ukv.reshape(MLA_KV_LORA, HEADS, 2 * HEAD_DIM)
    wk = jnp.pad(wkv[:, :, :HEAD_DIM], ((0, 0), (0, 0), (0, LANES - HEAD_DIM))).reshape(MLA_KV_LORA, HEADS * LANES)
    wv = wkv[:, :, HEAD_DIM:].reshape(MLA_KV_LORA, BRANCH_WIDTH)
    return wq.astype(BF16), wk.astype(BF16), wv.astype(BF16)


def _chunk_rows(a, B, S, t):
    return a.reshape(B, S // t, t, a.shape[-1])


def _chunk_cols_t(a, B, S, t):
    return a.reshape(B, S // t, t, a.shape[-1]).transpose(0, 1, 3, 2)


ATT_TQ = 256
ATT_TK = 256
MOE_BM = 512


def _token_mixers(x, B, S, g_mix, w_in, g_q_lat, g_kv_lat, w_uq, w_ukv, b_forget, w_branch, w_out,
                  tabs_mla, tabs_moba):
    T = B * S
    w_small, w_big = _split_in_weights(w_in)
    wq, wk, wv = _mla_up_weights(w_uq, w_ukv)
    g = g_mix[None, :]

    big = _inproj(x, g, w_big, tabs_moba)
    q_mla, k_mla, v_mla, f_logit = _mla_prep(
        x, g, w_small, g_q_lat[None, :], g_kv_lat[None, :], wq, wk, wv, tabs_mla,
        (HEAD_DIM + MLA_ROPE_DIM) ** -0.5)

    y_mla = _flash(q_mla.reshape(B, S, -1), _chunk_rows(k_mla, B, S, ATT_TK),
                   _chunk_cols_t(v_mla, B, S, ATT_TK), ATT_TQ, ATT_TK)

    c = _cumlogf(f_logit.reshape(B, S, HEADS).transpose(0, 2, 1), b_forget[:, None])
    c_rows = c.transpose(0, 2, 1).reshape(T, HEADS)
    q_fox, k_fox = _fox_prep(big, c_rows)
    fv = big[:, 2 * BRANCH_WIDTH:3 * BRANCH_WIDTH]
    y_fox = _flash(q_fox.reshape(B, S, -1), _chunk_rows(k_fox, B, S, ATT_TK),
                   _chunk_cols_t(fv, B, S, ATT_TK), ATT_TQ, ATT_TK)

    mv = big[:, 5 * BRANCH_WIDTH:6 * BRANCH_WIDTH]
    y_moba = _moba(big.reshape(B, S, -1), _chunk_rows(big, B, S, MOBA_BLOCK),
                   _chunk_cols_t(mv, B, S, MOBA_BLOCK))

    return _merge(x, y_mla.reshape(T, -1), y_fox.reshape(T, -1), y_moba.reshape(T, -1), big,
                  w_branch.astype(BF16), w_out.astype(BF16))


def _moe(x, g_ffn, w_router, w_gu, w_down, g_final):
    T = x.shape[0]
    A = 2 * T
    bm = MOE_BM
    h, route = _router(x, g_ffn[None, :], _pad_cols(w_router, LANES))
    top_e = route[:, :2].astype(jnp.int32).reshape(A)
    onehot = (top_e[:, None] == jnp.arange(N_EXPERTS)[None, :]).astype(jnp.int32)
    csum = jnp.cumsum(onehot, axis=0)
    counts = csum[-1]
    rank = jnp.sum((csum - onehot) * onehot, axis=1)
    padded = (counts + bm - 1) // bm * bm
    pad_end = jnp.cumsum(padded)
    dest = (pad_end - padded)[top_e] + rank
    n_rows = (A // bm + N_EXPERTS) * bm
    row_tok = jnp.zeros((n_rows,), jnp.int32).at[dest].set(jnp.arange(A, dtype=jnp.int32) // 2)
    block_e = jnp.minimum(jnp.searchsorted(pad_end, jnp.arange(n_rows // bm) * bm, side="right"),
                          N_EXPERTS - 1).astype(jnp.int32)
    n_used = (pad_end[-1:] // bm).astype(jnp.int32)

    x_rows = _gather_rows(h, row_tok)
    y_rows = _experts(x_rows, block_e, n_used, w_gu.astype(BF16), w_down.astype(BF16), bm)
    y_pairs = _gather_rows(y_rows, dest.astype(jnp.int32)).reshape(T, 2 * D_MODEL)
    return _combine(x, y_pairs, route, g_final[None, :])


def kernel(x, positions, g_mix, w_in, g_q_lat, g_kv_lat, w_uq, w_ukv, b_forget, w_branch, w_out, g_ffn,
           w_dense_gu, w_dense_down, w_router, w_exp_gu, w_exp_down, g_final):
    B, S, D = x.shape
    T = B * S
    depth = g_mix.shape[0]
    assert depth == 2 and D == D_MODEL and S % MOBA_BLOCK == 0
    pos = positions.reshape(T, 1).astype(jnp.int32)
    tabs_mla = _rope_tables(pos, *_rope_patterns(LANES, HEAD_DIM, MLA_ROPE_DIM // 2))
    tabs_moba = _rope_tables(pos, *_rope_patterns(HEAD_DIM, 0, HEAD_DIM // 2))
    x = x.reshape(T, D)
    for l in range(depth):
        x = _token_mixers(x, B, S, g_mix[l], w_in[l], g_q_lat[l], g_kv_lat[l], w_uq[l], w_ukv[l],
                          b_forget[l], w_branch[l], w_out[l], tabs_mla, tabs_moba)
        if l % 2 == 0:
            x = _dense_ffn(x, g_ffn[l][None, :], w_dense_gu[l // 2].astype(BF16),
                           w_dense_down[l // 2].astype(BF16))
        else:
            x = _moe(x, g_ffn[l], w_router[l // 2], w_exp_gu[l // 2], w_exp_down[l // 2], g_final)
    return x.reshape(B, S, D)
```

```python
import functools
import math

import jax
import jax.numpy as jnp
import numpy as np
from jax import lax
from jax.experimental import pallas as pl
from jax.experimental.pallas import tpu as pltpu

F32 = jnp.float32
BF16 = jnp.bfloat16
NEG_INF = float("-inf")
M_INIT = -1e30
LOG2E = math.log2(math.e)

D_MODEL = 1024
RMS_EPS = 1e-6
ROPE_THETA = 10000.0
HEADS = 8
HEAD_DIM = 64
V_ROWS = 80
Q_SCALE = HEAD_DIM ** -0.5 * math.log2(math.e)
MLA_Q_LORA = 256
MLA_KV_LORA = 128
MLA_ROPE_DIM = 32
BRANCH_WIDTH = HEADS * HEAD_DIM
MOBA_BLOCK = 256
MOBA_TOPK = 3
DENSE_FF = 2816
N_EXPERTS = 8
EXPERT_FF = 3584

LANES = 128
VMEM_LIMIT = 48 * 1024 * 1024

BIG_COLS = 6 * BRANCH_WIDTH + 3 * D_MODEL
BIG_TN = 512
FQ_TILE, MQ_TILE, MK_TILE = 0, 3, 4
SMALL_COLS = MLA_Q_LORA + MLA_KV_LORA + 2 * LANES


def _cparams(sem):
    return pltpu.CompilerParams(dimension_semantics=sem, vmem_limit_bytes=VMEM_LIMIT)


def _rms(x, g):
    return x * lax.rsqrt(jnp.mean(x * x, axis=-1, keepdims=True) + RMS_EPS) * g


def _rope_table_kernel(pos_ref, f_ref, mc_ref, m1_ref, m2_ref, c_ref, s1_ref, s2_ref):
    ang = pos_ref[...].astype(F32) * f_ref[...]
    cos = jnp.cos(ang)
    sin = jnp.sin(ang)
    mc = mc_ref[...]
    c_ref[...] = cos * mc + (1.0 - mc)
    s1_ref[...] = sin * m1_ref[...]
    s2_ref[...] = sin * m2_ref[...]


def _rope_tables(pos, freq, mc, m1, m2, tm=1024):
    T = pos.shape[0]
    row = pl.BlockSpec((tm, 1), lambda i: (i, 0))
    pat = pl.BlockSpec((1, LANES), lambda i: (0, 0))
    out = pl.BlockSpec((tm, LANES), lambda i: (i, 0))
    shp = jax.ShapeDtypeStruct((T, LANES), F32)
    return pl.pallas_call(
        _rope_table_kernel, grid=(T // tm,),
        in_specs=[row, pat, pat, pat, pat], out_specs=[out, out, out],
        out_shape=[shp, shp, shp], compiler_params=_cparams(("parallel",)),
        name="rope_tables",
    )(pos, freq, mc, m1, m2)


def _rope_patterns(group, x1_lo, half):
    d = 2 * half
    inv_freq = jnp.exp(-math.log(ROPE_THETA) * jnp.arange(half, dtype=F32) * 2.0 / d)
    lane = np.arange(LANES) % group
    in_x1 = (lane >= x1_lo) & (lane < x1_lo + half)
    in_x2 = (lane >= x1_lo + half) & (lane < x1_lo + d)
    k = np.where(in_x1, lane - x1_lo, np.where(in_x2, lane - x1_lo - half, 0))
    freq = jnp.where(jnp.asarray(in_x1 | in_x2), inv_freq[k], 0.0)[None, :].astype(F32)
    mc = jnp.asarray((in_x1 | in_x2).astype(np.float32))[None, :]
    m1 = jnp.asarray(-(in_x1.astype(np.float32)))[None, :]
    m2 = jnp.asarray(in_x2.astype(np.float32))[None, :]
    return freq, mc, m1, m2


def _apply_rope(x, c, s1, s2, half):
    n = x.shape[-1]
    reps = n // LANES
    c, s1, s2 = (jnp.tile(t, (1, reps)) if reps > 1 else t for t in (c, s1, s2))
    return x * c + pltpu.roll(x, n - half, 1) * s1 + pltpu.roll(x, half, 1) * s2


def _inproj_kernel(x_ref, g_ref, w_ref, c_ref, s1_ref, s2_ref, o_ref, h_ref):
    j = pl.program_id(1)

    @pl.when(j == 0)
    def _():
        h_ref[...] = _rms(x_ref[...], g_ref[...]).astype(BF16)

    acc = jnp.dot(h_ref[...], w_ref[...], preferred_element_type=F32)
    acc = acc * jnp.where(jnp.logical_or(j == FQ_TILE, j == MQ_TILE), Q_SCALE, 1.0)
    is_rope = jnp.logical_or(j == MQ_TILE, j == MK_TILE)

    @pl.when(is_rope)
    def _():
        o_ref[...] = _apply_rope(acc, c_ref[...], s1_ref[...], s2_ref[...], HEAD_DIM // 2).astype(BF16)

    @pl.when(jnp.logical_not(is_rope))
    def _():
        o_ref[...] = acc.astype(BF16)


def _inproj(x, g, w_big, tabs, tm=512):
    T = x.shape[0]
    tab = pl.BlockSpec((tm, LANES), lambda i, j: (i, 0))
    return pl.pallas_call(
        _inproj_kernel, grid=(T // tm, BIG_COLS // BIG_TN),
        in_specs=[pl.BlockSpec((tm, D_MODEL), lambda i, j: (i, 0)),
                  pl.BlockSpec((1, D_MODEL), lambda i, j: (0, 0)),
                  pl.BlockSpec((D_MODEL, BIG_TN), lambda i, j: (0, j)),
                  tab, tab, tab],
        out_specs=pl.BlockSpec((tm, BIG_TN), lambda i, j: (i, j)),
        out_shape=jax.ShapeDtypeStruct((T, BIG_COLS), BF16),
        scratch_shapes=[pltpu.VMEM((tm, D_MODEL), BF16)],
        compiler_params=_cparams(("parallel", "arbitrary")),
        name="inproj",
    )(x, g, w_big, *tabs)


def _mla_prep_kernel(x_ref, g_ref, ws_ref, gq_ref, gkv_ref, wq_ref, wk_ref, wv_ref,
                     c_ref, s1_ref, s2_ref, q_ref, k_ref, v_ref, fl_ref, *, scale):
    h = _rms(x_ref[...], g_ref[...]).astype(BF16)
    small = jnp.dot(h, ws_ref[...], preferred_element_type=F32)
    c_q = small[:, :MLA_Q_LORA]
    c_kv = small[:, MLA_Q_LORA:MLA_Q_LORA + MLA_KV_LORA]
    k_pe = small[:, MLA_Q_LORA + MLA_KV_LORA:MLA_Q_LORA + MLA_KV_LORA + LANES]
    fl_ref[...] = small[:, SMALL_COLS - LANES:SMALL_COLS - LANES + HEADS]
    c, s1, s2 = c_ref[...], s1_ref[...], s2_ref[...]
    half = MLA_ROPE_DIM // 2
    qn = _rms(c_q, gq_ref[...]).astype(BF16)
    q = jnp.dot(qn, wq_ref[...], preferred_element_type=F32) * scale
    q_ref[...] = _apply_rope(q, c, s1, s2, half).astype(BF16)
    kvn = _rms(c_kv, gkv_ref[...]).astype(BF16)
    k_nope = jnp.dot(kvn, wk_ref[...], preferred_element_type=F32)
    k_rot = _apply_rope(k_pe, c, s1, s2, half)
    k_ref[...] = (k_nope + jnp.tile(k_rot, (1, HEADS))).astype(BF16)
    v_ref[...] = jnp.dot(kvn, wv_ref[...], preferred_element_type=F32).astype(BF16)


def _mla_prep(x, g, w_small, g_q, g_kv, wq, wk, wv, tabs, scale, tm=512):
    T = x.shape[0]
    full = lambda shape: pl.BlockSpec(shape, lambda i: (0,) * len(shape))
    row = lambda n: pl.BlockSpec((tm, n), lambda i: (i, 0))
    qk = HEADS * LANES
    return pl.pallas_call(
        functools.partial(_mla_prep_kernel, scale=scale), grid=(T // tm,),
        in_specs=[row(D_MODEL), full((1, D_MODEL)), full((D_MODEL, SMALL_COLS)),
                  full((1, MLA_Q_LORA)), full((1, MLA_KV_LORA)),
                  full((MLA_Q_LORA, qk)), full((MLA_KV_LORA, qk)), full((MLA_KV_LORA, BRANCH_WIDTH)),
                  row(LANES), row(LANES), row(LANES)],
        out_specs=[row(qk), row(qk), row(BRANCH_WIDTH), row(HEADS)],
        out_shape=[jax.ShapeDtypeStruct((T, qk), BF16), jax.ShapeDtypeStruct((T, qk), BF16),
                   jax.ShapeDtypeStruct((T, BRANCH_WIDTH), BF16), jax.ShapeDtypeStruct((T, HEADS), F32)],
        compiler_params=_cparams(("parallel",)),
        name="mla_prep",
    )(x, g, w_small, g_q, g_kv, wq, wk, wv, *tabs)


def _cumlogf_kernel(fl_ref, b_ref, c_ref):
    z = fl_ref[0] + b_ref[...]
    x = jnp.minimum(z, 0.0) - jnp.log1p(jnp.exp(-jnp.abs(z)))
    n = x.shape[-1]
    lane = lax.broadcasted_iota(jnp.int32, x.shape, 1)
    d = 1
    while d < n:
        x = x + jnp.where(lane >= d, pltpu.roll(x, d, 1), 0.0)
        d *= 2
    c_ref[0] = x


def _cumlogf(fl_t, b_col):
    B, H, S = fl_t.shape
    return pl.pallas_call(
        _cumlogf_kernel, grid=(B,),
        in_specs=[pl.BlockSpec((1, H, S), lambda b: (b, 0, 0)), pl.BlockSpec((H, 1), lambda b: (0, 0))],
        out_specs=pl.BlockSpec((1, H, S), lambda b: (b, 0, 0)),
        out_shape=jax.ShapeDtypeStruct((B, H, S), F32),
        compiler_params=_cparams(("parallel",)),
        name="cumlogf",
    )(fl_t, b_col)


def _split3(c):
    hi = c.astype(BF16)
    r = c - hi.astype(F32)
    mid = r.astype(BF16)
    lo = (r - mid.astype(F32)).astype(BF16)
    return hi.astype(F32), mid.astype(F32), lo.astype(F32)


def _fox_prep_kernel(q_ref, k_ref, c_ref, qo_ref, ko_ref):
    tm = q_ref.shape[0]
    lane = lax.broadcasted_iota(jnp.int32, (tm, LANES), 1)
    c = c_ref[...] * LOG2E
    for hp in range(HEADS // 2):
        q2 = q_ref[:, hp * LANES:(hp + 1) * LANES].astype(F32)
        k2 = k_ref[:, hp * LANES:(hp + 1) * LANES].astype(F32)
        for hh in range(2):
            h = 2 * hp + hh
            hi, mid, lo = _split3(c[:, h:h + 1])
            aug_c = jnp.where(lane == HEAD_DIM, hi, jnp.where(lane == HEAD_DIM + 1, mid,
                              jnp.where(lane == HEAD_DIM + 2, lo, 0.0)))
            ones_a = jnp.where((lane >= HEAD_DIM) & (lane < HEAD_DIM + 3), 1.0, 0.0)
            qh = q2 if hh == 0 else pltpu.roll(q2, HEAD_DIM, 1)
            kh = k2 if hh == 0 else pltpu.roll(k2, HEAD_DIM, 1)
            q_aug = jnp.where(lane < HEAD_DIM, qh, ones_a + pltpu.roll(aug_c, 3, 1))
            k_aug = jnp.where(lane < HEAD_DIM, kh, pltpu.roll(ones_a, 3, 1) - aug_c)
            qo_ref[:, h * LANES:(h + 1) * LANES] = q_aug.astype(BF16)
            ko_ref[:, h * LANES:(h + 1) * LANES] = k_aug.astype(BF16)


def _fox_prep(big, c_rows, tm=512):
    T = big.shape[0]
    nb = BRANCH_WIDTH // BRANCH_WIDTH
    qk = HEADS * LANES
    return pl.pallas_call(
        _fox_prep_kernel, grid=(T // tm,),
        in_specs=[pl.BlockSpec((tm, BRANCH_WIDTH), lambda i: (i, 0)),
                  pl.BlockSpec((tm, BRANCH_WIDTH), lambda i: (i, nb)),
                  pl.BlockSpec((tm, HEADS), lambda i: (i, 0))],
        out_specs=[pl.BlockSpec((tm, qk), lambda i: (i, 0)), pl.BlockSpec((tm, qk), lambda i: (i, 0))],
        out_shape=[jax.ShapeDtypeStruct((T, qk), BF16), jax.ShapeDtypeStruct((T, qk), BF16)],
        compiler_params=_cparams(("parallel",)),
        name="fox_prep",
    )(big, big, c_rows)


def _nt_dot(a, b):
    return lax.dot_general(a, b, (((1,), (1,)), ((), ())), preferred_element_type=F32)


def _softmax_step(sts, m, acc, vts):
    m_new = m
    for st in sts:
        m_new = jnp.maximum(m_new, jnp.max(st, axis=0, keepdims=True))
    acc = jnp.exp2(m - m_new) * acc
    for st, vt in zip(sts, vts):
        acc = acc + jnp.dot(vt, jnp.exp2(st - m_new).astype(BF16), preferred_element_type=F32)
    return m_new, acc


def _softmax_init(tq):
    return jnp.full((1, tq), M_INIT, F32), jnp.zeros((V_ROWS, tq), F32)


def _softmax_finish(acc):
    return (acc[:HEAD_DIM] / acc[HEAD_DIM:HEAD_DIM + 1]).T


def _causal_mask(st, tk, tq, k0, q0):
    kpos = k0 + lax.broadcasted_iota(jnp.int32, (tk, tq), 0)
    qpos = q0 + lax.broadcasted_iota(jnp.int32, (tk, tq), 1)
    return jnp.where(kpos <= qpos, st, NEG_INF)


def _flash_kernel(q_ref, k_ref, vt_ref, o_ref, *, tq, tk):
    i = pl.program_id(2)
    r = tq // tk

    def step(j, carry, masked):
        out = []
        for hh in range(2):
            st = _nt_dot(k_ref[0, j, :, hh * LANES:(hh + 1) * LANES],
                         q_ref[0, :, hh * LANES:(hh + 1) * LANES])
            if masked:
                st = _causal_mask(st, tk, tq, j * tk, i * tq)
            out.append(_softmax_step([st], *carry[hh], [vt_ref[0, j, hh * V_ROWS:(hh + 1) * V_ROWS, :]]))
        return tuple(out)

    carry = (_softmax_init(tq), _softmax_init(tq))
    carry = lax.fori_loop(0, i * r, functools.partial(step, masked=False), carry)
    for d in range(r):
        carry = step(i * r + d, carry, True)
    o_ref[0] = jnp.concatenate([_softmax_finish(acc) for _, acc in carry], axis=1).astype(BF16)


def _flash(q, k, vt, tq, tk):
    B, S, _ = q.shape
    nk = S // tk
    return pl.pallas_call(
        functools.partial(_flash_kernel, tq=tq, tk=tk), grid=(B, HEADS // 2, S // tq),
        in_specs=[pl.BlockSpec((1, tq, 2 * LANES), lambda b, h, i: (b, i, h)),
                  pl.BlockSpec((1, nk, tk, 2 * LANES), lambda b, h, i: (b, 0, 0, h)),
                  pl.BlockSpec((1, nk, 2 * V_ROWS, tk), lambda b, h, i: (b, 0, h, 0))],
        out_specs=pl.BlockSpec((1, tq, LANES), lambda b, h, i: (b, i, h)),
        out_shape=jax.ShapeDtypeStruct((B, S, BRANCH_WIDTH), BF16),
        compiler_params=_cparams(("parallel", "parallel", "arbitrary")),
        name="flash",
    )(q, k, vt)


def _moba_kernel(q_ref, k_ref, vt_ref, o_ref, kmean_ref, bias_ref, *, nblk, tq):
    i = pl.program_id(2)
    blk = MOBA_BLOCK
    shift = blk.bit_length() - 1
    r = tq // blk

    @pl.when(i == 0)
    def _():
        for n in range(nblk):
            kmean_ref[n:n + 1, :] = jnp.mean(k_ref[0, n].astype(F32), axis=0, keepdims=True)

    lane = lax.broadcasted_iota(jnp.int32, (tq, LANES), 1)
    blk_id = lax.broadcasted_iota(jnp.int32, (nblk, tq), 0)
    own = i * r + (lax.broadcasted_iota(jnp.int32, (nblk, tq), 1) >> shift)
    q2 = q_ref[0]
    qs = []
    for hh in range(2):
        in_head = (lane >= hh * HEAD_DIM) & (lane < (hh + 1) * HEAD_DIM)
        q = jnp.where(in_head, q2, jnp.zeros_like(q2))
        qs.append(q)
        g = lax.dot_general(kmean_ref[...], q.astype(F32), (((1,), (1,)), ((), ())),
                            precision=lax.Precision.HIGHEST, preferred_element_type=F32)
        g = jnp.where(blk_id < own, g, NEG_INF)
        bias = jnp.full((nblk, tq), NEG_INF, F32)
        for _ in range(MOBA_TOPK):
            mx = jnp.max(g, axis=0, keepdims=True)
            first = jnp.min(jnp.where(g == mx, blk_id, nblk), axis=0, keepdims=True)
            pick = (blk_id == first) & (mx > NEG_INF)
            bias = jnp.where(pick, 0.0, bias)
            g = jnp.where(pick, NEG_INF, g)
        bias_ref[hh] = bias

    def scores(hh, n):
        return _nt_dot(k_ref[0, n], qs[hh]), bias_ref[hh, pl.ds(n, 1), :]

    def values(hh, n):
        return vt_ref[0, n, hh * V_ROWS:(hh + 1) * V_ROWS, :]

    def gated(p, carry):
        out = []
        for hh in range(2):
            sts = [sum(scores(hh, 2 * p + d)) for d in range(2)]
            out.append(_softmax_step(sts, *carry[hh], [values(hh, 2 * p + d) for d in range(2)]))
        return tuple(out)

    carry = (_softmax_init(tq), _softmax_init(tq))
    carry = lax.fori_loop(0, i * (r // 2), gated, carry)

    krow = lax.broadcasted_iota(jnp.int32, (blk, tq), 0)
    qcol = lax.broadcasted_iota(jnp.int32, (blk, tq), 1)
    out = []
    for hh in range(2):
        sts = []
        for d in range(r):
            st, b = scores(hh, i * r + d)
            own_causal = ((qcol >> shift) == d) & (krow <= (qcol & (blk - 1)))
            sts.append(jnp.where(own_causal, st, st + b))
        _, acc = _softmax_step(sts, *carry[hh], [values(hh, i * r + d) for d in range(r)])
        out.append(_softmax_finish(acc))
    o_ref[0] = jnp.concatenate(out, axis=1).astype(BF16)


def _moba(q, k, vt, tq):
    B, S, _ = q.shape
    nblk = S // MOBA_BLOCK
    assert tq % (2 * MOBA_BLOCK) == 0 and MOBA_BLOCK & (MOBA_BLOCK - 1) == 0
    qc = MQ_TILE * BIG_TN // LANES
    kc = MK_TILE * BIG_TN // LANES
    return pl.pallas_call(
        functools.partial(_moba_kernel, nblk=nblk, tq=tq), grid=(B, HEADS // 2, S // tq),
        in_specs=[pl.BlockSpec((1, tq, LANES), lambda b, h, i: (b, i, qc + h)),
                  pl.BlockSpec((1, nblk, MOBA_BLOCK, LANES), lambda b, h, i: (b, 0, 0, kc + h)),
                  pl.BlockSpec((1, nblk, 2 * V_ROWS, MOBA_BLOCK), lambda b, h, i: (b, 0, h, 0))],
        out_specs=pl.BlockSpec((1, tq, LANES), lambda b, h, i: (b, i, h)),
        out_shape=jax.ShapeDtypeStruct((B, S, BRANCH_WIDTH), BF16),
        scratch_shapes=[pltpu.VMEM((nblk, LANES), F32), pltpu.VMEM((2, nblk, tq), F32)],
        compiler_params=_cparams(("parallel", "parallel", "arbitrary")),
        name="moba",
    )(q, k, vt)


def _merge_kernel(x_ref, ya_ref, yb_ref, yc_ref, ga_ref, gb_ref, gc_ref, wb_ref, wo_ref, o_ref):
    merged = None
    for n, (y_ref, g_ref) in enumerate(((ya_ref, ga_ref), (yb_ref, gb_ref), (yc_ref, gc_ref))):
        proj = jnp.dot(y_ref[...], wb_ref[n], preferred_element_type=F32)
        term = jax.nn.sigmoid(g_ref[...].astype(F32)) * proj
        merged = term if merged is None else merged + term
    o_ref[...] = x_ref[...] + jnp.dot(merged.astype(BF16), wo_ref[...], preferred_element_type=F32)


def _merge(x, y_mla, y_fox, y_moba, big, w_branch, w_out, tm=256):
    T = x.shape[0]
    g0 = 6 * BRANCH_WIDTH // D_MODEL
    row = lambda n: pl.BlockSpec((tm, n), lambda i: (i, 0))
    gate = lambda n: pl.BlockSpec((tm, D_MODEL), lambda i: (i, g0 + n))
    return pl.pallas_call(
        _merge_kernel, grid=(T // tm,),
        in_specs=[row(D_MODEL), row(BRANCH_WIDTH), row(BRANCH_WIDTH), row(BRANCH_WIDTH),
                  gate(0), gate(1), gate(2),
                  pl.BlockSpec((3, BRANCH_WIDTH, D_MODEL), lambda i: (0, 0, 0)),
                  pl.BlockSpec((D_MODEL, D_MODEL), lambda i: (0, 0))],
        out_specs=row(D_MODEL),
        out_shape=jax.ShapeDtypeStruct((T, D_MODEL), F32),
        compiler_params=_cparams(("parallel",)),
        name="merge",
    )(x, y_mla, y_fox, y_moba, big, big, big, w_branch, w_out)


def _dense_ffn_kernel(x_ref, g_ref, wg_ref, wu_ref, wd_ref, o_ref, h_ref, acc_ref):
    f = pl.program_id(1)

    @pl.when(f == 0)
    def _():
        h_ref[...] = _rms(x_ref[...], g_ref[...]).astype(BF16)
        acc_ref[...] = x_ref[...]

    h = h_ref[...]
    gate = jnp.dot(h, wg_ref[...], preferred_element_type=F32)
    up = jnp.dot(h, wu_ref[...], preferred_element_type=F32)
    act = (jax.nn.silu(gate) * up).astype(BF16)
    acc_ref[...] += jnp.dot(act, wd_ref[...], preferred_element_type=F32)

    @pl.when(f == pl.num_programs(1) - 1)
    def _():
        o_ref[...] = acc_ref[...]


def _dense_ffn(x, g, w_gu, w_down, tm=512, tf=256):
    T = x.shape[0]
    nf = DENSE_FF // tf
    return pl.pallas_call(
        _dense_ffn_kernel, grid=(T // tm, nf),
        in_specs=[pl.BlockSpec((tm, D_MODEL), lambda i, f: (i, 0)),
                  pl.BlockSpec((1, D_MODEL), lambda i, f: (0, 0)),
                  pl.BlockSpec((D_MODEL, tf), lambda i, f: (0, f)),
                  pl.BlockSpec((D_MODEL, tf), lambda i, f: (0, nf + f)),
                  pl.BlockSpec((tf, D_MODEL), lambda i, f: (f, 0))],
        out_specs=pl.BlockSpec((tm, D_MODEL), lambda i, f: (i, 0)),
        out_shape=jax.ShapeDtypeStruct((T, D_MODEL), F32),
        scratch_shapes=[pltpu.VMEM((tm, D_MODEL), BF16), pltpu.VMEM((tm, D_MODEL), F32)],
        compiler_params=_cparams(("parallel", "arbitrary")),
        name="dense_ffn",
    )(x, g, w_gu, w_gu, w_down)


def _router_kernel(x_ref, g_ref, wr_ref, h_ref, r_ref):
    h = _rms(x_ref[...], g_ref[...])
    h_ref[...] = h
    logits = jnp.dot(h, wr_ref[...], precision=lax.Precision.HIGHEST, preferred_element_type=F32)
    lane = lax.broadcasted_iota(jnp.int32, logits.shape, 1)
    logits = jnp.where(lane < N_EXPERTS, logits, NEG_INF)
    m1 = jnp.max(logits, axis=-1, keepdims=True)
    i1 = jnp.min(jnp.where(logits == m1, lane, LANES), axis=-1, keepdims=True)
    rest = jnp.where(lane == i1, NEG_INF, logits)
    m2 = jnp.max(rest, axis=-1, keepdims=True)
    i2 = jnp.min(jnp.where(rest == m2, lane, LANES), axis=-1, keepdims=True)
    e2 = jnp.exp(m2 - m1)
    w1 = 1.0 / (1.0 + e2)
    w2 = e2 / (1.0 + e2)
    r_ref[...] = jnp.where(lane == 0, i1.astype(F32), jnp.where(lane == 1, i2.astype(F32),
                           jnp.where(lane == 2, w1, jnp.where(lane == 3, w2, 0.0))))


def _router(x, g, w_router_pad, tm=512):
    T = x.shape[0]
    return pl.pallas_call(
        _router_kernel, grid=(T // tm,),
        in_specs=[pl.BlockSpec((tm, D_MODEL), lambda i: (i, 0)),
                  pl.BlockSpec((1, D_MODEL), lambda i: (0, 0)),
                  pl.BlockSpec((D_MODEL, LANES), lambda i: (0, 0))],
        out_specs=[pl.BlockSpec((tm, D_MODEL), lambda i: (i, 0)), pl.BlockSpec((tm, LANES), lambda i: (i, 0))],
        out_shape=[jax.ShapeDtypeStruct((T, D_MODEL), F32), jax.ShapeDtypeStruct((T, LANES), F32)],
        compiler_params=_cparams(("parallel",)),
        name="router",
    )(x, g, w_router_pad)


def _gather_kernel(idx_ref, src_ref, o_ref, sem, *, bm):
    def row_copy(r):
        return pltpu.make_async_copy(src_ref.at[pl.ds(idx_ref[0, 0, r], 1), :], o_ref.at[pl.ds(r, 1), :], sem)

    def start(r, c):
        row_copy(r).start()
        return c

    def wait(r, c):
        row_copy(r).wait()
        return c

    lax.fori_loop(0, bm, start, 0)
    lax.fori_loop(0, bm, wait, 0)


def _gather_rows(src, idx, bm=256):
    M = idx.shape[0]
    C = src.shape[1]
    idx3 = idx.reshape(M // bm, 1, bm)
    return pl.pallas_call(
        functools.partial(_gather_kernel, bm=bm), grid=(M // bm,),
        in_specs=[pl.BlockSpec((1, 1, bm), lambda i: (i, 0, 0), memory_space=pltpu.SMEM),
                  pl.BlockSpec(memory_space=pl.ANY)],
        out_specs=pl.BlockSpec((bm, C), lambda i: (i, 0)),
        out_shape=jax.ShapeDtypeStruct((M, C), src.dtype),
        scratch_shapes=[pltpu.SemaphoreType.DMA(())],
        compiler_params=_cparams(("arbitrary",)),
        name="gather_rows",
    )(idx3, src)


def _expert_kernel(be_ref, nused_ref, x_ref, wg_ref, wu_ref, wd_ref, o_ref, acc_ref):
    b = pl.program_id(0)
    f = pl.program_id(1)
    used = b < nused_ref[0]

    @pl.when(used)
    def _():
        x = x_ref[...].astype(BF16)
        gate = jnp.dot(x, wg_ref[0], preferred_element_type=F32)
        up = jnp.dot(x, wu_ref[0], preferred_element_type=F32)
        act = (jax.nn.silu(gate) * up).astype(BF16)
        y = jnp.dot(act, wd_ref[0], preferred_element_type=F32)

        @pl.when(f == 0)
        def _():
            acc_ref[...] = y

        @pl.when(f > 0)
        def _():
            acc_ref[...] += y

    @pl.when(f == pl.num_programs(1) - 1)
    def _():
        o_ref[...] = jnp.where(used, acc_ref[...], 0.0)


def _experts(x_rows, block_e, n_used, w_gu, w_down, bm, tf=512):
    n_rows = x_rows.shape[0]
    nf = EXPERT_FF // tf
    nb = n_rows // bm

    def fsel(b, f, nused):
        return jnp.where(b < nused[0], f, nf - 1)

    grid_spec = pltpu.PrefetchScalarGridSpec(
        num_scalar_prefetch=2, grid=(nb, nf),
        in_specs=[pl.BlockSpec((bm, D_MODEL), lambda b, f, be, nu: (b, 0)),
                  pl.BlockSpec((1, D_MODEL, tf), lambda b, f, be, nu: (be[b], 0, fsel(b, f, nu))),
                  pl.BlockSpec((1, D_MODEL, tf), lambda b, f, be, nu: (be[b], 0, nf + fsel(b, f, nu))),
                  pl.BlockSpec((1, tf, D_MODEL), lambda b, f, be, nu: (be[b], fsel(b, f, nu), 0))],
        out_specs=pl.BlockSpec((bm, D_MODEL), lambda b, f, be, nu: (b, 0)),
        scratch_shapes=[pltpu.VMEM((bm, D_MODEL), F32)])
    return pl.pallas_call(
        _expert_kernel, grid_spec=grid_spec,
        out_shape=jax.ShapeDtypeStruct((n_rows, D_MODEL), F32),
        compiler_params=_cparams(("arbitrary", "arbitrary")),
        name="experts",
    )(block_e, n_used, x_rows, w_gu, w_gu, w_down)


def _combine_kernel(x_ref, y_ref, r_ref, g_ref, o_ref):
    r = r_ref[...]
    y = y_ref[...]
    x = x_ref[...] + r[:, 2:3] * y[:, :D_MODEL] + r[:, 3:4] * y[:, D_MODEL:]
    o_ref[...] = _rms(x, g_ref[...])


def _combine(x, y_pairs, route, g_final, tm=256):
    T = x.shape[0]
    return pl.pallas_call(
        _combine_kernel, grid=(T // tm,),
        in_specs=[pl.BlockSpec((tm, D_MODEL), lambda i: (i, 0)),
                  pl.BlockSpec((tm, 2 * D_MODEL), lambda i: (i, 0)),
                  pl.BlockSpec((tm, LANES), lambda i: (i, 0)),
                  pl.BlockSpec((1, D_MODEL), lambda i: (0, 0))],
        out_specs=pl.BlockSpec((tm, D_MODEL), lambda i: (i, 0)),
        out_shape=jax.ShapeDtypeStruct((T, D_MODEL), F32),
        compiler_params=_cparams(("parallel",)),
        name="combine",
    )(x, y_pairs, route, g_final)


def _pad_cols(w, n):
    return jnp.pad(w, ((0, 0), (0, n - w.shape[1])))


def _split_in_weights(w_in):
    sizes = [MLA_Q_LORA, MLA_KV_LORA, MLA_ROPE_DIM, BRANCH_WIDTH, BRANCH_WIDTH, BRANCH_WIDTH, HEADS,
             BRANCH_WIDTH, BRANCH_WIDTH, BRANCH_WIDTH, 3 * D_MODEL]
    pts = np.cumsum(sizes)[:-1]
    c_q, c_kv, k_pe, fq, fk, fv, fl, mq, mk, mv, gates = jnp.split(w_in, pts, axis=1)
    zeros = lambda n: jnp.zeros((D_MODEL, n), w_in.dtype)
    pe_tile = jnp.concatenate([zeros(HEAD_DIM), k_pe, zeros(LANES - HEAD_DIM - MLA_ROPE_DIM)], axis=1)
    w_small = jnp.concatenate([c_q, c_kv, pe_tile, _pad_cols(fl, LANES)], axis=1)
    w_big = jnp.concatenate([fq, fk, fv, mq, mk, mv, gates], axis=1)
    return w_small.astype(BF16), w_big.astype(BF16)


def _mla_up_weights(w_uq, w_ukv):
    dq = HEAD_DIM + MLA_ROPE_DIM
    wq = w_uq.reshape(MLA_Q_LORA, HEADS, dq)
    wq = jnp.pad(wq, ((0, 0), (0, 0), (0, LANES - dq))).reshape(MLA_Q_LORA, HEADS * LANES)
    wkv = w_ukv.reshape(MLA_KV_LORA, HEADS, 2 * HEAD_DIM)
    wk = jnp.pad(wkv[:, :, :HEAD_DIM], ((0, 0), (0, 0), (0, LANES - HEAD_DIM))).reshape(MLA_KV_LORA, HEADS * LANES)
    wv = wkv[:, :, HEAD_DIM:].reshape(MLA_KV_LORA, BRANCH_WIDTH)
    return wq.astype(BF16), wk.astype(BF16), wv.astype(BF16)


def _chunk_rows(a, B, S, t):
    return a.reshape(B, S // t, t, a.shape[-1])


def _value_tiles(v, B, S, t):
    v = v.reshape(B, S // t, t, HEADS, HEAD_DIM)
    pad = jnp.zeros(v.shape[:-1] + (V_ROWS - HEAD_DIM,), v.dtype).at[..., 0].set(1)
    return jnp.concatenate([v, pad], axis=-1).reshape(B, S // t, t, HEADS * V_ROWS).transpose(0, 1, 3, 2)


ATT_TQ = 512
ATT_TK = 512
MOBA_TQ = 512
MOE_BM = 512


def _token_mixers(x, B, S, g_mix, w_in, g_q_lat, g_kv_lat, w_uq, w_ukv, b_forget, w_branch, w_out,
                  tabs_mla, tabs_moba):
    T = B * S
    w_small, w_big = _split_in_weights(w_in)
    wq, wk, wv = _mla_up_weights(w_uq, w_ukv)
    g = g_mix[None, :]

    big = _inproj(x, g, w_big, tabs_moba)
    q_mla, k_mla, v_mla, f_logit = _mla_prep(
        x, g, w_small, g_q_lat[None, :], g_kv_lat[None, :], wq, wk, wv, tabs_mla,
        (HEAD_DIM + MLA_ROPE_DIM) ** -0.5 * LOG2E)

    y_mla = _flash(q_mla.reshape(B, S, -1), _chunk_rows(k_mla, B, S, ATT_TK),
                   _value_tiles(v_mla, B, S, ATT_TK), ATT_TQ, ATT_TK)

    c = _cumlogf(f_logit.reshape(B, S, HEADS).transpose(0, 2, 1), b_forget[:, None])
    c_rows = c.transpose(0, 2, 1).reshape(T, HEADS)
    q_fox, k_fox = _fox_prep(big, c_rows)
    fv = big[:, 2 * BRANCH_WIDTH:3 * BRANCH_WIDTH]
    y_fox = _flash(q_fox.reshape(B, S, -1), _chunk_rows(k_fox, B, S, ATT_TK),
                   _value_tiles(fv, B, S, ATT_TK), ATT_TQ, ATT_TK)

    mv = big[:, 5 * BRANCH_WIDTH:6 * BRANCH_WIDTH]
    y_moba = _moba(big.reshape(B, S, -1), _chunk_rows(big, B, S, MOBA_BLOCK),
                   _value_tiles(mv, B, S, MOBA_BLOCK), MOBA_TQ)

    return _merge(x, y_mla.reshape(T, -1), y_fox.reshape(T, -1), y_moba.reshape(T, -1), big,
                  w_branch.astype(BF16), w_out.astype(BF16))


def _moe(x, g_ffn, w_router, w_gu, w_down, g_final):
    T = x.shape[0]
    A = 2 * T
    bm = MOE_BM
    h, route = _router(x, g_ffn[None, :], _pad_cols(w_router, LANES))
    top_e = route[:, :2].astype(jnp.int32).reshape(A)
    onehot = (top_e[:, None] == jnp.arange(N_EXPERTS)[None, :]).astype(jnp.int32)
    csum = jnp.cumsum(onehot, axis=0)
    counts = csum[-1]
    rank = jnp.sum((csum - onehot) * onehot, axis=1)
    padded = (counts + bm - 1) // bm * bm
    pad_end = jnp.cumsum(padded)
    dest = (pad_end - padded)[top_e] + rank
    n_rows = (A // bm + N_EXPERTS) * bm
    row_tok = jnp.zeros((n_rows,), jnp.int32).at[dest].set(jnp.arange(A, dtype=jnp.int32) // 2)
    block_e = jnp.minimum(jnp.searchsorted(pad_end, jnp.arange(n_rows // bm) * bm, side="right"),
                          N_EXPERTS - 1).astype(jnp.int32)
    n_used = (pad_end[-1:] // bm).astype(jnp.int32)

    x_rows = _gather_rows(h, row_tok)
    y_rows = _experts(x_rows, block_e, n_used, w_gu.astype(BF16), w_down.astype(BF16), bm)
    y_pairs = _gather_rows(y_rows, dest.astype(jnp.int32)).reshape(T, 2 * D_MODEL)
    return _combine(x, y_pairs, route, g_final[None, :])


def kernel(x, positions, g_mix, w_in, g_q_lat, g_kv_lat, w_uq, w_ukv, b_forget, w_branch, w_out, g_ffn,
           w_dense_gu, w_dense_down, w_router, w_exp_gu, w_exp_down, g_final):
    B, S, D = x.shape
    T = B * S
    depth = g_mix.shape[0]
    assert depth == 2 and D == D_MODEL and S % MOBA_BLOCK == 0
    pos = positions.reshape(T, 1).astype(jnp.int32)
    tabs_mla = _rope_tables(pos, *_rope_patterns(LANES, HEAD_DIM, MLA_ROPE_DIM // 2))
    tabs_moba = _rope_tables(pos, *_rope_patterns(HEAD_DIM, 0, HEAD_DIM // 2))
    x = x.reshape(T, D)
    for l in range(depth):
        x = _token_mixers(x, B, S, g_mix[l], w_in[l], g_q_lat[l], g_kv_lat[l], w_uq[l], w_ukv[l],
                          b_forget[l], w_branch[l], w_out[l], tabs_mla, tabs_moba)
        if l % 2 == 0:
            x = _dense_ffn(x, g_ffn[l][None, :], w_dense_gu[l // 2].astype(BF16),
                           w_dense_down[l // 2].astype(BF16))
        else:
            x = _moe(x, g_ffn[l], w_router[l // 2], w_exp_gu[l // 2], w_exp_down[l // 2], g_final)
    return x.reshape(B, S, D)
```

```python
import functools
import math

import jax
import jax.numpy as jnp
import numpy as np
from jax import lax
from jax.experimental import pallas as pl
from jax.experimental.pallas import tpu as pltpu

F32 = jnp.float32
BF16 = jnp.bfloat16
NEG_INF = float("-inf")
M_INIT = -1e30
LOG2E = math.log2(math.e)

D_MODEL = 1024
RMS_EPS = 1e-6
ROPE_THETA = 10000.0
HEADS = 8
HEAD_DIM = 64
V_ROWS = 80
Q_SCALE = HEAD_DIM ** -0.5 * math.log2(math.e)
MLA_Q_LORA = 256
MLA_KV_LORA = 128
MLA_ROPE_DIM = 32
BRANCH_WIDTH = HEADS * HEAD_DIM
MOBA_BLOCK = 256
MOBA_TOPK = 3
DENSE_FF = 2816
N_EXPERTS = 8
EXPERT_FF = 3584

LANES = 128
VMEM_LIMIT = 48 * 1024 * 1024

COL_FQ, COL_MQ, COL_FK, COL_MK, COL_FV, COL_MV, COL_GATES = (n * BRANCH_WIDTH for n in range(7))
BIG_COLS = COL_GATES + 3 * D_MODEL
BIG_TN = 2 * BRANCH_WIDTH
SMALL_COLS = MLA_Q_LORA + MLA_KV_LORA + 2 * LANES


def _cparams(sem):
    return pltpu.CompilerParams(dimension_semantics=sem, vmem_limit_bytes=VMEM_LIMIT)


def _rms(x, g):
    return x * lax.rsqrt(jnp.mean(x * x, axis=-1, keepdims=True) + RMS_EPS) * g


def _rope_table_kernel(pos_ref, f_ref, mc_ref, m1_ref, m2_ref, c_ref, s1_ref, s2_ref):
    ang = pos_ref[...].astype(F32) * f_ref[...]
    cos = jnp.cos(ang)
    sin = jnp.sin(ang)
    mc = mc_ref[...]
    c_ref[...] = cos * mc + (1.0 - mc)
    s1_ref[...] = sin * m1_ref[...]
    s2_ref[...] = sin * m2_ref[...]


def _rope_tables(pos, freq, mc, m1, m2, tm=1024):
    T = pos.shape[0]
    row = pl.BlockSpec((tm, 1), lambda i: (i, 0))
    pat = pl.BlockSpec((1, LANES), lambda i: (0, 0))
    out = pl.BlockSpec((tm, LANES), lambda i: (i, 0))
    shp = jax.ShapeDtypeStruct((T, LANES), F32)
    return pl.pallas_call(
        _rope_table_kernel, grid=(T // tm,),
        in_specs=[row, pat, pat, pat, pat], out_specs=[out, out, out],
        out_shape=[shp, shp, shp], compiler_params=_cparams(("parallel",)),
        name="rope_tables",
    )(pos, freq, mc, m1, m2)


def _rope_patterns(group, x1_lo, half):
    d = 2 * half
    inv_freq = jnp.exp(-math.log(ROPE_THETA) * jnp.arange(half, dtype=F32) * 2.0 / d)
    lane = np.arange(LANES) % group
    in_x1 = (lane >= x1_lo) & (lane < x1_lo + half)
    in_x2 = (lane >= x1_lo + half) & (lane < x1_lo + d)
    k = np.where(in_x1, lane - x1_lo, np.where(in_x2, lane - x1_lo - half, 0))
    freq = jnp.where(jnp.asarray(in_x1 | in_x2), inv_freq[k], 0.0)[None, :].astype(F32)
    mc = jnp.asarray((in_x1 | in_x2).astype(np.float32))[None, :]
    m1 = jnp.asarray(-(in_x1.astype(np.float32)))[None, :]
    m2 = jnp.asarray(in_x2.astype(np.float32))[None, :]
    return freq, mc, m1, m2


def _apply_rope(x, c, s1, s2, half):
    n = x.shape[-1]
    reps = n // LANES
    c, s1, s2 = (jnp.tile(t, (1, reps)) if reps > 1 else t for t in (c, s1, s2))
    return x * c + pltpu.roll(x, n - half, 1) * s1 + pltpu.roll(x, half, 1) * s2


def _inproj_kernel(x_ref, g_ref, w_ref, c_ref, s1_ref, s2_ref, o_ref, h_ref):
    j = pl.program_id(1)

    @pl.when(j == 0)
    def _():
        h_ref[...] = _rms(x_ref[...], g_ref[...]).astype(BF16)

    acc = jnp.dot(h_ref[...], w_ref[...], preferred_element_type=F32)

    @pl.when(j < 2)
    def _():
        a = acc * jnp.where(j == 0, Q_SCALE, 1.0)
        o_ref[:, :BRANCH_WIDTH] = a[:, :BRANCH_WIDTH].astype(BF16)
        o_ref[:, BRANCH_WIDTH:] = _apply_rope(a[:, BRANCH_WIDTH:], c_ref[...], s1_ref[...], s2_ref[...],
                                              HEAD_DIM // 2).astype(BF16)

    @pl.when(j >= 2)
    def _():
        o_ref[...] = acc.astype(BF16)


def _inproj(x, g, w_big, tabs, tm=1024):
    T = x.shape[0]
    tab = pl.BlockSpec((tm, LANES), lambda i, j: (i, 0))
    return pl.pallas_call(
        _inproj_kernel, grid=(T // tm, BIG_COLS // BIG_TN),
        in_specs=[pl.BlockSpec((tm, D_MODEL), lambda i, j: (i, 0)),
                  pl.BlockSpec((1, D_MODEL), lambda i, j: (0, 0)),
                  pl.BlockSpec((D_MODEL, BIG_TN), lambda i, j: (0, j)),
                  tab, tab, tab],
        out_specs=pl.BlockSpec((tm, BIG_TN), lambda i, j: (i, j)),
        out_shape=jax.ShapeDtypeStruct((T, BIG_COLS), BF16),
        scratch_shapes=[pltpu.VMEM((tm, D_MODEL), BF16)],
        compiler_params=_cparams(("parallel", "arbitrary")),
        name="inproj",
    )(x, g, w_big, *tabs)


def _mla_prep_kernel(x_ref, g_ref, ws_ref, gq_ref, gkv_ref, wq_ref, wk_ref, wv_ref,
                     c_ref, s1_ref, s2_ref, q_ref, k_ref, v_ref, fl_ref, *, scale):
    h = _rms(x_ref[...], g_ref[...]).astype(BF16)
    small = jnp.dot(h, ws_ref[...], preferred_element_type=F32)
    c_q = small[:, :MLA_Q_LORA]
    c_kv = small[:, MLA_Q_LORA:MLA_Q_LORA + MLA_KV_LORA]
    k_pe = small[:, MLA_Q_LORA + MLA_KV_LORA:MLA_Q_LORA + MLA_KV_LORA + LANES]
    fl_ref[...] = small[:, SMALL_COLS - LANES:SMALL_COLS - LANES + HEADS]
    c, s1, s2 = c_ref[...], s1_ref[...], s2_ref[...]
    half = MLA_ROPE_DIM // 2
    qn = _rms(c_q, gq_ref[...]).astype(BF16)
    q = jnp.dot(qn, wq_ref[...], preferred_element_type=F32) * scale
    q_ref[...] = _apply_rope(q, c, s1, s2, half).astype(BF16)
    kvn = _rms(c_kv, gkv_ref[...]).astype(BF16)
    k_nope = jnp.dot(kvn, wk_ref[...], preferred_element_type=F32)
    k_rot = _apply_rope(k_pe, c, s1, s2, half)
    k_ref[...] = (k_nope + jnp.tile(k_rot, (1, HEADS))).astype(BF16)
    v_ref[...] = jnp.dot(kvn, wv_ref[...], preferred_element_type=F32).astype(BF16)


def _mla_prep(x, g, w_small, g_q, g_kv, wq, wk, wv, tabs, scale, tm=512):
    T = x.shape[0]
    full = lambda shape: pl.BlockSpec(shape, lambda i: (0,) * len(shape))
    row = lambda n: pl.BlockSpec((tm, n), lambda i: (i, 0))
    qk = HEADS * LANES
    return pl.pallas_call(
        functools.partial(_mla_prep_kernel, scale=scale), grid=(T // tm,),
        in_specs=[row(D_MODEL), full((1, D_MODEL)), full((D_MODEL, SMALL_COLS)),
                  full((1, MLA_Q_LORA)), full((1, MLA_KV_LORA)),
                  full((MLA_Q_LORA, qk)), full((MLA_KV_LORA, qk)), full((MLA_KV_LORA, BRANCH_WIDTH)),
                  row(LANES), row(LANES), row(LANES)],
        out_specs=[row(qk), row(qk), row(BRANCH_WIDTH), row(HEADS)],
        out_shape=[jax.ShapeDtypeStruct((T, qk), BF16), jax.ShapeDtypeStruct((T, qk), BF16),
                   jax.ShapeDtypeStruct((T, BRANCH_WIDTH), BF16), jax.ShapeDtypeStruct((T, HEADS), F32)],
        compiler_params=_cparams(("parallel",)),
        name="mla_prep",
    )(x, g, w_small, g_q, g_kv, wq, wk, wv, *tabs)


def _cumlogf_kernel(fl_ref, b_ref, c_ref):
    z = fl_ref[0] + b_ref[...]
    x = jnp.minimum(z, 0.0) - jnp.log1p(jnp.exp(-jnp.abs(z)))
    n = x.shape[-1]
    lane = lax.broadcasted_iota(jnp.int32, x.shape, 1)
    d = 1
    while d < n:
        x = x + jnp.where(lane >= d, pltpu.roll(x, d, 1), 0.0)
        d *= 2
    c_ref[0] = x


def _cumlogf(fl_t, b_col):
    B, H, S = fl_t.shape
    return pl.pallas_call(
        _cumlogf_kernel, grid=(B,),
        in_specs=[pl.BlockSpec((1, H, S), lambda b: (b, 0, 0)), pl.BlockSpec((H, 1), lambda b: (0, 0))],
        out_specs=pl.BlockSpec((1, H, S), lambda b: (b, 0, 0)),
        out_shape=jax.ShapeDtypeStruct((B, H, S), F32),
        compiler_params=_cparams(("parallel",)),
        name="cumlogf",
    )(fl_t, b_col)


def _split3(c):
    hi = c.astype(BF16)
    r = c - hi.astype(F32)
    mid = r.astype(BF16)
    lo = (r - mid.astype(F32)).astype(BF16)
    return hi.astype(F32), mid.astype(F32), lo.astype(F32)


def _fox_prep_kernel(q_ref, k_ref, c_ref, qo_ref, ko_ref):
    tm = q_ref.shape[0]
    lane = lax.broadcasted_iota(jnp.int32, (tm, LANES), 1)
    c = c_ref[...] * LOG2E
    for hp in range(HEADS // 2):
        q2 = q_ref[:, hp * LANES:(hp + 1) * LANES].astype(F32)
        k2 = k_ref[:, hp * LANES:(hp + 1) * LANES].astype(F32)
        for hh in range(2):
            h = 2 * hp + hh
            hi, mid, lo = _split3(c[:, h:h + 1])
            aug_c = jnp.where(lane == HEAD_DIM, hi, jnp.where(lane == HEAD_DIM + 1, mid,
                              jnp.where(lane == HEAD_DIM + 2, lo, 0.0)))
            ones_a = jnp.where((lane >= HEAD_DIM) & (lane < HEAD_DIM + 3), 1.0, 0.0)
            qh = q2 if hh == 0 else pltpu.roll(q2, HEAD_DIM, 1)
            kh = k2 if hh == 0 else pltpu.roll(k2, HEAD_DIM, 1)
            q_aug = jnp.where(lane < HEAD_DIM, qh, ones_a + pltpu.roll(aug_c, 3, 1))
            k_aug = jnp.where(lane < HEAD_DIM, kh, pltpu.roll(ones_a, 3, 1) - aug_c)
            qo_ref[:, h * LANES:(h + 1) * LANES] = q_aug.astype(BF16)
            ko_ref[:, h * LANES:(h + 1) * LANES] = k_aug.astype(BF16)


def _fox_prep(big, c_rows, tm=512):
    T = big.shape[0]
    qk = HEADS * LANES
    return pl.pallas_call(
        _fox_prep_kernel, grid=(T // tm,),
        in_specs=[pl.BlockSpec((tm, BRANCH_WIDTH), lambda i: (i, COL_FQ // BRANCH_WIDTH)),
                  pl.BlockSpec((tm, BRANCH_WIDTH), lambda i: (i, COL_FK // BRANCH_WIDTH)),
                  pl.BlockSpec((tm, HEADS), lambda i: (i, 0))],
        out_specs=[pl.BlockSpec((tm, qk), lambda i: (i, 0)), pl.BlockSpec((tm, qk), lambda i: (i, 0))],
        out_shape=[jax.ShapeDtypeStruct((T, qk), BF16), jax.ShapeDtypeStruct((T, qk), BF16)],
        compiler_params=_cparams(("parallel",)),
        name="fox_prep",
    )(big, big, c_rows)


def _nt_dot(a, b):
    return lax.dot_general(a, b, (((1,), (1,)), ((), ())), preferred_element_type=F32)


def _softmax_step(sts, m, acc, vts):
    m_new = m
    for st in sts:
        m_new = jnp.maximum(m_new, jnp.max(st, axis=0, keepdims=True))
    acc = jnp.exp2(m - m_new) * acc
    for st, vt in zip(sts, vts):
        acc = acc + jnp.dot(vt, jnp.exp2(st - m_new).astype(BF16), preferred_element_type=F32)
    return m_new, acc


def _softmax_init(tq):
    return jnp.full((1, tq), M_INIT, F32), jnp.zeros((V_ROWS, tq), F32)


def _softmax_finish(acc):
    return (acc[:HEAD_DIM] / acc[HEAD_DIM:HEAD_DIM + 1]).T


def _causal_mask(st, tk, tq, k0, q0):
    kpos = k0 + lax.broadcasted_iota(jnp.int32, (tk, tq), 0)
    qpos = q0 + lax.broadcasted_iota(jnp.int32, (tk, tq), 1)
    return jnp.where(kpos <= qpos, st, NEG_INF)


def _attend(npairs, qk, val, past, diag, tq, sa, sb):
    heads = range(2)

    def put(dst, j):
        for hh in heads:
            dst[hh][...] = qk(hh, j)

    def advance(carry, src, j, fn):
        return tuple(_softmax_step([fn(hh, j, src[hh][...])], *carry[hh], [val(hh, j)]) for hh in heads)

    put(sa, 0)

    def body(jj, carry):
        j0 = 2 * jj
        put(sb, j0 + 1)
        carry = advance(carry, sa, j0, past)
        put(sa, j0 + 2)
        return advance(carry, sb, j0 + 1, past)

    carry = lax.fori_loop(0, npairs, body, (_softmax_init(tq), _softmax_init(tq)))
    j0 = 2 * npairs
    put(sb, j0 + 1)
    carry = advance(carry, sa, j0, lambda hh, j, st: diag(hh, 0, st))
    carry = advance(carry, sb, j0 + 1, lambda hh, j, st: diag(hh, 1, st))
    return jnp.concatenate([_softmax_finish(acc) for _, acc in carry], axis=1)


def _score_scratch(tq, tk):
    return [pltpu.VMEM((tk, tq), F32) for _ in range(4)]


def _flash_kernel(q_ref, k_ref, vt_ref, o_ref, sa0, sa1, sb0, sb1, *, tq, tk):
    i = pl.program_id(2)
    assert tq == 2 * tk
    krow = lax.broadcasted_iota(jnp.int32, (tk, tq), 0)
    qcol = lax.broadcasted_iota(jnp.int32, (tk, tq), 1)

    def qk(hh, j):
        return _nt_dot(k_ref[0, j, :, hh * LANES:(hh + 1) * LANES], q_ref[0, :, hh * LANES:(hh + 1) * LANES])

    def val(hh, j):
        return vt_ref[0, j, hh * V_ROWS:(hh + 1) * V_ROWS, :]

    def diag(hh, d, st):
        return jnp.where(d * tk + krow <= qcol, st, NEG_INF)

    out = _attend(i, qk, val, lambda hh, j, st: st, diag, tq, (sa0, sa1), (sb0, sb1))
    o_ref[0] = out.astype(BF16)


def _flash(q, k, vt, tq, tk):
    B, S, _ = q.shape
    nk = S // tk
    return pl.pallas_call(
        functools.partial(_flash_kernel, tq=tq, tk=tk), grid=(B, HEADS // 2, S // tq),
        in_specs=[pl.BlockSpec((1, tq, 2 * LANES), lambda b, h, i: (b, i, h)),
                  pl.BlockSpec((1, nk, tk, 2 * LANES), lambda b, h, i: (b, 0, 0, h)),
                  pl.BlockSpec((1, nk, 2 * V_ROWS, tk), lambda b, h, i: (b, 0, h, 0))],
        out_specs=pl.BlockSpec((1, tq, LANES), lambda b, h, i: (b, i, h)),
        out_shape=jax.ShapeDtypeStruct((B, S, BRANCH_WIDTH), BF16),
        scratch_shapes=_score_scratch(tq, tk),
        compiler_params=_cparams(("parallel", "parallel", "arbitrary")),
        name="flash",
    )(q, k, vt)


def _moba_kernel(q_ref, k_ref, vt_ref, o_ref, kmean_ref, bias_ref, qm_ref, sa0, sa1, sb0, sb1, *, nblk, tq):
    i = pl.program_id(2)
    blk = MOBA_BLOCK
    shift = blk.bit_length() - 1
    r = tq // blk

    @pl.when(i == 0)
    def _():
        for n in range(nblk):
            kmean_ref[n:n + 1, :] = jnp.mean(k_ref[0, n].astype(F32), axis=0, keepdims=True)

    lane = lax.broadcasted_iota(jnp.int32, (tq, LANES), 1)
    blk_id = lax.broadcasted_iota(jnp.int32, (nblk, tq), 0)
    own = i * r + (lax.broadcasted_iota(jnp.int32, (nblk, tq), 1) >> shift)
    q2 = q_ref[0]
    for hh in range(2):
        in_head = (lane >= hh * HEAD_DIM) & (lane < (hh + 1) * HEAD_DIM)
        q = jnp.where(in_head, q2, jnp.zeros_like(q2))
        qm_ref[hh] = q
        g = lax.dot_general(kmean_ref[...], q.astype(F32), (((1,), (1,)), ((), ())),
                            precision=lax.Precision.HIGHEST, preferred_element_type=F32)
        g = jnp.where(blk_id < own, g, NEG_INF)
        bias = jnp.full((nblk, tq), NEG_INF, F32)
        for _ in range(MOBA_TOPK):
            mx = jnp.max(g, axis=0, keepdims=True)
            first = jnp.min(jnp.where(g == mx, blk_id, nblk), axis=0, keepdims=True)
            pick = (blk_id == first) & (mx > NEG_INF)
            bias = jnp.where(pick, 0.0, bias)
            g = jnp.where(pick, NEG_INF, g)
        bias_ref[hh] = bias

    krow = lax.broadcasted_iota(jnp.int32, (blk, tq), 0)
    qcol = lax.broadcasted_iota(jnp.int32, (blk, tq), 1)

    def qk(hh, n):
        return _nt_dot(k_ref[0, n], qm_ref[hh])

    def val(hh, n):
        return vt_ref[0, n, hh * V_ROWS:(hh + 1) * V_ROWS, :]

    def past(hh, n, st):
        return st + bias_ref[hh, pl.ds(n, 1), :]

    def diag(hh, d, st):
        own_causal = ((qcol >> shift) == d) & (krow <= (qcol & (blk - 1)))
        return jnp.where(own_causal, st, past(hh, i * r + d, st))

    out = _attend(i, qk, val, past, diag, tq, (sa0, sa1), (sb0, sb1))
    o_ref[0] = out.astype(BF16)


def _moba(q, k, vt, tq):
    B, S, _ = q.shape
    nblk = S // MOBA_BLOCK
    assert tq == 2 * MOBA_BLOCK and MOBA_BLOCK & (MOBA_BLOCK - 1) == 0
    qc = COL_MQ // LANES
    kc = COL_MK // LANES
    return pl.pallas_call(
        functools.partial(_moba_kernel, nblk=nblk, tq=tq), grid=(B, HEADS // 2, S // tq),
        in_specs=[pl.BlockSpec((1, tq, LANES), lambda b, h, i: (b, i, qc + h)),
                  pl.BlockSpec((1, nblk, MOBA_BLOCK, LANES), lambda b, h, i: (b, 0, 0, kc + h)),
                  pl.BlockSpec((1, nblk, 2 * V_ROWS, MOBA_BLOCK), lambda b, h, i: (b, 0, h, 0))],
        out_specs=pl.BlockSpec((1, tq, LANES), lambda b, h, i: (b, i, h)),
        out_shape=jax.ShapeDtypeStruct((B, S, BRANCH_WIDTH), BF16),
        scratch_shapes=[pltpu.VMEM((nblk, LANES), F32), pltpu.VMEM((2, nblk, tq), F32),
                        pltpu.VMEM((2, tq, LANES), BF16)] + _score_scratch(tq, MOBA_BLOCK),
        compiler_params=_cparams(("parallel", "parallel", "arbitrary")),
        name="moba",
    )(q, k, vt)


def _merge_kernel(x_ref, ya_ref, yb_ref, yc_ref, ga_ref, gb_ref, gc_ref, wb_ref, wo_ref, o_ref):
    merged = None
    for n, (y_ref, g_ref) in enumerate(((ya_ref, ga_ref), (yb_ref, gb_ref), (yc_ref, gc_ref))):
        proj = jnp.dot(y_ref[...], wb_ref[n], preferred_element_type=F32)
        term = jax.nn.sigmoid(g_ref[...].astype(F32)) * proj
        merged = term if merged is None else merged + term
    o_ref[...] = x_ref[...] + jnp.dot(merged.astype(BF16), wo_ref[...], preferred_element_type=F32)


def _merge(x, y_mla, y_fox, y_moba, big, w_branch, w_out, tm=256):
    T = x.shape[0]
    g0 = COL_GATES // D_MODEL
    row = lambda n: pl.BlockSpec((tm, n), lambda i: (i, 0))
    gate = lambda n: pl.BlockSpec((tm, D_MODEL), lambda i: (i, g0 + n))
    return pl.pallas_call(
        _merge_kernel, grid=(T // tm,),
        in_specs=[row(D_MODEL), row(BRANCH_WIDTH), row(BRANCH_WIDTH), row(BRANCH_WIDTH),
                  gate(0), gate(1), gate(2),
                  pl.BlockSpec((3, BRANCH_WIDTH, D_MODEL), lambda i: (0, 0, 0)),
                  pl.BlockSpec((D_MODEL, D_MODEL), lambda i: (0, 0))],
        out_specs=row(D_MODEL),
        out_shape=jax.ShapeDtypeStruct((T, D_MODEL), F32),
        compiler_params=_cparams(("parallel",)),
        name="merge",
    )(x, y_mla, y_fox, y_moba, big, big, big, w_branch, w_out)


def _dense_ffn_kernel(x_ref, g_ref, wg_ref, wu_ref, wd_ref, o_ref, h_ref, acc_ref):
    f = pl.program_id(1)

    @pl.when(f == 0)
    def _():
        h_ref[...] = _rms(x_ref[...], g_ref[...]).astype(BF16)
        acc_ref[...] = x_ref[...]

    h = h_ref[...]
    gate = jnp.dot(h, wg_ref[...], preferred_element_type=F32)
    up = jnp.dot(h, wu_ref[...], preferred_element_type=F32)
    act = (jax.nn.silu(gate) * up).astype(BF16)
    acc_ref[...] += jnp.dot(act, wd_ref[...], preferred_element_type=F32)

    @pl.when(f == pl.num_programs(1) - 1)
    def _():
        o_ref[...] = acc_ref[...]


def _dense_ffn(x, g, w_gu, w_down, tm=1024, tf=256):
    T = x.shape[0]
    nf = DENSE_FF // tf
    return pl.pallas_call(
        _dense_ffn_kernel, grid=(T // tm, nf),
        in_specs=[pl.BlockSpec((tm, D_MODEL), lambda i, f: (i, 0)),
                  pl.BlockSpec((1, D_MODEL), lambda i, f: (0, 0)),
                  pl.BlockSpec((D_MODEL, tf), lambda i, f: (0, f)),
                  pl.BlockSpec((D_MODEL, tf), lambda i, f: (0, nf + f)),
                  pl.BlockSpec((tf, D_MODEL), lambda i, f: (f, 0))],
        out_specs=pl.BlockSpec((tm, D_MODEL), lambda i, f: (i, 0)),
        out_shape=jax.ShapeDtypeStruct((T, D_MODEL), F32),
        scratch_shapes=[pltpu.VMEM((tm, D_MODEL), BF16), pltpu.VMEM((tm, D_MODEL), F32)],
        compiler_params=_cparams(("parallel", "arbitrary")),
        name="dense_ffn",
    )(x, g, w_gu, w_gu, w_down)


def _router_kernel(x_ref, g_ref, wr_ref, h_ref, r_ref):
    h = _rms(x_ref[...], g_ref[...])
    h_ref[...] = h
    logits = jnp.dot(h, wr_ref[...], precision=lax.Precision.HIGHEST, preferred_element_type=F32)
    lane = lax.broadcasted_iota(jnp.int32, logits.shape, 1)
    logits = jnp.where(lane < N_EXPERTS, logits, NEG_INF)
    m1 = jnp.max(logits, axis=-1, keepdims=True)
    i1 = jnp.min(jnp.where(logits == m1, lane, LANES), axis=-1, keepdims=True)
    rest = jnp.where(lane == i1, NEG_INF, logits)
    m2 = jnp.max(rest, axis=-1, keepdims=True)
    i2 = jnp.min(jnp.where(rest == m2, lane, LANES), axis=-1, keepdims=True)
    e2 = jnp.exp(m2 - m1)
    w1 = 1.0 / (1.0 + e2)
    w2 = e2 / (1.0 + e2)
    r_ref[...] = jnp.where(lane == 0, i1.astype(F32), jnp.where(lane == 1, i2.astype(F32),
                           jnp.where(lane == 2, w1, jnp.where(lane == 3, w2, 0.0))))


def _router(x, g, w_router_pad, tm=512):
    T = x.shape[0]
    return pl.pallas_call(
        _router_kernel, grid=(T // tm,),
        in_specs=[pl.BlockSpec((tm, D_MODEL), lambda i: (i, 0)),
                  pl.BlockSpec((1, D_MODEL), lambda i: (0, 0)),
                  pl.BlockSpec((D_MODEL, LANES), lambda i: (0, 0))],
        out_specs=[pl.BlockSpec((tm, D_MODEL), lambda i: (i, 0)), pl.BlockSpec((tm, LANES), lambda i: (i, 0))],
        out_shape=[jax.ShapeDtypeStruct((T, D_MODEL), F32), jax.ShapeDtypeStruct((T, LANES), F32)],
        compiler_params=_cparams(("parallel",)),
        name="router",
    )(x, g, w_router_pad)


GATHER_UNROLL = 8


def _gather_kernel(idx_ref, prev_ref, src_ref, o_ref, sems, *, bm):
    b = pl.program_id(0)
    slot = b % 2

    def copy(ids_ref, blk, r, s):
        return pltpu.make_async_copy(src_ref.at[pl.ds(ids_ref[0, 0, r], 1), :],
                                     o_ref.at[pl.ds(blk * bm + r, 1), :], sems.at[s])

    def start(g, c):
        for k in range(GATHER_UNROLL):
            copy(idx_ref, b, g * GATHER_UNROLL + k, slot).start(priority=k % 2)
        return c

    def wait_block(ids_ref, blk, s):
        def wait(g, c):
            for k in range(GATHER_UNROLL):
                copy(ids_ref, blk, g * GATHER_UNROLL + k, s).wait()
            return c
        lax.fori_loop(0, bm // GATHER_UNROLL, wait, 0)

    lax.fori_loop(0, bm // GATHER_UNROLL, start, 0)

    @pl.when(b > 0)
    def _():
        wait_block(prev_ref, b - 1, 1 - slot)

    @pl.when(b == pl.num_programs(0) - 1)
    def _():
        wait_block(idx_ref, b, slot)


def _gather_rows(src, idx, bm=512):
    M = idx.shape[0]
    C = src.shape[1]
    assert src.dtype.itemsize == 4 and M % bm == 0 and bm % GATHER_UNROLL == 0
    idx3 = idx.reshape(M // bm, 1, bm)
    return pl.pallas_call(
        functools.partial(_gather_kernel, bm=bm), grid=(M // bm,),
        in_specs=[pl.BlockSpec((1, 1, bm), lambda i: (i, 0, 0), memory_space=pltpu.SMEM),
                  pl.BlockSpec((1, 1, bm), lambda i: (jnp.maximum(i - 1, 0), 0, 0), memory_space=pltpu.SMEM),
                  pl.BlockSpec(memory_space=pl.ANY)],
        out_specs=pl.BlockSpec(memory_space=pl.ANY),
        out_shape=jax.ShapeDtypeStruct((M, C), src.dtype),
        scratch_shapes=[pltpu.SemaphoreType.DMA((2,))],
        compiler_params=_cparams(("arbitrary",)),
        name="gather_rows",
    )(idx3, idx3, src)


def _expert_kernel(be_ref, nused_ref, x_ref, wg_ref, wu_ref, wd_ref, o_ref, acc_ref):
    b = pl.program_id(0)
    f = pl.program_id(1)
    used = b < nused_ref[0]

    @pl.when(used)
    def _():
        x = x_ref[...].astype(BF16)
        gate = jnp.dot(x, wg_ref[0], preferred_element_type=F32)
        up = jnp.dot(x, wu_ref[0], preferred_element_type=F32)
        act = (jax.nn.silu(gate) * up).astype(BF16)
        y = jnp.dot(act, wd_ref[0], preferred_element_type=F32)

        @pl.when(f == 0)
        def _():
            acc_ref[...] = y

        @pl.when(f > 0)
        def _():
            acc_ref[...] += y

    @pl.when(f == pl.num_programs(1) - 1)
    def _():
        o_ref[...] = jnp.where(used, acc_ref[...], 0.0)


def _experts(x_rows, block_e, n_used, w_gu, w_down, bm, tf=896):
    n_rows = x_rows.shape[0]
    nf = EXPERT_FF // tf
    nb = n_rows // bm

    def fsel(b, f, nused):
        return jnp.where(b < nused[0], f, nf - 1)

    grid_spec = pltpu.PrefetchScalarGridSpec(
        num_scalar_prefetch=2, grid=(nb, nf),
        in_specs=[pl.BlockSpec((bm, D_MODEL), lambda b, f, be, nu: (b, 0)),
                  pl.BlockSpec((1, D_MODEL, tf), lambda b, f, be, nu: (be[b], 0, fsel(b, f, nu))),
                  pl.BlockSpec((1, D_MODEL, tf), lambda b, f, be, nu: (be[b], 0, nf + fsel(b, f, nu))),
                  pl.BlockSpec((1, tf, D_MODEL), lambda b, f, be, nu: (be[b], fsel(b, f, nu), 0))],
        out_specs=pl.BlockSpec((bm, D_MODEL), lambda b, f, be, nu: (b, 0)),
        scratch_shapes=[pltpu.VMEM((bm, D_MODEL), F32)])
    return pl.pallas_call(
        _expert_kernel, grid_spec=grid_spec,
        out_shape=jax.ShapeDtypeStruct((n_rows, D_MODEL), F32),
        compiler_params=_cparams(("arbitrary", "arbitrary")),
        name="experts",
    )(block_e, n_used, x_rows, w_gu, w_gu, w_down)


def _combine_kernel(x_ref, y0_ref, y1_ref, r_ref, g_ref, o_ref):
    r = r_ref[...]
    x = x_ref[...] + r[:, 2:3] * y0_ref[...] + r[:, 3:4] * y1_ref[...]
    o_ref[...] = _rms(x, g_ref[...])


def _combine(x, y_slots, route, g_final, tm=512):
    T = x.shape[0]
    return pl.pallas_call(
        _combine_kernel, grid=(T // tm,),
        in_specs=[pl.BlockSpec((tm, D_MODEL), lambda i: (i, 0)),
                  pl.BlockSpec((tm, D_MODEL), lambda i: (i, 0)),
                  pl.BlockSpec((tm, D_MODEL), lambda i: (T // tm + i, 0)),
                  pl.BlockSpec((tm, LANES), lambda i: (i, 0)),
                  pl.BlockSpec((1, D_MODEL), lambda i: (0, 0))],
        out_specs=pl.BlockSpec((tm, D_MODEL), lambda i: (i, 0)),
        out_shape=jax.ShapeDtypeStruct((T, D_MODEL), F32),
        compiler_params=_cparams(("parallel",)),
        name="combine",
    )(x, y_slots, y_slots, route, g_final)


def _pad_cols(w, n):
    return jnp.pad(w, ((0, 0), (0, n - w.shape[1])))


def _split_in_weights(w_in):
    sizes = [MLA_Q_LORA, MLA_KV_LORA, MLA_ROPE_DIM, BRANCH_WIDTH, BRANCH_WIDTH, BRANCH_WIDTH, HEADS,
             BRANCH_WIDTH, BRANCH_WIDTH, BRANCH_WIDTH, 3 * D_MODEL]
    pts = np.cumsum(sizes)[:-1]
    c_q, c_kv, k_pe, fq, fk, fv, fl, mq, mk, mv, gates = jnp.split(w_in, pts, axis=1)
    zeros = lambda n: jnp.zeros((D_MODEL, n), w_in.dtype)
    pe_tile = jnp.concatenate([zeros(HEAD_DIM), k_pe, zeros(LANES - HEAD_DIM - MLA_ROPE_DIM)], axis=1)
    w_small = jnp.concatenate([c_q, c_kv, pe_tile, _pad_cols(fl, LANES)], axis=1)
    w_big = jnp.concatenate([fq, mq, fk, mk, fv, mv, gates], axis=1)
    return w_small.astype(BF16), w_big.astype(BF16)


def _mla_up_weights(w_uq, w_ukv):
    dq = HEAD_DIM + MLA_ROPE_DIM
    wq = w_uq.reshape(MLA_Q_LORA, HEADS, dq)
    wq = jnp.pad(wq, ((0, 0), (0, 0), (0, LANES - dq))).reshape(MLA_Q_LORA, HEADS * LANES)
    wkv = w_ukv.reshape(MLA_KV_LORA, HEADS, 2 * HEAD_DIM)
    wk = jnp.pad(wkv[:, :, :HEAD_DIM], ((0, 0), (0, 0), (0, LANES - HEAD_DIM))).reshape(MLA_KV_LORA, HEADS * LANES)
    wv = wkv[:, :, HEAD_DIM:].reshape(MLA_KV_LORA, BRANCH_WIDTH)
    return wq.astype(BF16), wk.astype(BF16), wv.astype(BF16)


def _chunk_rows(a, B, S, t):
    return a.reshape(B, S // t, t, a.shape[-1])


def _value_tiles(v, B, S, t):
    v = v.reshape(B, S // t, t, HEADS, HEAD_DIM)
    pad = jnp.zeros(v.shape[:-1] + (V_ROWS - HEAD_DIM,), v.dtype).at[..., 0].set(1)
    return jnp.concatenate([v, pad], axis=-1).reshape(B, S // t, t, HEADS * V_ROWS).transpose(0, 1, 3, 2)


ATT_TQ = 512
ATT_TK = 256
MOBA_TQ = 512
MOE_BM = 512


def _token_mixers(x, B, S, g_mix, w_in, g_q_lat, g_kv_lat, w_uq, w_ukv, b_forget, w_branch, w_out,
                  tabs_mla, tabs_moba):
    T = B * S
    w_small, w_big = _split_in_weights(w_in)
    wq, wk, wv = _mla_up_weights(w_uq, w_ukv)
    g = g_mix[None, :]

    big = _inproj(x, g, w_big, tabs_moba)
    q_mla, k_mla, v_mla, f_logit = _mla_prep(
        x, g, w_small, g_q_lat[None, :], g_kv_lat[None, :], wq, wk, wv, tabs_mla,
        (HEAD_DIM + MLA_ROPE_DIM) ** -0.5 * LOG2E)

    y_mla = _flash(q_mla.reshape(B, S, -1), _chunk_rows(k_mla, B, S, ATT_TK),
                   _value_tiles(v_mla, B, S, ATT_TK), ATT_TQ, ATT_TK)

    c = _cumlogf(f_logit.reshape(B, S, HEADS).transpose(0, 2, 1), b_forget[:, None])
    c_rows = c.transpose(0, 2, 1).reshape(T, HEADS)
    q_fox, k_fox = _fox_prep(big, c_rows)
    fv = big[:, COL_FV:COL_FV + BRANCH_WIDTH]
    y_fox = _flash(q_fox.reshape(B, S, -1), _chunk_rows(k_fox, B, S, ATT_TK),
                   _value_tiles(fv, B, S, ATT_TK), ATT_TQ, ATT_TK)

    mv = big[:, COL_MV:COL_MV + BRANCH_WIDTH]
    y_moba = _moba(big.reshape(B, S, -1), _chunk_rows(big, B, S, MOBA_BLOCK),
                   _value_tiles(mv, B, S, MOBA_BLOCK), MOBA_TQ)

    return _merge(x, y_mla.reshape(T, -1), y_fox.reshape(T, -1), y_moba.reshape(T, -1), big,
                  w_branch.astype(BF16), w_out.astype(BF16))


def _moe(x, g_ffn, w_router, w_gu, w_down, g_final):
    T = x.shape[0]
    A = 2 * T
    bm = MOE_BM
    h, route = _router(x, g_ffn[None, :], _pad_cols(w_router, LANES))
    top_e = route[:, :2].astype(jnp.int32).reshape(A)
    onehot = (top_e[:, None] == jnp.arange(N_EXPERTS)[None, :]).astype(jnp.int32)
    csum = jnp.cumsum(onehot, axis=0)
    counts = csum[-1]
    rank = jnp.sum((csum - onehot) * onehot, axis=1)
    padded = (counts + bm - 1) // bm * bm
    pad_end = jnp.cumsum(padded)
    dest = (pad_end - padded)[top_e] + rank
    n_rows = (A // bm + N_EXPERTS) * bm
    row_tok = jnp.zeros((n_rows,), jnp.int32).at[dest].set(jnp.arange(A, dtype=jnp.int32) // 2)
    block_e = jnp.minimum(jnp.searchsorted(pad_end, jnp.arange(n_rows // bm) * bm, side="right"),
                          N_EXPERTS - 1).astype(jnp.int32)
    n_used = (pad_end[-1:] // bm).astype(jnp.int32)

    x_rows = _gather_rows(h, row_tok)
    y_rows = _experts(x_rows, block_e, n_used, w_gu.astype(BF16), w_down.astype(BF16), bm)
    y_slots = _gather_rows(y_rows, dest.astype(jnp.int32).reshape(T, 2).T.reshape(A))
    return _combine(x, y_slots, route, g_final[None, :])


def kernel(x, positions, g_mix, w_in, g_q_lat, g_kv_lat, w_uq, w_ukv, b_forget, w_branch, w_out, g_ffn,
           w_dense_gu, w_dense_down, w_router, w_exp_gu, w_exp_down, g_final):
    B, S, D = x.shape
    T = B * S
    depth = g_mix.shape[0]
    assert depth == 2 and D == D_MODEL and S % MOBA_BLOCK == 0
    pos = positions.reshape(T, 1).astype(jnp.int32)
    tabs_mla = _rope_tables(pos, *_rope_patterns(LANES, HEAD_DIM, MLA_ROPE_DIM // 2))
    tabs_moba = _rope_tables(pos, *_rope_patterns(HEAD_DIM, 0, HEAD_DIM // 2))
    x = x.reshape(T, D)
    for l in range(depth):
        x = _token_mixers(x, B, S, g_mix[l], w_in[l], g_q_lat[l], g_kv_lat[l], w_uq[l], w_ukv[l],
                          b_forget[l], w_branch[l], w_out[l], tabs_mla, tabs_moba)
        if l % 2 == 0:
            x = _dense_ffn(x, g_ffn[l][None, :], w_dense_gu[l // 2].astype(BF16),
                           w_dense_down[l // 2].astype(BF16))
        else:
            x = _moe(x, g_ffn[l], w_router[l // 2], w_exp_gu[l // 2], w_exp_down[l // 2], g_final)
    return x.reshape(B, S, D)
```

```python
import functools
import math

import jax
import jax.numpy as jnp
import numpy as np
from jax import lax
from jax.experimental import pallas as pl
from jax.experimental.pallas import tpu as pltpu
from jax.experimental.pallas import tpu_sc as plsc

F32 = jnp.float32
BF16 = jnp.bfloat16
NEG_INF = float("-inf")
M_INIT = -1e30
LOG2E = math.log2(math.e)

D_MODEL = 1024
RMS_EPS = 1e-6
ROPE_THETA = 10000.0
HEADS = 8
HEAD_DIM = 64
V_ROWS = 80
Q_SCALE = HEAD_DIM ** -0.5 * math.log2(math.e)
MLA_Q_LORA = 256
MLA_KV_LORA = 128
MLA_ROPE_DIM = 32
BRANCH_WIDTH = HEADS * HEAD_DIM
MOBA_BLOCK = 256
MOBA_TOPK = 3
DENSE_FF = 2816
N_EXPERTS = 8
EXPERT_FF = 3584

LANES = 128
VMEM_LIMIT = 48 * 1024 * 1024

COL_FQ, COL_MQ, COL_FK, COL_MK, COL_FV, COL_MV, COL_GATES = (n * BRANCH_WIDTH for n in range(7))
BIG_COLS = COL_GATES + 3 * D_MODEL
BIG_TN = 2 * BRANCH_WIDTH
SMALL_COLS = MLA_Q_LORA + MLA_KV_LORA + 2 * LANES


def _cparams(sem):
    return pltpu.CompilerParams(dimension_semantics=sem, vmem_limit_bytes=VMEM_LIMIT)


def _rms(x, g):
    return x * lax.rsqrt(jnp.mean(x * x, axis=-1, keepdims=True) + RMS_EPS) * g


def _rope_table_kernel(pos_ref, f_ref, mc_ref, m1_ref, m2_ref, c_ref, s1_ref, s2_ref):
    ang = pos_ref[...].astype(F32) * f_ref[...]
    cos = jnp.cos(ang)
    sin = jnp.sin(ang)
    mc = mc_ref[...]
    c_ref[...] = cos * mc + (1.0 - mc)
    s1_ref[...] = sin * m1_ref[...]
    s2_ref[...] = sin * m2_ref[...]


def _rope_tables(pos, freq, mc, m1, m2, tm=1024):
    T = pos.shape[0]
    row = pl.BlockSpec((tm, 1), lambda i: (i, 0))
    pat = pl.BlockSpec((1, LANES), lambda i: (0, 0))
    out = pl.BlockSpec((tm, LANES), lambda i: (i, 0))
    shp = jax.ShapeDtypeStruct((T, LANES), F32)
    return pl.pallas_call(
        _rope_table_kernel, grid=(T // tm,),
        in_specs=[row, pat, pat, pat, pat], out_specs=[out, out, out],
        out_shape=[shp, shp, shp], compiler_params=_cparams(("parallel",)),
        name="rope_tables",
    )(pos, freq, mc, m1, m2)


def _rope_patterns(group, x1_lo, half):
    d = 2 * half
    inv_freq = jnp.exp(-math.log(ROPE_THETA) * jnp.arange(half, dtype=F32) * 2.0 / d)
    lane = np.arange(LANES) % group
    in_x1 = (lane >= x1_lo) & (lane < x1_lo + half)
    in_x2 = (lane >= x1_lo + half) & (lane < x1_lo + d)
    k = np.where(in_x1, lane - x1_lo, np.where(in_x2, lane - x1_lo - half, 0))
    freq = jnp.where(jnp.asarray(in_x1 | in_x2), inv_freq[k], 0.0)[None, :].astype(F32)
    mc = jnp.asarray((in_x1 | in_x2).astype(np.float32))[None, :]
    m1 = jnp.asarray(-(in_x1.astype(np.float32)))[None, :]
    m2 = jnp.asarray(in_x2.astype(np.float32))[None, :]
    return freq, mc, m1, m2


def _apply_rope(x, c, s1, s2, half):
    n = x.shape[-1]
    reps = n // LANES
    c, s1, s2 = (jnp.tile(t, (1, reps)) if reps > 1 else t for t in (c, s1, s2))
    return x * c + pltpu.roll(x, n - half, 1) * s1 + pltpu.roll(x, half, 1) * s2


def _inproj_kernel(x_ref, g_ref, w_ref, c_ref, s1_ref, s2_ref, o_ref, h_ref):
    j = pl.program_id(1)

    @pl.when(j == 0)
    def _():
        h_ref[...] = _rms(x_ref[...], g_ref[...]).astype(BF16)

    acc = jnp.dot(h_ref[...], w_ref[...], preferred_element_type=F32)

    @pl.when(j < 2)
    def _():
        a = acc * jnp.where(j == 0, Q_SCALE, 1.0)
        o_ref[:, :BRANCH_WIDTH] = a[:, :BRANCH_WIDTH].astype(BF16)
        o_ref[:, BRANCH_WIDTH:] = _apply_rope(a[:, BRANCH_WIDTH:], c_ref[...], s1_ref[...], s2_ref[...],
                                              HEAD_DIM // 2).astype(BF16)

    @pl.when(j >= 2)
    def _():
        o_ref[...] = acc.astype(BF16)


def _inproj(x, g, w_big, tabs, tm=1024):
    T = x.shape[0]
    tab = pl.BlockSpec((tm, LANES), lambda i, j: (i, 0))
    return pl.pallas_call(
        _inproj_kernel, grid=(T // tm, BIG_COLS // BIG_TN),
        in_specs=[pl.BlockSpec((tm, D_MODEL), lambda i, j: (i, 0)),
                  pl.BlockSpec((1, D_MODEL), lambda i, j: (0, 0)),
                  pl.BlockSpec((D_MODEL, BIG_TN), lambda i, j: (0, j)),
                  tab, tab, tab],
        out_specs=pl.BlockSpec((tm, BIG_TN), lambda i, j: (i, j)),
        out_shape=jax.ShapeDtypeStruct((T, BIG_COLS), BF16),
        scratch_shapes=[pltpu.VMEM((tm, D_MODEL), BF16)],
        compiler_params=_cparams(("parallel", "arbitrary")),
        name="inproj",
    )(x, g, w_big, *tabs)


def _mla_prep_kernel(x_ref, g_ref, ws_ref, gq_ref, gkv_ref, wq_ref, wk_ref, wv_ref,
                     c_ref, s1_ref, s2_ref, q_ref, k_ref, v_ref, fl_ref, *, scale):
    h = _rms(x_ref[...], g_ref[...]).astype(BF16)
    small = jnp.dot(h, ws_ref[...], preferred_element_type=F32)
    c_q = small[:, :MLA_Q_LORA]
    c_kv = small[:, MLA_Q_LORA:MLA_Q_LORA + MLA_KV_LORA]
    k_pe = small[:, MLA_Q_LORA + MLA_KV_LORA:MLA_Q_LORA + MLA_KV_LORA + LANES]
    fl_ref[...] = small[:, SMALL_COLS - LANES:SMALL_COLS - LANES + HEADS]
    c, s1, s2 = c_ref[...], s1_ref[...], s2_ref[...]
    half = MLA_ROPE_DIM // 2
    qn = _rms(c_q, gq_ref[...]).astype(BF16)
    q = jnp.dot(qn, wq_ref[...], preferred_element_type=F32) * scale
    q_ref[...] = _apply_rope(q, c, s1, s2, half).astype(BF16)
    kvn = _rms(c_kv, gkv_ref[...]).astype(BF16)
    k_nope = jnp.dot(kvn, wk_ref[...], preferred_element_type=F32)
    k_rot = _apply_rope(k_pe, c, s1, s2, half)
    k_ref[...] = (k_nope + jnp.tile(k_rot, (1, HEADS))).astype(BF16)
    v_ref[...] = jnp.dot(kvn, wv_ref[...], preferred_element_type=F32).astype(BF16)


def _mla_prep(x, g, w_small, g_q, g_kv, wq, wk, wv, tabs, scale, tm=512):
    T = x.shape[0]
    full = lambda shape: pl.BlockSpec(shape, lambda i: (0,) * len(shape))
    row = lambda n: pl.BlockSpec((tm, n), lambda i: (i, 0))
    qk = HEADS * LANES
    return pl.pallas_call(
        functools.partial(_mla_prep_kernel, scale=scale), grid=(T // tm,),
        in_specs=[row(D_MODEL), full((1, D_MODEL)), full((D_MODEL, SMALL_COLS)),
                  full((1, MLA_Q_LORA)), full((1, MLA_KV_LORA)),
                  full((MLA_Q_LORA, qk)), full((MLA_KV_LORA, qk)), full((MLA_KV_LORA, BRANCH_WIDTH)),
                  row(LANES), row(LANES), row(LANES)],
        out_specs=[row(qk), row(qk), row(BRANCH_WIDTH), row(HEADS)],
        out_shape=[jax.ShapeDtypeStruct((T, qk), BF16), jax.ShapeDtypeStruct((T, qk), BF16),
                   jax.ShapeDtypeStruct((T, BRANCH_WIDTH), BF16), jax.ShapeDtypeStruct((T, HEADS), F32)],
        compiler_params=_cparams(("parallel",)),
        name="mla_prep",
    )(x, g, w_small, g_q, g_kv, wq, wk, wv, *tabs)


def _cumlogf_kernel(fl_ref, b_ref, c_ref):
    z = fl_ref[0] + b_ref[...]
    x = jnp.minimum(z, 0.0) - jnp.log1p(jnp.exp(-jnp.abs(z)))
    n = x.shape[-1]
    lane = lax.broadcasted_iota(jnp.int32, x.shape, 1)
    d = 1
    while d < n:
        x = x + jnp.where(lane >= d, pltpu.roll(x, d, 1), 0.0)
        d *= 2
    c_ref[0] = x


def _cumlogf(fl_t, b_col):
    B, H, S = fl_t.shape
    return pl.pallas_call(
        _cumlogf_kernel, grid=(B,),
        in_specs=[pl.BlockSpec((1, H, S), lambda b: (b, 0, 0)), pl.BlockSpec((H, 1), lambda b: (0, 0))],
        out_specs=pl.BlockSpec((1, H, S), lambda b: (b, 0, 0)),
        out_shape=jax.ShapeDtypeStruct((B, H, S), F32),
        compiler_params=_cparams(("parallel",)),
        name="cumlogf",
    )(fl_t, b_col)


def _split3(c):
    hi = c.astype(BF16)
    r = c - hi.astype(F32)
    mid = r.astype(BF16)
    lo = (r - mid.astype(F32)).astype(BF16)
    return hi.astype(F32), mid.astype(F32), lo.astype(F32)


def _fox_prep_kernel(q_ref, k_ref, c_ref, qo_ref, ko_ref):
    tm = q_ref.shape[0]
    lane = lax.broadcasted_iota(jnp.int32, (tm, LANES), 1)
    c = c_ref[...] * LOG2E
    for hp in range(HEADS // 2):
        q2 = q_ref[:, hp * LANES:(hp + 1) * LANES].astype(F32)
        k2 = k_ref[:, hp * LANES:(hp + 1) * LANES].astype(F32)
        for hh in range(2):
            h = 2 * hp + hh
            hi, mid, lo = _split3(c[:, h:h + 1])
            aug_c = jnp.where(lane == HEAD_DIM, hi, jnp.where(lane == HEAD_DIM + 1, mid,
                              jnp.where(lane == HEAD_DIM + 2, lo, 0.0)))
            ones_a = jnp.where((lane >= HEAD_DIM) & (lane < HEAD_DIM + 3), 1.0, 0.0)
            qh = q2 if hh == 0 else pltpu.roll(q2, HEAD_DIM, 1)
            kh = k2 if hh == 0 else pltpu.roll(k2, HEAD_DIM, 1)
            q_aug = jnp.where(lane < HEAD_DIM, qh, ones_a + pltpu.roll(aug_c, 3, 1))
            k_aug = jnp.where(lane < HEAD_DIM, kh, pltpu.roll(ones_a, 3, 1) - aug_c)
            qo_ref[:, h * LANES:(h + 1) * LANES] = q_aug.astype(BF16)
            ko_ref[:, h * LANES:(h + 1) * LANES] = k_aug.astype(BF16)


def _fox_prep(big, c_rows, tm=512):
    T = big.shape[0]
    qk = HEADS * LANES
    return pl.pallas_call(
        _fox_prep_kernel, grid=(T // tm,),
        in_specs=[pl.BlockSpec((tm, BRANCH_WIDTH), lambda i: (i, COL_FQ // BRANCH_WIDTH)),
                  pl.BlockSpec((tm, BRANCH_WIDTH), lambda i: (i, COL_FK // BRANCH_WIDTH)),
                  pl.BlockSpec((tm, HEADS), lambda i: (i, 0))],
        out_specs=[pl.BlockSpec((tm, qk), lambda i: (i, 0)), pl.BlockSpec((tm, qk), lambda i: (i, 0))],
        out_shape=[jax.ShapeDtypeStruct((T, qk), BF16), jax.ShapeDtypeStruct((T, qk), BF16)],
        compiler_params=_cparams(("parallel",)),
        name="fox_prep",
    )(big, big, c_rows)


def _nt_dot(a, b):
    return lax.dot_general(a, b, (((1,), (1,)), ((), ())), preferred_element_type=F32)


def _softmax_step(sts, m, acc, vts):
    m_new = m
    for st in sts:
        m_new = jnp.maximum(m_new, jnp.max(st, axis=0, keepdims=True))
    acc = jnp.exp2(m - m_new) * acc
    for st, vt in zip(sts, vts):
        acc = acc + jnp.dot(vt, jnp.exp2(st - m_new).astype(BF16), preferred_element_type=F32)
    return m_new, acc


def _softmax_init(tq):
    return jnp.full((1, tq), M_INIT, F32), jnp.zeros((V_ROWS, tq), F32)


def _softmax_finish(acc):
    return (acc[:HEAD_DIM] / acc[HEAD_DIM:HEAD_DIM + 1]).T


def _causal_mask(st, tk, tq, k0, q0):
    kpos = k0 + lax.broadcasted_iota(jnp.int32, (tk, tq), 0)
    qpos = q0 + lax.broadcasted_iota(jnp.int32, (tk, tq), 1)
    return jnp.where(kpos <= qpos, st, NEG_INF)


def _attend(npairs, qk, val, past, diag, tq, sa, sb):
    heads = range(2)

    def put(dst, j):
        for hh in heads:
            dst[hh][...] = qk(hh, j)

    def advance(carry, src, j, fn):
        return tuple(_softmax_step([fn(hh, j, src[hh][...])], *carry[hh], [val(hh, j)]) for hh in heads)

    put(sa, 0)

    def body(jj, carry):
        j0 = 2 * jj
        put(sb, j0 + 1)
        carry = advance(carry, sa, j0, past)
        put(sa, j0 + 2)
        return advance(carry, sb, j0 + 1, past)

    carry = lax.fori_loop(0, npairs, body, (_softmax_init(tq), _softmax_init(tq)))
    j0 = 2 * npairs
    put(sb, j0 + 1)
    carry = advance(carry, sa, j0, lambda hh, j, st: diag(hh, 0, st))
    carry = advance(carry, sb, j0 + 1, lambda hh, j, st: diag(hh, 1, st))
    return jnp.concatenate([_softmax_finish(acc) for _, acc in carry], axis=1)


def _score_scratch(tq, tk):
    return [pltpu.VMEM((tk, tq), F32) for _ in range(4)]


def _flash_kernel(q_ref, k_ref, vt_ref, o_ref, sa0, sa1, sb0, sb1, *, tq, tk):
    i = pl.program_id(2)
    assert tq == 2 * tk
    krow = lax.broadcasted_iota(jnp.int32, (tk, tq), 0)
    qcol = lax.broadcasted_iota(jnp.int32, (tk, tq), 1)

    def qk(hh, j):
        return _nt_dot(k_ref[0, j, :, hh * LANES:(hh + 1) * LANES], q_ref[0, :, hh * LANES:(hh + 1) * LANES])

    def val(hh, j):
        return vt_ref[0, j, hh * V_ROWS:(hh + 1) * V_ROWS, :]

    def diag(hh, d, st):
        return jnp.where(d * tk + krow <= qcol, st, NEG_INF)

    out = _attend(i, qk, val, lambda hh, j, st: st, diag, tq, (sa0, sa1), (sb0, sb1))
    o_ref[0] = out.astype(BF16)


def _flash(q, k, vt, tq, tk):
    B, S, _ = q.shape
    nk = S // tk
    return pl.pallas_call(
        functools.partial(_flash_kernel, tq=tq, tk=tk), grid=(B, HEADS // 2, S // tq),
        in_specs=[pl.BlockSpec((1, tq, 2 * LANES), lambda b, h, i: (b, i, h)),
                  pl.BlockSpec((1, nk, tk, 2 * LANES), lambda b, h, i: (b, 0, 0, h)),
                  pl.BlockSpec((1, nk, 2 * V_ROWS, tk), lambda b, h, i: (b, 0, h, 0))],
        out_specs=pl.BlockSpec((1, tq, LANES), lambda b, h, i: (b, i, h)),
        out_shape=jax.ShapeDtypeStruct((B, S, BRANCH_WIDTH), BF16),
        scratch_shapes=_score_scratch(tq, tk),
        compiler_params=_cparams(("parallel", "parallel", "arbitrary")),
        name="flash",
    )(q, k, vt)


def _moba_kernel(q_ref, k_ref, vt_ref, o_ref, kmean_ref, bias_ref, qm_ref, sa0, sa1, sb0, sb1, *, nblk, tq):
    i = pl.program_id(2)
    blk = MOBA_BLOCK
    shift = blk.bit_length() - 1
    r = tq // blk

    @pl.when(i == 0)
    def _():
        for n in range(nblk):
            kmean_ref[n:n + 1, :] = jnp.mean(k_ref[0, n].astype(F32), axis=0, keepdims=True)

    lane = lax.broadcasted_iota(jnp.int32, (tq, LANES), 1)
    blk_id = lax.broadcasted_iota(jnp.int32, (nblk, tq), 0)
    own = i * r + (lax.broadcasted_iota(jnp.int32, (nblk, tq), 1) >> shift)
    q2 = q_ref[0]
    for hh in range(2):
        in_head = (lane >= hh * HEAD_DIM) & (lane < (hh + 1) * HEAD_DIM)
        q = jnp.where(in_head, q2, jnp.zeros_like(q2))
        qm_ref[hh] = q
        g = lax.dot_general(kmean_ref[...], q.astype(F32), (((1,), (1,)), ((), ())),
                            precision=lax.Precision.HIGHEST, preferred_element_type=F32)
        g = jnp.where(blk_id < own, g, NEG_INF)
        bias = jnp.full((nblk, tq), NEG_INF, F32)
        for _ in range(MOBA_TOPK):
            mx = jnp.max(g, axis=0, keepdims=True)
            first = jnp.min(jnp.where(g == mx, blk_id, nblk), axis=0, keepdims=True)
            pick = (blk_id == first) & (mx > NEG_INF)
            bias = jnp.where(pick, 0.0, bias)
            g = jnp.where(pick, NEG_INF, g)
        bias_ref[hh] = bias

    krow = lax.broadcasted_iota(jnp.int32, (blk, tq), 0)
    qcol = lax.broadcasted_iota(jnp.int32, (blk, tq), 1)

    def qk(hh, n):
        return _nt_dot(k_ref[0, n], qm_ref[hh])

    def val(hh, n):
        return vt_ref[0, n, hh * V_ROWS:(hh + 1) * V_ROWS, :]

    def past(hh, n, st):
        return st + bias_ref[hh, pl.ds(n, 1), :]

    def diag(hh, d, st):
        own_causal = ((qcol >> shift) == d) & (krow <= (qcol & (blk - 1)))
        return jnp.where(own_causal, st, past(hh, i * r + d, st))

    out = _attend(i, qk, val, past, diag, tq, (sa0, sa1), (sb0, sb1))
    o_ref[0] = out.astype(BF16)


def _moba(q, k, vt, tq):
    B, S, _ = q.shape
    nblk = S // MOBA_BLOCK
    assert tq == 2 * MOBA_BLOCK and MOBA_BLOCK & (MOBA_BLOCK - 1) == 0
    qc = COL_MQ // LANES
    kc = COL_MK // LANES
    return pl.pallas_call(
        functools.partial(_moba_kernel, nblk=nblk, tq=tq), grid=(B, HEADS // 2, S // tq),
        in_specs=[pl.BlockSpec((1, tq, LANES), lambda b, h, i: (b, i, qc + h)),
                  pl.BlockSpec((1, nblk, MOBA_BLOCK, LANES), lambda b, h, i: (b, 0, 0, kc + h)),
                  pl.BlockSpec((1, nblk, 2 * V_ROWS, MOBA_BLOCK), lambda b, h, i: (b, 0, h, 0))],
        out_specs=pl.BlockSpec((1, tq, LANES), lambda b, h, i: (b, i, h)),
        out_shape=jax.ShapeDtypeStruct((B, S, BRANCH_WIDTH), BF16),
        scratch_shapes=[pltpu.VMEM((nblk, LANES), F32), pltpu.VMEM((2, nblk, tq), F32),
                        pltpu.VMEM((2, tq, LANES), BF16)] + _score_scratch(tq, MOBA_BLOCK),
        compiler_params=_cparams(("parallel", "parallel", "arbitrary")),
        name="moba",
    )(q, k, vt)


def _merge_kernel(x_ref, ya_ref, yb_ref, yc_ref, ga_ref, gb_ref, gc_ref, wb_ref, wo_ref, o_ref):
    merged = None
    for n, (y_ref, g_ref) in enumerate(((ya_ref, ga_ref), (yb_ref, gb_ref), (yc_ref, gc_ref))):
        proj = jnp.dot(y_ref[...], wb_ref[n], preferred_element_type=F32)
        term = jax.nn.sigmoid(g_ref[...].astype(F32)) * proj
        merged = term if merged is None else merged + term
    o_ref[...] = x_ref[...] + jnp.dot(merged.astype(BF16), wo_ref[...], preferred_element_type=F32)


def _merge(x, y_mla, y_fox, y_moba, big, w_branch, w_out, tm=256):
    T = x.shape[0]
    g0 = COL_GATES // D_MODEL
    row = lambda n: pl.BlockSpec((tm, n), lambda i: (i, 0))
    gate = lambda n: pl.BlockSpec((tm, D_MODEL), lambda i: (i, g0 + n))
    return pl.pallas_call(
        _merge_kernel, grid=(T // tm,),
        in_specs=[row(D_MODEL), row(BRANCH_WIDTH), row(BRANCH_WIDTH), row(BRANCH_WIDTH),
                  gate(0), gate(1), gate(2),
                  pl.BlockSpec((3, BRANCH_WIDTH, D_MODEL), lambda i: (0, 0, 0)),
                  pl.BlockSpec((D_MODEL, D_MODEL), lambda i: (0, 0))],
        out_specs=row(D_MODEL),
        out_shape=jax.ShapeDtypeStruct((T, D_MODEL), F32),
        compiler_params=_cparams(("parallel",)),
        name="merge",
    )(x, y_mla, y_fox, y_moba, big, big, big, w_branch, w_out)


def _dense_ffn_kernel(x_ref, g_ref, wg_ref, wu_ref, wd_ref, o_ref, h_ref, acc_ref):
    f = pl.program_id(1)

    @pl.when(f == 0)
    def _():
        h_ref[...] = _rms(x_ref[...], g_ref[...]).astype(BF16)
        acc_ref[...] = x_ref[...]

    h = h_ref[...]
    gate = jnp.dot(h, wg_ref[...], preferred_element_type=F32)
    up = jnp.dot(h, wu_ref[...], preferred_element_type=F32)
    act = (jax.nn.silu(gate) * up).astype(BF16)
    acc_ref[...] += jnp.dot(act, wd_ref[...], preferred_element_type=F32)

    @pl.when(f == pl.num_programs(1) - 1)
    def _():
        o_ref[...] = acc_ref[...]


def _dense_ffn(x, g, w_gu, w_down, tm=1024, tf=256):
    T = x.shape[0]
    nf = DENSE_FF // tf
    return pl.pallas_call(
        _dense_ffn_kernel, grid=(T // tm, nf),
        in_specs=[pl.BlockSpec((tm, D_MODEL), lambda i, f: (i, 0)),
                  pl.BlockSpec((1, D_MODEL), lambda i, f: (0, 0)),
                  pl.BlockSpec((D_MODEL, tf), lambda i, f: (0, f)),
                  pl.BlockSpec((D_MODEL, tf), lambda i, f: (0, nf + f)),
                  pl.BlockSpec((tf, D_MODEL), lambda i, f: (f, 0))],
        out_specs=pl.BlockSpec((tm, D_MODEL), lambda i, f: (i, 0)),
        out_shape=jax.ShapeDtypeStruct((T, D_MODEL), F32),
        scratch_shapes=[pltpu.VMEM((tm, D_MODEL), BF16), pltpu.VMEM((tm, D_MODEL), F32)],
        compiler_params=_cparams(("parallel", "arbitrary")),
        name="dense_ffn",
    )(x, g, w_gu, w_gu, w_down)


def _router_kernel(x_ref, g_ref, wr_ref, h_ref, r_ref):
    h = _rms(x_ref[...], g_ref[...])
    h_ref[...] = h
    logits = jnp.dot(h, wr_ref[...], precision=lax.Precision.HIGHEST, preferred_element_type=F32)
    lane = lax.broadcasted_iota(jnp.int32, logits.shape, 1)
    logits = jnp.where(lane < N_EXPERTS, logits, NEG_INF)
    m1 = jnp.max(logits, axis=-1, keepdims=True)
    i1 = jnp.min(jnp.where(logits == m1, lane, LANES), axis=-1, keepdims=True)
    rest = jnp.where(lane == i1, NEG_INF, logits)
    m2 = jnp.max(rest, axis=-1, keepdims=True)
    i2 = jnp.min(jnp.where(rest == m2, lane, LANES), axis=-1, keepdims=True)
    e2 = jnp.exp(m2 - m1)
    w1 = 1.0 / (1.0 + e2)
    w2 = e2 / (1.0 + e2)
    r_ref[...] = jnp.where(lane == 0, i1.astype(F32), jnp.where(lane == 1, i2.astype(F32),
                           jnp.where(lane == 2, w1, jnp.where(lane == 3, w2, 0.0))))


def _router(x, g, w_router_pad, tm=512):
    T = x.shape[0]
    return pl.pallas_call(
        _router_kernel, grid=(T // tm,),
        in_specs=[pl.BlockSpec((tm, D_MODEL), lambda i: (i, 0)),
                  pl.BlockSpec((1, D_MODEL), lambda i: (0, 0)),
                  pl.BlockSpec((D_MODEL, LANES), lambda i: (0, 0))],
        out_specs=[pl.BlockSpec((tm, D_MODEL), lambda i: (i, 0)), pl.BlockSpec((tm, LANES), lambda i: (i, 0))],
        out_shape=[jax.ShapeDtypeStruct((T, D_MODEL), F32), jax.ShapeDtypeStruct((T, LANES), F32)],
        compiler_params=_cparams(("parallel",)),
        name="router",
    )(x, g, w_router_pad)


GATHER_WINDOW = 128
GATHER_ROWS = 32


def _gather_rows(src, idx):
    M = idx.shape[0]
    C = src.shape[1]
    mesh = plsc.VectorSubcoreMesh(core_axis_name="core", subcore_axis_name="subcore")
    per = M // (mesh.num_cores * mesh.num_subcores)
    assert per * mesh.num_cores * mesh.num_subcores == M and per % GATHER_WINDOW == 0

    @pl.kernel(out_type=jax.ShapeDtypeStruct((M, C), src.dtype), mesh=mesh, name="gather_rows",
               scratch_types=[pltpu.VMEM((GATHER_WINDOW,), jnp.int32), pltpu.VMEM((GATHER_ROWS, C), src.dtype)])
    def gather(x_hbm, i_hbm, o_hbm, idx_v, buf):
        w = lax.axis_index("core") * mesh.num_subcores + lax.axis_index("subcore")

        @pl.loop(0, per // GATHER_WINDOW)
        def _(t):
            base = w * per + t * GATHER_WINDOW
            pltpu.sync_copy(i_hbm.at[pl.ds(base, GATHER_WINDOW)], idx_v)
            for k in range(GATHER_WINDOW // GATHER_ROWS):
                pltpu.sync_copy(x_hbm.at[idx_v.at[pl.ds(k * GATHER_ROWS, GATHER_ROWS)]], buf)
                pltpu.sync_copy(buf, o_hbm.at[pl.ds(base + k * GATHER_ROWS, GATHER_ROWS)])

    return gather(src, idx)


def _expert_kernel(be_ref, nused_ref, x_ref, wg_ref, wu_ref, wd_ref, o_ref, acc_ref):
    b = pl.program_id(0)
    f = pl.program_id(1)
    used = b < nused_ref[0]

    @pl.when(used)
    def _():
        x = x_ref[...].astype(BF16)
        gate = jnp.dot(x, wg_ref[0], preferred_element_type=F32)
        up = jnp.dot(x, wu_ref[0], preferred_element_type=F32)
        act = (jax.nn.silu(gate) * up).astype(BF16)
        y = jnp.dot(act, wd_ref[0], preferred_element_type=F32)

        @pl.when(f == 0)
        def _():
            acc_ref[...] = y

        @pl.when(f > 0)
        def _():
            acc_ref[...] += y

    @pl.when(f == pl.num_programs(1) - 1)
    def _():
        o_ref[...] = jnp.where(used, acc_ref[...], 0.0)


def _experts(x_rows, block_e, n_used, w_gu, w_down, bm, tf=896):
    n_rows = x_rows.shape[0]
    nf = EXPERT_FF // tf
    nb = n_rows // bm

    def fsel(b, f, nused):
        return jnp.where(b < nused[0], f, nf - 1)

    grid_spec = pltpu.PrefetchScalarGridSpec(
        num_scalar_prefetch=2, grid=(nb, nf),
        in_specs=[pl.BlockSpec((bm, D_MODEL), lambda b, f, be, nu: (b, 0)),
                  pl.BlockSpec((1, D_MODEL, tf), lambda b, f, be, nu: (be[b], 0, fsel(b, f, nu))),
                  pl.BlockSpec((1, D_MODEL, tf), lambda b, f, be, nu: (be[b], 0, nf + fsel(b, f, nu))),
                  pl.BlockSpec((1, tf, D_MODEL), lambda b, f, be, nu: (be[b], fsel(b, f, nu), 0))],
        out_specs=pl.BlockSpec((bm, D_MODEL), lambda b, f, be, nu: (b, 0)),
        scratch_shapes=[pltpu.VMEM((bm, D_MODEL), F32)])
    return pl.pallas_call(
        _expert_kernel, grid_spec=grid_spec,
        out_shape=jax.ShapeDtypeStruct((n_rows, D_MODEL), F32),
        compiler_params=_cparams(("arbitrary", "arbitrary")),
        name="experts",
    )(block_e, n_used, x_rows, w_gu, w_gu, w_down)


def _combine_kernel(x_ref, y0_ref, y1_ref, r_ref, g_ref, o_ref):
    r = r_ref[...]
    x = x_ref[...] + r[:, 2:3] * y0_ref[...] + r[:, 3:4] * y1_ref[...]
    o_ref[...] = _rms(x, g_ref[...])


def _combine(x, y_slots, route, g_final, tm=512):
    T = x.shape[0]
    return pl.pallas_call(
        _combine_kernel, grid=(T // tm,),
        in_specs=[pl.BlockSpec((tm, D_MODEL), lambda i: (i, 0)),
                  pl.BlockSpec((tm, D_MODEL), lambda i: (i, 0)),
                  pl.BlockSpec((tm, D_MODEL), lambda i: (T // tm + i, 0)),
                  pl.BlockSpec((tm, LANES), lambda i: (i, 0)),
                  pl.BlockSpec((1, D_MODEL), lambda i: (0, 0))],
        out_specs=pl.BlockSpec((tm, D_MODEL), lambda i: (i, 0)),
        out_shape=jax.ShapeDtypeStruct((T, D_MODEL), F32),
        compiler_params=_cparams(("parallel",)),
        name="combine",
    )(x, y_slots, y_slots, route, g_final)


def _pad_cols(w, n):
    return jnp.pad(w, ((0, 0), (0, n - w.shape[1])))


def _split_in_weights(w_in):
    sizes = [MLA_Q_LORA, MLA_KV_LORA, MLA_ROPE_DIM, BRANCH_WIDTH, BRANCH_WIDTH, BRANCH_WIDTH, HEADS,
             BRANCH_WIDTH, BRANCH_WIDTH, BRANCH_WIDTH, 3 * D_MODEL]
    pts = np.cumsum(sizes)[:-1]
    c_q, c_kv, k_pe, fq, fk, fv, fl, mq, mk, mv, gates = jnp.split(w_in, pts, axis=1)
    zeros = lambda n: jnp.zeros((D_MODEL, n), w_in.dtype)
    pe_tile = jnp.concatenate([zeros(HEAD_DIM), k_pe, zeros(LANES - HEAD_DIM - MLA_ROPE_DIM)], axis=1)
    w_small = jnp.concatenate([c_q, c_kv, pe_tile, _pad_cols(fl, LANES)], axis=1)
    w_big = jnp.concatenate([fq, mq, fk, mk, fv, mv, gates], axis=1)
    return w_small.astype(BF16), w_big.astype(BF16)


def _mla_up_weights(w_uq, w_ukv):
    dq = HEAD_DIM + MLA_ROPE_DIM
    wq = w_uq.reshape(MLA_Q_LORA, HEADS, dq)
    wq = jnp.pad(wq, ((0, 0), (0, 0), (0, LANES - dq))).reshape(MLA_Q_LORA, HEADS * LANES)
    wkv = w_ukv.reshape(MLA_KV_LORA, HEADS, 2 * HEAD_DIM)
    wk = jnp.pad(wkv[:, :, :HEAD_DIM], ((0, 0), (0, 0), (0, LANES - HEAD_DIM))).reshape(MLA_KV_LORA, HEADS * LANES)
    wv = wkv[:, :, HEAD_DIM:].reshape(MLA_KV_LORA, BRANCH_WIDTH)
    return wq.astype(BF16), wk.astype(BF16), wv.astype(BF16)


def _chunk_rows(a, B, S, t):
    return a.reshape(B, S // t, t, a.shape[-1])


def _value_tiles(v, B, S, t):
    v = v.reshape(B, S // t, t, HEADS, HEAD_DIM)
    pad = jnp.zeros(v.shape[:-1] + (V_ROWS - HEAD_DIM,), v.dtype).at[..., 0].set(1)
    return jnp.concatenate([v, pad], axis=-1).reshape(B, S // t, t, HEADS * V_ROWS).transpose(0, 1, 3, 2)


ATT_TQ = 512
ATT_TK = 256
MOBA_TQ = 512
MOE_BM = 512


def _token_mixers(x, B, S, g_mix, w_in, g_q_lat, g_kv_lat, w_uq, w_ukv, b_forget, w_branch, w_out,
                  tabs_mla, tabs_moba):
    T = B * S
    w_small, w_big = _split_in_weights(w_in)
    wq, wk, wv = _mla_up_weights(w_uq, w_ukv)
    g = g_mix[None, :]

    big = _inproj(x, g, w_big, tabs_moba)
    q_mla, k_mla, v_mla, f_logit = _mla_prep(
        x, g, w_small, g_q_lat[None, :], g_kv_lat[None, :], wq, wk, wv, tabs_mla,
        (HEAD_DIM + MLA_ROPE_DIM) ** -0.5 * LOG2E)

    y_mla = _flash(q_mla.reshape(B, S, -1), _chunk_rows(k_mla, B, S, ATT_TK),
                   _value_tiles(v_mla, B, S, ATT_TK), ATT_TQ, ATT_TK)

    c = _cumlogf(f_logit.reshape(B, S, HEADS).transpose(0, 2, 1), b_forget[:, None])
    c_rows = c.transpose(0, 2, 1).reshape(T, HEADS)
    q_fox, k_fox = _fox_prep(big, c_rows)
    fv = big[:, COL_FV:COL_FV + BRANCH_WIDTH]
    y_fox = _flash(q_fox.reshape(B, S, -1), _chunk_rows(k_fox, B, S, ATT_TK),
                   _value_tiles(fv, B, S, ATT_TK), ATT_TQ, ATT_TK)

    mv = big[:, COL_MV:COL_MV + BRANCH_WIDTH]
    y_moba = _moba(big.reshape(B, S, -1), _chunk_rows(big, B, S, MOBA_BLOCK),
                   _value_tiles(mv, B, S, MOBA_BLOCK), MOBA_TQ)

    return _merge(x, y_mla.reshape(T, -1), y_fox.reshape(T, -1), y_moba.reshape(T, -1), big,
                  w_branch.astype(BF16), w_out.astype(BF16))


def _moe(x, g_ffn, w_router, w_gu, w_down, g_final):
    T = x.shape[0]
    A = 2 * T
    bm = MOE_BM
    h, route = _router(x, g_ffn[None, :], _pad_cols(w_router, LANES))
    top_e = route[:, :2].astype(jnp.int32).reshape(A)
    onehot = (top_e[:, None] == jnp.arange(N_EXPERTS)[None, :]).astype(jnp.int32)
    csum = jnp.cumsum(onehot, axis=0)
    counts = csum[-1]
    rank = jnp.sum((csum - onehot) * onehot, axis=1)
    padded = (counts + bm - 1) // bm * bm
    pad_end = jnp.cumsum(padded)
    dest = (pad_end - padded)[top_e] + rank
    n_rows = (A // bm + N_EXPERTS) * bm
    row_tok = jnp.zeros((n_rows,), jnp.int32).at[dest].set(jnp.arange(A, dtype=jnp.int32) // 2)
    block_e = jnp.minimum(jnp.searchsorted(pad_end, jnp.arange(n_rows // bm) * bm, side="right"),
                          N_EXPERTS - 1).astype(jnp.int32)
    n_used = (pad_end[-1:] // bm).astype(jnp.int32)

    x_rows = _gather_rows(h, row_tok)
    y_rows = _experts(x_rows, block_e, n_used, w_gu.astype(BF16), w_down.astype(BF16), bm)
    y_slots = _gather_rows(y_rows, dest.astype(jnp.int32).reshape(T, 2).T.reshape(A))
    return _combine(x, y_slots, route, g_final[None, :])


def kernel(x, positions, g_mix, w_in, g_q_lat, g_kv_lat, w_uq, w_ukv, b_forget, w_branch, w_out, g_ffn,
           w_dense_gu, w_dense_down, w_router, w_exp_gu, w_exp_down, g_final):
    B, S, D = x.shape
    T = B * S
    depth = g_mix.shape[0]
    assert depth == 2 and D == D_MODEL and S % MOBA_BLOCK == 0
    pos = positions.reshape(T, 1).astype(jnp.int32)
    tabs_mla = _rope_tables(pos, *_rope_patterns(LANES, HEAD_DIM, MLA_ROPE_DIM // 2))
    tabs_moba = _rope_tables(pos, *_rope_patterns(HEAD_DIM, 0, HEAD_DIM // 2))
    x = x.reshape(T, D)
    for l in range(depth):
        x = _token_mixers(x, B, S, g_mix[l], w_in[l], g_q_lat[l], g_kv_lat[l], w_uq[l], w_ukv[l],
                          b_forget[l], w_branch[l], w_out[l], tabs_mla, tabs_moba)
        if l % 2 == 0:
            x = _dense_ffn(x, g_ffn[l][None, :], w_dense_gu[l // 2].astype(BF16),
                           w_dense_down[l // 2].astype(BF16))
        else:
            x = _moe(x, g_ffn[l], w_router[l // 2], w_exp_gu[l // 2], w_exp_down[l // 2], g_final)
    return x.reshape(B, S, D)
```

```python
import functools
import math

import jax
import jax.numpy as jnp
import numpy as np
from jax import lax
from jax.experimental import pallas as pl
from jax.experimental.pallas import tpu as pltpu
from jax.experimental.pallas import tpu_sc as plsc

F32 = jnp.float32
BF16 = jnp.bfloat16
NEG_INF = float("-inf")
M_INIT = -1e30
LOG2E = math.log2(math.e)

D_MODEL = 1024
RMS_EPS = 1e-6
ROPE_THETA = 10000.0
HEADS = 8
HEAD_DIM = 64
V_ROWS = 80
Q_SCALE = HEAD_DIM ** -0.5 * math.log2(math.e)
MLA_Q_LORA = 256
MLA_KV_LORA = 128
MLA_ROPE_DIM = 32
BRANCH_WIDTH = HEADS * HEAD_DIM
MOBA_BLOCK = 256
MOBA_TOPK = 3
DENSE_FF = 2816
N_EXPERTS = 8
EXPERT_FF = 3584

LANES = 128
VMEM_LIMIT = 48 * 1024 * 1024
EXPERT_VMEM_LIMIT = 58 * 1024 * 1024

COL_FQ, COL_MQ, COL_FK, COL_MK, COL_GATES = (n * BRANCH_WIDTH for n in range(5))
BIG_COLS = COL_GATES + 3 * D_MODEL
BIG_TN = 2 * BRANCH_WIDTH
W_TILES = BIG_COLS // BIG_TN + 1
V_TILE = 2
KV_CHUNK = 256
SMALL_COLS = MLA_Q_LORA + MLA_KV_LORA + 2 * LANES


def _cparams(sem, vmem_limit=VMEM_LIMIT):
    return pltpu.CompilerParams(dimension_semantics=sem, vmem_limit_bytes=vmem_limit)


def _rms(x, g):
    return x * lax.rsqrt(jnp.mean(x * x, axis=-1, keepdims=True) + RMS_EPS) * g


def _rope_table_kernel(pos_ref, f_ref, mc_ref, m1_ref, m2_ref, c_ref, s1_ref, s2_ref):
    ang = pos_ref[...].astype(F32) * f_ref[...]
    cos = jnp.cos(ang)
    sin = jnp.sin(ang)
    mc = mc_ref[...]
    c_ref[...] = cos * mc + (1.0 - mc)
    s1_ref[...] = sin * m1_ref[...]
    s2_ref[...] = sin * m2_ref[...]


def _rope_tables(pos, freq, mc, m1, m2, tm=1024):
    T = pos.shape[0]
    row = pl.BlockSpec((tm, 1), lambda i: (i, 0))
    pat = pl.BlockSpec((1, LANES), lambda i: (0, 0))
    out = pl.BlockSpec((tm, LANES), lambda i: (i, 0))
    shp = jax.ShapeDtypeStruct((T, LANES), F32)
    return pl.pallas_call(
        _rope_table_kernel, grid=(T // tm,),
        in_specs=[row, pat, pat, pat, pat], out_specs=[out, out, out],
        out_shape=[shp, shp, shp], compiler_params=_cparams(("parallel",)),
        name="rope_tables",
    )(pos, freq, mc, m1, m2)


def _rope_patterns(group, x1_lo, half):
    d = 2 * half
    inv_freq = jnp.exp(-math.log(ROPE_THETA) * jnp.arange(half, dtype=F32) * 2.0 / d)
    lane = np.arange(LANES) % group
    in_x1 = (lane >= x1_lo) & (lane < x1_lo + half)
    in_x2 = (lane >= x1_lo + half) & (lane < x1_lo + d)
    k = np.where(in_x1, lane - x1_lo, np.where(in_x2, lane - x1_lo - half, 0))
    freq = jnp.where(jnp.asarray(in_x1 | in_x2), inv_freq[k], 0.0)[None, :].astype(F32)
    mc = jnp.asarray((in_x1 | in_x2).astype(np.float32))[None, :]
    m1 = jnp.asarray(-(in_x1.astype(np.float32)))[None, :]
    m2 = jnp.asarray(in_x2.astype(np.float32))[None, :]
    return freq, mc, m1, m2


def _apply_rope(x, c, s1, s2, half):
    n = x.shape[-1]
    reps = n // LANES
    c, s1, s2 = (jnp.tile(t, (1, reps)) if reps > 1 else t for t in (c, s1, s2))
    return x * c + pltpu.roll(x, n - half, 1) * s1 + pltpu.roll(x, half, 1) * s2


def _store_value_tiles(vt_ref, a):
    row = lax.broadcasted_iota(jnp.int32, (V_ROWS - HEAD_DIM, KV_CHUNK), 0)
    pad = jnp.where(row == 0, 1.0, 0.0).astype(BF16)
    for c in range(a.shape[0] // KV_CHUNK):
        at = a[c * KV_CHUNK:(c + 1) * KV_CHUNK, :].T.astype(BF16)
        for h in range(HEADS):
            vt_ref[0, c, h * V_ROWS:h * V_ROWS + HEAD_DIM, :] = at[h * HEAD_DIM:(h + 1) * HEAD_DIM]
            vt_ref[0, c, h * V_ROWS + HEAD_DIM:(h + 1) * V_ROWS, :] = pad


def _value_tile_spec(tm, S, grid_rank):
    per = S // tm
    imap = (lambda i: (i // per, i % per, 0, 0)) if grid_rank == 1 else (lambda i, j: (i // per, i % per, 0, 0))
    return pl.BlockSpec((1, tm // KV_CHUNK, HEADS * V_ROWS, KV_CHUNK), imap)


def _value_tile_shape(B, S):
    return jax.ShapeDtypeStruct((B, S // KV_CHUNK, HEADS * V_ROWS, KV_CHUNK), BF16)


def _inproj_kernel(x_ref, g_ref, w_ref, c_ref, s1_ref, s2_ref, o_ref, vf_ref, vm_ref, h_ref):
    j = pl.program_id(1)

    @pl.when(j == 0)
    def _():
        h_ref[...] = _rms(x_ref[...], g_ref[...]).astype(BF16)

    acc = jnp.dot(h_ref[...], w_ref[...], preferred_element_type=F32)

    @pl.when(j < 2)
    def _():
        a = acc * jnp.where(j == 0, Q_SCALE, 1.0)
        o_ref[:, :BRANCH_WIDTH] = a[:, :BRANCH_WIDTH].astype(BF16)
        o_ref[:, BRANCH_WIDTH:] = _apply_rope(a[:, BRANCH_WIDTH:], c_ref[...], s1_ref[...], s2_ref[...],
                                              HEAD_DIM // 2).astype(BF16)

    @pl.when(j == V_TILE)
    def _():
        _store_value_tiles(vf_ref, acc[:, :BRANCH_WIDTH])
        _store_value_tiles(vm_ref, acc[:, BRANCH_WIDTH:])

    @pl.when(j > V_TILE)
    def _():
        o_ref[...] = acc.astype(BF16)


def _inproj(x, g, w_big, tabs, B, S, tm=1024):
    T = x.shape[0]
    tab = pl.BlockSpec((tm, LANES), lambda i, j: (i, 0))
    vspec = _value_tile_spec(tm, S, 2)
    return pl.pallas_call(
        _inproj_kernel, grid=(T // tm, W_TILES),
        in_specs=[pl.BlockSpec((tm, D_MODEL), lambda i, j: (i, 0)),
                  pl.BlockSpec((1, D_MODEL), lambda i, j: (0, 0)),
                  pl.BlockSpec((D_MODEL, BIG_TN), lambda i, j: (0, j)),
                  tab, tab, tab],
        out_specs=[pl.BlockSpec((tm, BIG_TN), lambda i, j: (i, jnp.where(j < V_TILE, j, j - 1))), vspec, vspec],
        out_shape=[jax.ShapeDtypeStruct((T, BIG_COLS), BF16), _value_tile_shape(B, S), _value_tile_shape(B, S)],
        scratch_shapes=[pltpu.VMEM((tm, D_MODEL), BF16)],
        compiler_params=_cparams(("parallel", "arbitrary")),
        name="inproj",
    )(x, g, w_big, *tabs)


def _mla_prep_kernel(x_ref, g_ref, ws_ref, gq_ref, gkv_ref, wq_ref, wk_ref, wv_ref,
                     c_ref, s1_ref, s2_ref, q_ref, k_ref, v_ref, fl_ref, *, scale):
    h = _rms(x_ref[...], g_ref[...]).astype(BF16)
    small = jnp.dot(h, ws_ref[...], preferred_element_type=F32)
    c_q = small[:, :MLA_Q_LORA]
    c_kv = small[:, MLA_Q_LORA:MLA_Q_LORA + MLA_KV_LORA]
    k_pe = small[:, MLA_Q_LORA + MLA_KV_LORA:MLA_Q_LORA + MLA_KV_LORA + LANES]
    fl_ref[...] = small[:, SMALL_COLS - LANES:SMALL_COLS - LANES + HEADS]
    c, s1, s2 = c_ref[...], s1_ref[...], s2_ref[...]
    half = MLA_ROPE_DIM // 2
    qn = _rms(c_q, gq_ref[...]).astype(BF16)
    q = jnp.dot(qn, wq_ref[...], preferred_element_type=F32) * scale
    q_ref[...] = _apply_rope(q, c, s1, s2, half).astype(BF16)
    kvn = _rms(c_kv, gkv_ref[...]).astype(BF16)
    k_nope = jnp.dot(kvn, wk_ref[...], preferred_element_type=F32)
    k_rot = _apply_rope(k_pe, c, s1, s2, half)
    k_ref[...] = (k_nope + jnp.tile(k_rot, (1, HEADS))).astype(BF16)
    _store_value_tiles(v_ref, jnp.dot(kvn, wv_ref[...], preferred_element_type=F32))


def _mla_prep(x, g, w_small, g_q, g_kv, wq, wk, wv, tabs, scale, B, S, tm=512):
    T = x.shape[0]
    full = lambda shape: pl.BlockSpec(shape, lambda i: (0,) * len(shape))
    row = lambda n: pl.BlockSpec((tm, n), lambda i: (i, 0))
    qk = HEADS * LANES
    return pl.pallas_call(
        functools.partial(_mla_prep_kernel, scale=scale), grid=(T // tm,),
        in_specs=[row(D_MODEL), full((1, D_MODEL)), full((D_MODEL, SMALL_COLS)),
                  full((1, MLA_Q_LORA)), full((1, MLA_KV_LORA)),
                  full((MLA_Q_LORA, qk)), full((MLA_KV_LORA, qk)), full((MLA_KV_LORA, BRANCH_WIDTH)),
                  row(LANES), row(LANES), row(LANES)],
        out_specs=[row(qk), row(qk), _value_tile_spec(tm, S, 1), row(HEADS)],
        out_shape=[jax.ShapeDtypeStruct((T, qk), BF16), jax.ShapeDtypeStruct((T, qk), BF16),
                   _value_tile_shape(B, S), jax.ShapeDtypeStruct((T, HEADS), F32)],
        compiler_params=_cparams(("parallel",)),
        name="mla_prep",
    )(x, g, w_small, g_q, g_kv, wq, wk, wv, *tabs)


def _cumlogf_kernel(fl_ref, b_ref, c_ref):
    z = fl_ref[0] + b_ref[...]
    x = jnp.minimum(z, 0.0) - jnp.log1p(jnp.exp(-jnp.abs(z)))
    n = x.shape[-1]
    lane = lax.broadcasted_iota(jnp.int32, x.shape, 1)
    d = 1
    while d < n:
        x = x + jnp.where(lane >= d, pltpu.roll(x, d, 1), 0.0)
        d *= 2
    c_ref[0] = x


def _cumlogf(fl_t, b_col):
    B, H, S = fl_t.shape
    return pl.pallas_call(
        _cumlogf_kernel, grid=(B,),
        in_specs=[pl.BlockSpec((1, H, S), lambda b: (b, 0, 0)), pl.BlockSpec((H, 1), lambda b: (0, 0))],
        out_specs=pl.BlockSpec((1, H, S), lambda b: (b, 0, 0)),
        out_shape=jax.ShapeDtypeStruct((B, H, S), F32),
        compiler_params=_cparams(("parallel",)),
        name="cumlogf",
    )(fl_t, b_col)


def _split3(c):
    hi = c.astype(BF16)
    r = c - hi.astype(F32)
    mid = r.astype(BF16)
    lo = (r - mid.astype(F32)).astype(BF16)
    return hi.astype(F32), mid.astype(F32), lo.astype(F32)


def _fox_prep_kernel(q_ref, k_ref, c_ref, qo_ref, ko_ref):
    tm = q_ref.shape[0]
    lane = lax.broadcasted_iota(jnp.int32, (tm, LANES), 1)
    c = c_ref[...] * LOG2E
    for hp in range(HEADS // 2):
        q2 = q_ref[:, hp * LANES:(hp + 1) * LANES].astype(F32)
        k2 = k_ref[:, hp * LANES:(hp + 1) * LANES].astype(F32)
        for hh in range(2):
            h = 2 * hp + hh
            hi, mid, lo = _split3(c[:, h:h + 1])
            aug_c = jnp.where(lane == HEAD_DIM, hi, jnp.where(lane == HEAD_DIM + 1, mid,
                              jnp.where(lane == HEAD_DIM + 2, lo, 0.0)))
            ones_a = jnp.where((lane >= HEAD_DIM) & (lane < HEAD_DIM + 3), 1.0, 0.0)
            qh = q2 if hh == 0 else pltpu.roll(q2, HEAD_DIM, 1)
            kh = k2 if hh == 0 else pltpu.roll(k2, HEAD_DIM, 1)
            q_aug = jnp.where(lane < HEAD_DIM, qh, ones_a + pltpu.roll(aug_c, 3, 1))
            k_aug = jnp.where(lane < HEAD_DIM, kh, pltpu.roll(ones_a, 3, 1) - aug_c)
            qo_ref[:, h * LANES:(h + 1) * LANES] = q_aug.astype(BF16)
            ko_ref[:, h * LANES:(h + 1) * LANES] = k_aug.astype(BF16)


def _fox_prep(big, c_rows, tm=512):
    T = big.shape[0]
    qk = HEADS * LANES
    return pl.pallas_call(
        _fox_prep_kernel, grid=(T // tm,),
        in_specs=[pl.BlockSpec((tm, BRANCH_WIDTH), lambda i: (i, COL_FQ // BRANCH_WIDTH)),
                  pl.BlockSpec((tm, BRANCH_WIDTH), lambda i: (i, COL_FK // BRANCH_WIDTH)),
                  pl.BlockSpec((tm, HEADS), lambda i: (i, 0))],
        out_specs=[pl.BlockSpec((tm, qk), lambda i: (i, 0)), pl.BlockSpec((tm, qk), lambda i: (i, 0))],
        out_shape=[jax.ShapeDtypeStruct((T, qk), BF16), jax.ShapeDtypeStruct((T, qk), BF16)],
        compiler_params=_cparams(("parallel",)),
        name="fox_prep",
    )(big, big, c_rows)


def _nt_dot(a, b):
    return lax.dot_general(a, b, (((1,), (1,)), ((), ())), preferred_element_type=F32)


def _softmax_step(sts, m, acc, vts):
    m_new = m
    for st in sts:
        m_new = jnp.maximum(m_new, jnp.max(st, axis=0, keepdims=True))
    acc = jnp.exp2(m - m_new) * acc
    for st, vt in zip(sts, vts):
        acc = acc + jnp.dot(vt, jnp.exp2(st - m_new).astype(BF16), preferred_element_type=F32)
    return m_new, acc


def _softmax_init(tq):
    return jnp.full((1, tq), M_INIT, F32), jnp.zeros((V_ROWS, tq), F32)


def _softmax_finish(acc):
    return (acc[:HEAD_DIM] / acc[HEAD_DIM:HEAD_DIM + 1]).T


def _attend(npairs, qk, val, past, diag, tq, sa, sb):
    heads = range(2)

    def put(dst, j):
        for hh in heads:
            dst[hh][...] = qk(hh, j)

    def advance(carry, src, j, fn):
        return tuple(_softmax_step([fn(hh, j, src[hh][...])], *carry[hh], [val(hh, j)]) for hh in heads)

    put(sa, 0)

    def body(jj, carry):
        j0 = 2 * jj
        put(sb, j0 + 1)
        carry = advance(carry, sa, j0, past)
        put(sa, j0 + 2)
        return advance(carry, sb, j0 + 1, past)

    carry = lax.fori_loop(0, npairs, body, (_softmax_init(tq), _softmax_init(tq)))
    j0 = 2 * npairs
    put(sb, j0 + 1)
    carry = advance(carry, sa, j0, lambda hh, j, st: diag(hh, 0, st))
    carry = advance(carry, sb, j0 + 1, lambda hh, j, st: diag(hh, 1, st))
    return jnp.concatenate([_softmax_finish(acc) for _, acc in carry], axis=1)


def _score_scratch(tq, tk):
    return [pltpu.VMEM((tk, tq), F32) for _ in range(4)]


def _flash_kernel(q_ref, k_ref, vt_ref, o_ref, sa0, sa1, sb0, sb1, *, tq, tk):
    i = pl.program_id(2)
    assert tq == 2 * tk
    krow = lax.broadcasted_iota(jnp.int32, (tk, tq), 0)
    qcol = lax.broadcasted_iota(jnp.int32, (tk, tq), 1)

    def qk(hh, j):
        return _nt_dot(k_ref[0, j, :, hh * LANES:(hh + 1) * LANES], q_ref[0, :, hh * LANES:(hh + 1) * LANES])

    def val(hh, j):
        return vt_ref[0, j, hh * V_ROWS:(hh + 1) * V_ROWS, :]

    def diag(hh, d, st):
        return jnp.where(d * tk + krow <= qcol, st, NEG_INF)

    out = _attend(i, qk, val, lambda hh, j, st: st, diag, tq, (sa0, sa1), (sb0, sb1))
    o_ref[0] = out.astype(BF16)


def _flash(q, k, vt, tq, tk):
    B, S, _ = q.shape
    nk = S // tk
    return pl.pallas_call(
        functools.partial(_flash_kernel, tq=tq, tk=tk), grid=(B, HEADS // 2, S // tq),
        in_specs=[pl.BlockSpec((1, tq, 2 * LANES), lambda b, h, i: (b, i, h)),
                  pl.BlockSpec((1, nk, tk, 2 * LANES), lambda b, h, i: (b, 0, 0, h)),
                  pl.BlockSpec((1, nk, 2 * V_ROWS, tk), lambda b, h, i: (b, 0, h, 0))],
        out_specs=pl.BlockSpec((1, tq, LANES), lambda b, h, i: (b, i, h)),
        out_shape=jax.ShapeDtypeStruct((B, S, BRANCH_WIDTH), BF16),
        scratch_shapes=_score_scratch(tq, tk),
        compiler_params=_cparams(("parallel", "parallel", "arbitrary")),
        name="flash",
    )(q, k, vt)


def _moba_kernel(q_ref, k_ref, vt_ref, o_ref, kmean_ref, bias_ref, qm_ref, sa0, sa1, sb0, sb1, *, nblk, tq):
    i = pl.program_id(2)
    blk = MOBA_BLOCK
    shift = blk.bit_length() - 1
    r = tq // blk

    @pl.when(i == 0)
    def _():
        for n in range(nblk):
            kmean_ref[n:n + 1, :] = jnp.mean(k_ref[0, n].astype(F32), axis=0, keepdims=True)

    lane = lax.broadcasted_iota(jnp.int32, (tq, LANES), 1)
    blk_id = lax.broadcasted_iota(jnp.int32, (nblk, tq), 0)
    own = i * r + (lax.broadcasted_iota(jnp.int32, (nblk, tq), 1) >> shift)
    q2 = q_ref[0]
    for hh in range(2):
        in_head = (lane >= hh * HEAD_DIM) & (lane < (hh + 1) * HEAD_DIM)
        q = jnp.where(in_head, q2, jnp.zeros_like(q2))
        qm_ref[hh] = q
        g = lax.dot_general(kmean_ref[...], q.astype(F32), (((1,), (1,)), ((), ())),
                            precision=lax.Precision.HIGHEST, preferred_element_type=F32)
        g = jnp.where(blk_id < own, g, NEG_INF)
        bias = jnp.full((nblk, tq), NEG_INF, F32)
        for _ in range(MOBA_TOPK):
            mx = jnp.max(g, axis=0, keepdims=True)
            first = jnp.min(jnp.where(g == mx, blk_id, nblk), axis=0, keepdims=True)
            pick = (blk_id == first) & (mx > NEG_INF)
            bias = jnp.where(pick, 0.0, bias)
            g = jnp.where(pick, NEG_INF, g)
        bias_ref[hh] = bias

    krow = lax.broadcasted_iota(jnp.int32, (blk, tq), 0)
    qcol = lax.broadcasted_iota(jnp.int32, (blk, tq), 1)

    def qk(hh, n):
        return _nt_dot(k_ref[0, n], qm_ref[hh])

    def val(hh, n):
        return vt_ref[0, n, hh * V_ROWS:(hh + 1) * V_ROWS, :]

    def past(hh, n, st):
        return st + bias_ref[hh, pl.ds(n, 1), :]

    def diag(hh, d, st):
        own_causal = ((qcol >> shift) == d) & (krow <= (qcol & (blk - 1)))
        return jnp.where(own_causal, st, past(hh, i * r + d, st))

    out = _attend(i, qk, val, past, diag, tq, (sa0, sa1), (sb0, sb1))
    o_ref[0] = out.astype(BF16)


def _moba(q, k, vt, tq):
    B, S, _ = q.shape
    nblk = S // MOBA_BLOCK
    assert tq == 2 * MOBA_BLOCK and MOBA_BLOCK & (MOBA_BLOCK - 1) == 0
    qc = COL_MQ // LANES
    kc = COL_MK // LANES
    return pl.pallas_call(
        functools.partial(_moba_kernel, nblk=nblk, tq=tq), grid=(B, HEADS // 2, S // tq),
        in_specs=[pl.BlockSpec((1, tq, LANES), lambda b, h, i: (b, i, qc + h)),
                  pl.BlockSpec((1, nblk, MOBA_BLOCK, LANES), lambda b, h, i: (b, 0, 0, kc + h)),
                  pl.BlockSpec((1, nblk, 2 * V_ROWS, MOBA_BLOCK), lambda b, h, i: (b, 0, h, 0))],
        out_specs=pl.BlockSpec((1, tq, LANES), lambda b, h, i: (b, i, h)),
        out_shape=jax.ShapeDtypeStruct((B, S, BRANCH_WIDTH), BF16),
        scratch_shapes=[pltpu.VMEM((nblk, LANES), F32), pltpu.VMEM((2, nblk, tq), F32),
                        pltpu.VMEM((2, tq, LANES), BF16)] + _score_scratch(tq, MOBA_BLOCK),
        compiler_params=_cparams(("parallel", "parallel", "arbitrary")),
        name="moba",
    )(q, k, vt)


def _merge_kernel(x_ref, ya_ref, yb_ref, yc_ref, ga_ref, gb_ref, gc_ref, wb_ref, wo_ref, o_ref):
    merged = None
    for n, (y_ref, g_ref) in enumerate(((ya_ref, ga_ref), (yb_ref, gb_ref), (yc_ref, gc_ref))):
        proj = jnp.dot(y_ref[...], wb_ref[n], preferred_element_type=F32)
        term = jax.nn.sigmoid(g_ref[...].astype(F32)) * proj
        merged = term if merged is None else merged + term
    o_ref[...] = x_ref[...] + jnp.dot(merged.astype(BF16), wo_ref[...], preferred_element_type=F32)


def _merge(x, y_mla, y_fox, y_moba, big, w_branch, w_out, tm=256):
    T = x.shape[0]
    g0 = COL_GATES // D_MODEL
    row = lambda n: pl.BlockSpec((tm, n), lambda i: (i, 0))
    gate = lambda n: pl.BlockSpec((tm, D_MODEL), lambda i: (i, g0 + n))
    return pl.pallas_call(
        _merge_kernel, grid=(T // tm,),
        in_specs=[row(D_MODEL), row(BRANCH_WIDTH), row(BRANCH_WIDTH), row(BRANCH_WIDTH),
                  gate(0), gate(1), gate(2),
                  pl.BlockSpec((3, BRANCH_WIDTH, D_MODEL), lambda i: (0, 0, 0)),
                  pl.BlockSpec((D_MODEL, D_MODEL), lambda i: (0, 0))],
        out_specs=row(D_MODEL),
        out_shape=jax.ShapeDtypeStruct((T, D_MODEL), F32),
        compiler_params=_cparams(("parallel",)),
        name="merge",
    )(x, y_mla, y_fox, y_moba, big, big, big, w_branch, w_out)


def _dense_ffn_kernel(x_ref, g_ref, wg_ref, wu_ref, wd_ref, o_ref):
    x = x_ref[...]
    h = _rms(x, g_ref[...]).astype(BF16)
    gate = jnp.dot(h, wg_ref[...], preferred_element_type=F32)
    up = jnp.dot(h, wu_ref[...], preferred_element_type=F32)
    act = (jax.nn.silu(gate) * up).astype(BF16)
    o_ref[...] = x + jnp.dot(act, wd_ref[...], preferred_element_type=F32)


def _resident(shape, index_map):
    return pl.BlockSpec(shape, index_map, pipeline_mode=pl.Buffered(1))


def _dense_ffn(x, g, w_gu, w_down, tm=512):
    T = x.shape[0]
    return pl.pallas_call(
        _dense_ffn_kernel, grid=(T // tm,),
        in_specs=[pl.BlockSpec((tm, D_MODEL), lambda i: (i, 0)),
                  _resident((1, D_MODEL), lambda i: (0, 0)),
                  _resident((D_MODEL, DENSE_FF), lambda i: (0, 0)),
                  _resident((D_MODEL, DENSE_FF), lambda i: (0, 1)),
                  _resident((DENSE_FF, D_MODEL), lambda i: (0, 0))],
        out_specs=pl.BlockSpec((tm, D_MODEL), lambda i: (i, 0)),
        out_shape=jax.ShapeDtypeStruct((T, D_MODEL), F32),
        compiler_params=_cparams(("parallel",)),
        name="dense_ffn",
    )(x, g, w_gu, w_gu, w_down)


def _router_kernel(x_ref, g_ref, wr_ref, h_ref, r_ref):
    h = _rms(x_ref[...], g_ref[...])
    h_ref[...] = h
    logits = jnp.dot(h, wr_ref[...], precision=lax.Precision.HIGHEST, preferred_element_type=F32)
    lane = lax.broadcasted_iota(jnp.int32, logits.shape, 1)
    logits = jnp.where(lane < N_EXPERTS, logits, NEG_INF)
    m1 = jnp.max(logits, axis=-1, keepdims=True)
    i1 = jnp.min(jnp.where(logits == m1, lane, LANES), axis=-1, keepdims=True)
    rest = jnp.where(lane == i1, NEG_INF, logits)
    m2 = jnp.max(rest, axis=-1, keepdims=True)
    i2 = jnp.min(jnp.where(rest == m2, lane, LANES), axis=-1, keepdims=True)
    e2 = jnp.exp(m2 - m1)
    w1 = 1.0 / (1.0 + e2)
    w2 = e2 / (1.0 + e2)
    r_ref[...] = jnp.where(lane == 0, i1.astype(F32), jnp.where(lane == 1, i2.astype(F32),
                           jnp.where(lane == 2, w1, jnp.where(lane == 3, w2, 0.0))))


def _router(x, g, w_router_pad, tm=512):
    T = x.shape[0]
    return pl.pallas_call(
        _router_kernel, grid=(T // tm,),
        in_specs=[pl.BlockSpec((tm, D_MODEL), lambda i: (i, 0)),
                  pl.BlockSpec((1, D_MODEL), lambda i: (0, 0)),
                  pl.BlockSpec((D_MODEL, LANES), lambda i: (0, 0))],
        out_specs=[pl.BlockSpec((tm, D_MODEL), lambda i: (i, 0)), pl.BlockSpec((tm, LANES), lambda i: (i, 0))],
        out_shape=[jax.ShapeDtypeStruct((T, D_MODEL), F32), jax.ShapeDtypeStruct((T, LANES), F32)],
        compiler_params=_cparams(("parallel",)),
        name="router",
    )(x, g, w_router_pad)


GATHER_WINDOW = 128
GATHER_ROWS = 32


def _gather_rows(src, idx):
    M = idx.shape[0]
    C = src.shape[1]
    mesh = plsc.VectorSubcoreMesh(core_axis_name="core", subcore_axis_name="subcore")
    per = M // (mesh.num_cores * mesh.num_subcores)
    assert per * mesh.num_cores * mesh.num_subcores == M and per % GATHER_WINDOW == 0

    @pl.kernel(out_type=jax.ShapeDtypeStruct((M, C), src.dtype), mesh=mesh, name="gather_rows",
               scratch_types=[pltpu.VMEM((GATHER_WINDOW,), jnp.int32), pltpu.VMEM((GATHER_ROWS, C), src.dtype)])
    def gather(x_hbm, i_hbm, o_hbm, idx_v, buf):
        w = lax.axis_index("core") * mesh.num_subcores + lax.axis_index("subcore")

        @pl.loop(0, per // GATHER_WINDOW)
        def _(t):
            base = w * per + t * GATHER_WINDOW
            pltpu.sync_copy(i_hbm.at[pl.ds(base, GATHER_WINDOW)], idx_v)
            for k in range(GATHER_WINDOW // GATHER_ROWS):
                pltpu.sync_copy(x_hbm.at[idx_v.at[pl.ds(k * GATHER_ROWS, GATHER_ROWS)]], buf)
                pltpu.sync_copy(buf, o_hbm.at[pl.ds(base + k * GATHER_ROWS, GATHER_ROWS)])

    return gather(src, idx)


def _scatter_rows(src, dest0, dest1, n_rows):
    T, C = src.shape
    mesh = plsc.VectorSubcoreMesh(core_axis_name="core", subcore_axis_name="subcore")
    per = T // (mesh.num_cores * mesh.num_subcores)
    assert per * mesh.num_cores * mesh.num_subcores == T and per % GATHER_WINDOW == 0

    @pl.kernel(out_type=jax.ShapeDtypeStruct((n_rows, C), src.dtype), mesh=mesh, name="scatter_rows",
               scratch_types=[pltpu.VMEM((GATHER_WINDOW,), jnp.int32), pltpu.VMEM((GATHER_WINDOW,), jnp.int32),
                              pltpu.VMEM((GATHER_ROWS, C), src.dtype)])
    def scatter(x_hbm, d0_hbm, d1_hbm, o_hbm, i0, i1, buf):
        w = lax.axis_index("core") * mesh.num_subcores + lax.axis_index("subcore")

        @pl.loop(0, per // GATHER_WINDOW)
        def _(t):
            base = w * per + t * GATHER_WINDOW
            pltpu.sync_copy(d0_hbm.at[pl.ds(base, GATHER_WINDOW)], i0)
            pltpu.sync_copy(d1_hbm.at[pl.ds(base, GATHER_WINDOW)], i1)
            for k in range(GATHER_WINDOW // GATHER_ROWS):
                pltpu.sync_copy(x_hbm.at[pl.ds(base + k * GATHER_ROWS, GATHER_ROWS)], buf)
                pltpu.sync_copy(buf, o_hbm.at[i0.at[pl.ds(k * GATHER_ROWS, GATHER_ROWS)]])
                pltpu.sync_copy(buf, o_hbm.at[i1.at[pl.ds(k * GATHER_ROWS, GATHER_ROWS)]])

    return scatter(src, dest0, dest1)


def _expert_kernel(be_ref, nused_ref, valid_ref, x_ref, wg_ref, wu_ref, wd_ref, o_ref, xb_ref, acc_ref):
    b = pl.program_id(0)
    f = pl.program_id(1)
    last = pl.num_programs(1) - 1
    used = b < nused_ref[0]

    @pl.when(used)
    def _():
        @pl.when(f == 0)
        def _():
            row = lax.broadcasted_iota(jnp.int32, (x_ref.shape[0], 1), 0)
            xb_ref[...] = jnp.where(row < valid_ref[b], x_ref[...], 0.0).astype(BF16)

        x = xb_ref[...]
        gate = jnp.dot(x, wg_ref[0], preferred_element_type=F32)
        up = jnp.dot(x, wu_ref[0], preferred_element_type=F32)
        act = (jax.nn.silu(gate) * up).astype(BF16)
        y = jnp.dot(act, wd_ref[0], preferred_element_type=F32)

        @pl.when(f == 0)
        def _():
            acc_ref[...] = y

        @pl.when(jnp.logical_and(f > 0, f < last))
        def _():
            acc_ref[...] += y

        @pl.when(f == last)
        def _():
            o_ref[...] = acc_ref[...] + y

    @pl.when(jnp.logical_and(f == last, jnp.logical_not(used)))
    def _():
        o_ref[...] = jnp.zeros_like(o_ref)


def _experts(x_rows, block_e, n_used, valid, w_gu, w_down, bm, tf=1792):
    n_rows = x_rows.shape[0]
    nf = EXPERT_FF // tf
    nb = n_rows // bm

    def fsel(b, f, nused):
        return jnp.where(b < nused[0], f, nf - 1)

    grid_spec = pltpu.PrefetchScalarGridSpec(
        num_scalar_prefetch=3, grid=(nb, nf),
        in_specs=[pl.BlockSpec((bm, D_MODEL), lambda b, f, be, nu, va: (b, 0)),
                  pl.BlockSpec((1, D_MODEL, tf), lambda b, f, be, nu, va: (be[b], 0, fsel(b, f, nu))),
                  pl.BlockSpec((1, D_MODEL, tf), lambda b, f, be, nu, va: (be[b], 0, nf + fsel(b, f, nu))),
                  pl.BlockSpec((1, tf, D_MODEL), lambda b, f, be, nu, va: (be[b], fsel(b, f, nu), 0))],
        out_specs=pl.BlockSpec((bm, D_MODEL), lambda b, f, be, nu, va: (b, 0)),
        scratch_shapes=[pltpu.VMEM((bm, D_MODEL), BF16), pltpu.VMEM((bm, D_MODEL), F32)])
    assert nf >= 2
    return pl.pallas_call(
        _expert_kernel, grid_spec=grid_spec,
        out_shape=jax.ShapeDtypeStruct((n_rows, D_MODEL), F32),
        compiler_params=_cparams(("arbitrary", "arbitrary"), EXPERT_VMEM_LIMIT),
        name="experts",
    )(block_e, n_used, valid, x_rows, w_gu, w_gu, w_down)


def _combine_kernel(x_ref, y0_ref, y1_ref, r_ref, g_ref, o_ref):
    r = r_ref[...]
    x = x_ref[...] + r[:, 2:3] * y0_ref[...] + r[:, 3:4] * y1_ref[...]
    o_ref[...] = _rms(x, g_ref[...])


def _combine(x, y_slots, route, g_final, tm=512):
    T = x.shape[0]
    return pl.pallas_call(
        _combine_kernel, grid=(T // tm,),
        in_specs=[pl.BlockSpec((tm, D_MODEL), lambda i: (i, 0)),
                  pl.BlockSpec((tm, D_MODEL), lambda i: (i, 0)),
                  pl.BlockSpec((tm, D_MODEL), lambda i: (T // tm + i, 0)),
                  pl.BlockSpec((tm, LANES), lambda i: (i, 0)),
                  pl.BlockSpec((1, D_MODEL), lambda i: (0, 0))],
        out_specs=pl.BlockSpec((tm, D_MODEL), lambda i: (i, 0)),
        out_shape=jax.ShapeDtypeStruct((T, D_MODEL), F32),
        compiler_params=_cparams(("parallel",)),
        name="combine",
    )(x, y_slots, y_slots, route, g_final)


def _pad_cols(w, n):
    return jnp.pad(w, ((0, 0), (0, n - w.shape[1])))


def _split_in_weights(w_in):
    sizes = [MLA_Q_LORA, MLA_KV_LORA, MLA_ROPE_DIM, BRANCH_WIDTH, BRANCH_WIDTH, BRANCH_WIDTH, HEADS,
             BRANCH_WIDTH, BRANCH_WIDTH, BRANCH_WIDTH, 3 * D_MODEL]
    pts = np.cumsum(sizes)[:-1]
    c_q, c_kv, k_pe, fq, fk, fv, fl, mq, mk, mv, gates = jnp.split(w_in, pts, axis=1)
    zeros = lambda n: jnp.zeros((D_MODEL, n), w_in.dtype)
    pe_tile = jnp.concatenate([zeros(HEAD_DIM), k_pe, zeros(LANES - HEAD_DIM - MLA_ROPE_DIM)], axis=1)
    w_small = jnp.concatenate([c_q, c_kv, pe_tile, _pad_cols(fl, LANES)], axis=1)
    w_big = jnp.concatenate([fq, mq, fk, mk, fv, mv, gates], axis=1)
    return w_small.astype(BF16), w_big.astype(BF16)


def _mla_up_weights(w_uq, w_ukv):
    dq = HEAD_DIM + MLA_ROPE_DIM
    wq = w_uq.reshape(MLA_Q_LORA, HEADS, dq)
    wq = jnp.pad(wq, ((0, 0), (0, 0), (0, LANES - dq))).reshape(MLA_Q_LORA, HEADS * LANES)
    wkv = w_ukv.reshape(MLA_KV_LORA, HEADS, 2 * HEAD_DIM)
    wk = jnp.pad(wkv[:, :, :HEAD_DIM], ((0, 0), (0, 0), (0, LANES - HEAD_DIM))).reshape(MLA_KV_LORA, HEADS * LANES)
    wv = wkv[:, :, HEAD_DIM:].reshape(MLA_KV_LORA, BRANCH_WIDTH)
    return wq.astype(BF16), wk.astype(BF16), wv.astype(BF16)


def _chunk_rows(a, B, S, t):
    return a.reshape(B, S // t, t, a.shape[-1])


ATT_TQ = 512
ATT_TK = KV_CHUNK
MOBA_TQ = 512
MOE_BM = 512


def _token_mixers(x, B, S, g_mix, w_in, g_q_lat, g_kv_lat, w_uq, w_ukv, b_forget, w_branch, w_out,
                  tabs_mla, tabs_moba):
    T = B * S
    w_small, w_big = _split_in_weights(w_in)
    wq, wk, wv = _mla_up_weights(w_uq, w_ukv)
    g = g_mix[None, :]

    big, vt_fox, vt_moba = _inproj(x, g, w_big, tabs_moba, B, S)
    q_mla, k_mla, vt_mla, f_logit = _mla_prep(
        x, g, w_small, g_q_lat[None, :], g_kv_lat[None, :], wq, wk, wv, tabs_mla,
        (HEAD_DIM + MLA_ROPE_DIM) ** -0.5 * LOG2E, B, S)

    y_mla = _flash(q_mla.reshape(B, S, -1), _chunk_rows(k_mla, B, S, ATT_TK), vt_mla, ATT_TQ, ATT_TK)

    c = _cumlogf(f_logit.reshape(B, S, HEADS).transpose(0, 2, 1), b_forget[:, None])
    c_rows = c.transpose(0, 2, 1).reshape(T, HEADS)
    q_fox, k_fox = _fox_prep(big, c_rows)
    y_fox = _flash(q_fox.reshape(B, S, -1), _chunk_rows(k_fox, B, S, ATT_TK), vt_fox, ATT_TQ, ATT_TK)

    y_moba = _moba(big.reshape(B, S, -1), _chunk_rows(big, B, S, MOBA_BLOCK), vt_moba, MOBA_TQ)

    return _merge(x, y_mla.reshape(T, -1), y_fox.reshape(T, -1), y_moba.reshape(T, -1), big,
                  w_branch.astype(BF16), w_out.astype(BF16))


def _moe(x, g_ffn, w_router, w_gu, w_down, g_final):
    T = x.shape[0]
    A = 2 * T
    bm = MOE_BM
    h, route = _router(x, g_ffn[None, :], _pad_cols(w_router, LANES))
    top_e = route[:, :2].astype(jnp.int32).reshape(A)
    onehot = (top_e[:, None] == jnp.arange(N_EXPERTS)[None, :]).astype(jnp.int32)
    csum = jnp.cumsum(onehot, axis=0)
    counts = csum[-1]
    rank = jnp.sum((csum - onehot) * onehot, axis=1)
    padded = (counts + bm - 1) // bm * bm
    pad_end = jnp.cumsum(padded)
    dest = (pad_end - padded)[top_e] + rank
    n_rows = (A // bm + N_EXPERTS) * bm
    block_start = jnp.arange(n_rows // bm) * bm
    block_e = jnp.minimum(jnp.searchsorted(pad_end, block_start, side="right"), N_EXPERTS - 1).astype(jnp.int32)
    n_used = (pad_end[-1:] // bm).astype(jnp.int32)
    valid = jnp.clip((pad_end - padded + counts)[block_e] - block_start, 0, bm).astype(jnp.int32)
    dest_slots = dest.astype(jnp.int32).reshape(T, 2).T

    x_rows = _scatter_rows(h, dest_slots[0], dest_slots[1], n_rows)
    y_rows = _experts(x_rows, block_e, n_used, valid, w_gu.astype(BF16), w_down.astype(BF16), bm)
    y_slots = _gather_rows(y_rows, dest_slots.reshape(A))
    return _combine(x, y_slots, route, g_final[None, :])


def kernel(x, positions, g_mix, w_in, g_q_lat, g_kv_lat, w_uq, w_ukv, b_forget, w_branch, w_out, g_ffn,
           w_dense_gu, w_dense_down, w_router, w_exp_gu, w_exp_down, g_final):
    B, S, D = x.shape
    T = B * S
    depth = g_mix.shape[0]
    assert depth == 2 and D == D_MODEL and S % MOBA_BLOCK == 0 and KV_CHUNK == MOBA_BLOCK
    pos = positions.reshape(T, 1).astype(jnp.int32)
    tabs_mla = _rope_tables(pos, *_rope_patterns(LANES, HEAD_DIM, MLA_ROPE_DIM // 2))
    tabs_moba = _rope_tables(pos, *_rope_patterns(HEAD_DIM, 0, HEAD_DIM // 2))
    x = x.reshape(T, D)
    for l in range(depth):
        x = _token_mixers(x, B, S, g_mix[l], w_in[l], g_q_lat[l], g_kv_lat[l], w_uq[l], w_ukv[l],
                          b_forget[l], w_branch[l], w_out[l], tabs_mla, tabs_moba)
        if l % 2 == 0:
            x = _dense_ffn(x, g_ffn[l][None, :], w_dense_gu[l // 2].astype(BF16),
                           w_dense_down[l // 2].astype(BF16))
        else:
            x = _moe(x, g_ffn[l], w_router[l // 2], w_exp_gu[l // 2], w_exp_down[l // 2], g_final)
    return x.reshape(B, S, D)
```

```python
import functools
import math

import jax
import jax.numpy as jnp
import numpy as np
from jax import lax
from jax.experimental import pallas as pl
from jax.experimental.pallas import tpu as pltpu
from jax.experimental.pallas import tpu_sc as plsc

F32 = jnp.float32
BF16 = jnp.bfloat16
NEG_INF = float("-inf")
M_INIT = -1e30
LOG2E = math.log2(math.e)

D_MODEL = 1024
RMS_EPS = 1e-6
ROPE_THETA = 10000.0
HEADS = 8
HEAD_DIM = 64
V_ROWS = 80
Q_SCALE = HEAD_DIM ** -0.5 * math.log2(math.e)
MLA_Q_LORA = 256
MLA_KV_LORA = 128
MLA_ROPE_DIM = 32
BRANCH_WIDTH = HEADS * HEAD_DIM
MOBA_BLOCK = 256
MOBA_TOPK = 3
DENSE_FF = 2816
N_EXPERTS = 8
EXPERT_FF = 3584

LANES = 128
VMEM_LIMIT = 48 * 1024 * 1024
EXPERT_VMEM_LIMIT = 58 * 1024 * 1024

COL_FQ, COL_MQ, COL_FK, COL_MK, COL_GATES = (n * BRANCH_WIDTH for n in range(5))
BIG_COLS = COL_GATES + 3 * D_MODEL
BIG_TN = 2 * BRANCH_WIDTH
W_TILES = BIG_COLS // BIG_TN + 1
V_TILE = 2
KV_CHUNK = 256
FLASH_HEADS = 4
MOBA_STEP_HEADS = 2
SMALL_COLS = MLA_Q_LORA + MLA_KV_LORA + 2 * LANES


def _cparams(sem, vmem_limit=VMEM_LIMIT):
    return pltpu.CompilerParams(dimension_semantics=sem, vmem_limit_bytes=vmem_limit)


def _rms(x, g):
    return x * lax.rsqrt(jnp.mean(x * x, axis=-1, keepdims=True) + RMS_EPS) * g


def _rope_table_kernel(pos_ref, f_ref, mc_ref, m1_ref, m2_ref, c_ref, s1_ref, s2_ref):
    ang = pos_ref[...].astype(F32) * f_ref[...]
    cos = jnp.cos(ang)
    sin = jnp.sin(ang)
    mc = mc_ref[...]
    c_ref[...] = cos * mc + (1.0 - mc)
    s1_ref[...] = sin * m1_ref[...]
    s2_ref[...] = sin * m2_ref[...]


def _rope_tables(pos, freq, mc, m1, m2, tm=1024):
    T = pos.shape[0]
    row = pl.BlockSpec((tm, 1), lambda i: (i, 0))
    pat = pl.BlockSpec((1, LANES), lambda i: (0, 0))
    out = pl.BlockSpec((tm, LANES), lambda i: (i, 0))
    shp = jax.ShapeDtypeStruct((T, LANES), F32)
    return pl.pallas_call(
        _rope_table_kernel, grid=(T // tm,),
        in_specs=[row, pat, pat, pat, pat], out_specs=[out, out, out],
        out_shape=[shp, shp, shp], compiler_params=_cparams(("parallel",)),
        name="rope_tables",
    )(pos, freq, mc, m1, m2)


def _rope_patterns(group, x1_lo, half):
    d = 2 * half
    inv_freq = jnp.exp(-math.log(ROPE_THETA) * jnp.arange(half, dtype=F32) * 2.0 / d)
    lane = np.arange(LANES) % group
    in_x1 = (lane >= x1_lo) & (lane < x1_lo + half)
    in_x2 = (lane >= x1_lo + half) & (lane < x1_lo + d)
    k = np.where(in_x1, lane - x1_lo, np.where(in_x2, lane - x1_lo - half, 0))
    freq = jnp.where(jnp.asarray(in_x1 | in_x2), inv_freq[k], 0.0)[None, :].astype(F32)
    mc = jnp.asarray((in_x1 | in_x2).astype(np.float32))[None, :]
    m1 = jnp.asarray(-(in_x1.astype(np.float32)))[None, :]
    m2 = jnp.asarray(in_x2.astype(np.float32))[None, :]
    return freq, mc, m1, m2


def _apply_rope(x, c, s1, s2, half):
    n = x.shape[-1]
    reps = n // LANES
    c, s1, s2 = (jnp.tile(t, (1, reps)) if reps > 1 else t for t in (c, s1, s2))
    return x * c + pltpu.roll(x, n - half, 1) * s1 + pltpu.roll(x, half, 1) * s2


def _store_value_tiles(vt_ref, a):
    row = lax.broadcasted_iota(jnp.int32, (V_ROWS - HEAD_DIM, KV_CHUNK), 0)
    pad = jnp.where(row == 0, 1.0, 0.0).astype(BF16)
    for c in range(a.shape[0] // KV_CHUNK):
        at = a[c * KV_CHUNK:(c + 1) * KV_CHUNK, :].T.astype(BF16)
        for h in range(HEADS):
            vt_ref[0, c, h * V_ROWS:h * V_ROWS + HEAD_DIM, :] = at[h * HEAD_DIM:(h + 1) * HEAD_DIM]
            vt_ref[0, c, h * V_ROWS + HEAD_DIM:(h + 1) * V_ROWS, :] = pad


def _value_tile_spec(tm, S, grid_rank):
    per = S // tm
    imap = (lambda i: (i // per, i % per, 0, 0)) if grid_rank == 1 else (lambda i, j: (i // per, i % per, 0, 0))
    return pl.BlockSpec((1, tm // KV_CHUNK, HEADS * V_ROWS, KV_CHUNK), imap)


def _value_tile_shape(B, S):
    return jax.ShapeDtypeStruct((B, S // KV_CHUNK, HEADS * V_ROWS, KV_CHUNK), BF16)


def _inproj_kernel(x_ref, g_ref, w_ref, c_ref, s1_ref, s2_ref, o_ref, vf_ref, vm_ref, h_ref):
    j = pl.program_id(1)

    @pl.when(j == 0)
    def _():
        h_ref[...] = _rms(x_ref[...], g_ref[...]).astype(BF16)

    acc = jnp.dot(h_ref[...], w_ref[...], preferred_element_type=F32)

    @pl.when(j < 2)
    def _():
        a = acc * jnp.where(j == 0, Q_SCALE, 1.0)
        o_ref[:, :BRANCH_WIDTH] = a[:, :BRANCH_WIDTH].astype(BF16)
        o_ref[:, BRANCH_WIDTH:] = _apply_rope(a[:, BRANCH_WIDTH:], c_ref[...], s1_ref[...], s2_ref[...],
                                              HEAD_DIM // 2).astype(BF16)

    @pl.when(j == V_TILE)
    def _():
        _store_value_tiles(vf_ref, acc[:, :BRANCH_WIDTH])
        _store_value_tiles(vm_ref, acc[:, BRANCH_WIDTH:])

    @pl.when(j > V_TILE)
    def _():
        o_ref[...] = acc.astype(BF16)


def _inproj(x, g, w_big, tabs, B, S, tm=1024):
    T = x.shape[0]
    tab = pl.BlockSpec((tm, LANES), lambda i, j: (i, 0))
    vspec = _value_tile_spec(tm, S, 2)
    return pl.pallas_call(
        _inproj_kernel, grid=(T // tm, W_TILES),
        in_specs=[pl.BlockSpec((tm, D_MODEL), lambda i, j: (i, 0)),
                  pl.BlockSpec((1, D_MODEL), lambda i, j: (0, 0)),
                  pl.BlockSpec((D_MODEL, BIG_TN), lambda i, j: (0, j)),
                  tab, tab, tab],
        out_specs=[pl.BlockSpec((tm, BIG_TN), lambda i, j: (i, jnp.where(j < V_TILE, j, j - 1))), vspec, vspec],
        out_shape=[jax.ShapeDtypeStruct((T, BIG_COLS), BF16), _value_tile_shape(B, S), _value_tile_shape(B, S)],
        scratch_shapes=[pltpu.VMEM((tm, D_MODEL), BF16)],
        compiler_params=_cparams(("parallel", "arbitrary")),
        name="inproj",
    )(x, g, w_big, *tabs)


def _mla_prep_kernel(x_ref, g_ref, ws_ref, gq_ref, gkv_ref, wq_ref, wk_ref, wv_ref,
                     c_ref, s1_ref, s2_ref, q_ref, k_ref, v_ref, fl_ref, *, scale):
    h = _rms(x_ref[...], g_ref[...]).astype(BF16)
    small = jnp.dot(h, ws_ref[...], preferred_element_type=F32)
    c_q = small[:, :MLA_Q_LORA]
    c_kv = small[:, MLA_Q_LORA:MLA_Q_LORA + MLA_KV_LORA]
    k_pe = small[:, MLA_Q_LORA + MLA_KV_LORA:MLA_Q_LORA + MLA_KV_LORA + LANES]
    fl_ref[...] = small[:, SMALL_COLS - LANES:SMALL_COLS - LANES + HEADS]
    c, s1, s2 = c_ref[...], s1_ref[...], s2_ref[...]
    half = MLA_ROPE_DIM // 2
    qn = _rms(c_q, gq_ref[...]).astype(BF16)
    q = jnp.dot(qn, wq_ref[...], preferred_element_type=F32) * scale
    q_ref[...] = _apply_rope(q, c, s1, s2, half).astype(BF16)
    kvn = _rms(c_kv, gkv_ref[...]).astype(BF16)
    k_nope = jnp.dot(kvn, wk_ref[...], preferred_element_type=F32)
    k_rot = _apply_rope(k_pe, c, s1, s2, half)
    k_ref[...] = (k_nope + jnp.tile(k_rot, (1, HEADS))).astype(BF16)
    _store_value_tiles(v_ref, jnp.dot(kvn, wv_ref[...], preferred_element_type=F32))


def _mla_prep(x, g, w_small, g_q, g_kv, wq, wk, wv, tabs, scale, B, S, tm=512):
    T = x.shape[0]
    full = lambda shape: pl.BlockSpec(shape, lambda i: (0,) * len(shape))
    row = lambda n: pl.BlockSpec((tm, n), lambda i: (i, 0))
    qk = HEADS * LANES
    return pl.pallas_call(
        functools.partial(_mla_prep_kernel, scale=scale), grid=(T // tm,),
        in_specs=[row(D_MODEL), full((1, D_MODEL)), full((D_MODEL, SMALL_COLS)),
                  full((1, MLA_Q_LORA)), full((1, MLA_KV_LORA)),
                  full((MLA_Q_LORA, qk)), full((MLA_KV_LORA, qk)), full((MLA_KV_LORA, BRANCH_WIDTH)),
                  row(LANES), row(LANES), row(LANES)],
        out_specs=[row(qk), row(qk), _value_tile_spec(tm, S, 1), row(HEADS)],
        out_shape=[jax.ShapeDtypeStruct((T, qk), BF16), jax.ShapeDtypeStruct((T, qk), BF16),
                   _value_tile_shape(B, S), jax.ShapeDtypeStruct((T, HEADS), F32)],
        compiler_params=_cparams(("parallel",)),
        name="mla_prep",
    )(x, g, w_small, g_q, g_kv, wq, wk, wv, *tabs)


def _cumlogf_kernel(fl_ref, b_ref, c_ref):
    z = fl_ref[0] + b_ref[...]
    x = jnp.minimum(z, 0.0) - jnp.log1p(jnp.exp(-jnp.abs(z)))
    n = x.shape[-1]
    lane = lax.broadcasted_iota(jnp.int32, x.shape, 1)
    d = 1
    while d < n:
        x = x + jnp.where(lane >= d, pltpu.roll(x, d, 1), 0.0)
        d *= 2
    c_ref[0] = x


def _cumlogf(fl_t, b_col):
    B, H, S = fl_t.shape
    return pl.pallas_call(
        _cumlogf_kernel, grid=(B,),
        in_specs=[pl.BlockSpec((1, H, S), lambda b: (b, 0, 0)), pl.BlockSpec((H, 1), lambda b: (0, 0))],
        out_specs=pl.BlockSpec((1, H, S), lambda b: (b, 0, 0)),
        out_shape=jax.ShapeDtypeStruct((B, H, S), F32),
        compiler_params=_cparams(("parallel",)),
        name="cumlogf",
    )(fl_t, b_col)


def _split3(c):
    hi = c.astype(BF16)
    r = c - hi.astype(F32)
    mid = r.astype(BF16)
    lo = (r - mid.astype(F32)).astype(BF16)
    return hi.astype(F32), mid.astype(F32), lo.astype(F32)


def _fox_prep_kernel(q_ref, k_ref, c_ref, qo_ref, ko_ref):
    tm = q_ref.shape[0]
    lane = lax.broadcasted_iota(jnp.int32, (tm, LANES), 1)
    c = c_ref[...] * LOG2E
    for hp in range(HEADS // 2):
        q2 = q_ref[:, hp * LANES:(hp + 1) * LANES].astype(F32)
        k2 = k_ref[:, hp * LANES:(hp + 1) * LANES].astype(F32)
        for hh in range(2):
            h = 2 * hp + hh
            hi, mid, lo = _split3(c[:, h:h + 1])
            aug_c = jnp.where(lane == HEAD_DIM, hi, jnp.where(lane == HEAD_DIM + 1, mid,
                              jnp.where(lane == HEAD_DIM + 2, lo, 0.0)))
            ones_a = jnp.where((lane >= HEAD_DIM) & (lane < HEAD_DIM + 3), 1.0, 0.0)
            qh = q2 if hh == 0 else pltpu.roll(q2, HEAD_DIM, 1)
            kh = k2 if hh == 0 else pltpu.roll(k2, HEAD_DIM, 1)
            q_aug = jnp.where(lane < HEAD_DIM, qh, ones_a + pltpu.roll(aug_c, 3, 1))
            k_aug = jnp.where(lane < HEAD_DIM, kh, pltpu.roll(ones_a, 3, 1) - aug_c)
            qo_ref[:, h * LANES:(h + 1) * LANES] = q_aug.astype(BF16)
            ko_ref[:, h * LANES:(h + 1) * LANES] = k_aug.astype(BF16)


def _fox_prep(big, c_rows, tm=512):
    T = big.shape[0]
    qk = HEADS * LANES
    return pl.pallas_call(
        _fox_prep_kernel, grid=(T // tm,),
        in_specs=[pl.BlockSpec((tm, BRANCH_WIDTH), lambda i: (i, COL_FQ // BRANCH_WIDTH)),
                  pl.BlockSpec((tm, BRANCH_WIDTH), lambda i: (i, COL_FK // BRANCH_WIDTH)),
                  pl.BlockSpec((tm, HEADS), lambda i: (i, 0))],
        out_specs=[pl.BlockSpec((tm, qk), lambda i: (i, 0)), pl.BlockSpec((tm, qk), lambda i: (i, 0))],
        out_shape=[jax.ShapeDtypeStruct((T, qk), BF16), jax.ShapeDtypeStruct((T, qk), BF16)],
        compiler_params=_cparams(("parallel",)),
        name="fox_prep",
    )(big, big, c_rows)


def _nt_dot(a, b):
    return lax.dot_general(a, b, (((1,), (1,)), ((), ())), preferred_element_type=F32)


def _softmax_step(st, m, acc, vt):
    m_new = jnp.maximum(m, jnp.max(st, axis=0, keepdims=True))
    upd = jnp.dot(vt, jnp.exp2(st - m_new).astype(BF16), preferred_element_type=F32)
    return m_new, jnp.exp2(m - m_new) * acc + upd


def _softmax_init(tq):
    return jnp.full((1, tq), M_INIT, F32), jnp.zeros((V_ROWS, tq), F32)


def _softmax_finish(acc):
    return (acc[:HEAD_DIM] / acc[HEAD_DIM:HEAD_DIM + 1]).T


def _attend(npairs, qk, val, past, diag, tq, sa, sb):
    heads = range(len(sa))

    def put(dst, j):
        for hh in heads:
            dst[hh][...] = qk(hh, j)

    def advance(carry, src, j, fn):
        return tuple(_softmax_step(fn(hh, j, src[hh][...]), *carry[hh], val(hh, j)) for hh in heads)

    put(sa, 0)

    def body(jj, carry):
        j0 = 2 * jj
        put(sb, j0 + 1)
        carry = advance(carry, sa, j0, past)
        put(sa, j0 + 2)
        return advance(carry, sb, j0 + 1, past)

    carry = lax.fori_loop(0, npairs, body, tuple(_softmax_init(tq) for _ in heads))
    j0 = 2 * npairs
    put(sb, j0 + 1)
    carry = advance(carry, sa, j0, lambda hh, j, st: diag(hh, 0, st))
    carry = advance(carry, sb, j0 + 1, lambda hh, j, st: diag(hh, 1, st))
    return jnp.concatenate([_softmax_finish(acc) for _, acc in carry], axis=1)


def _score_scratch(tq, tk, nh):
    return [pltpu.VMEM((tk, tq), F32) for _ in range(2 * nh)]


def _flash_kernel(q_ref, k_ref, vt_ref, o_ref, *scores, tq, tk, nh):
    i = pl.program_id(2)
    assert tq == 2 * tk
    krow = lax.broadcasted_iota(jnp.int32, (tk, tq), 0)
    qcol = lax.broadcasted_iota(jnp.int32, (tk, tq), 1)

    def qk(hh, j):
        return _nt_dot(k_ref[0, j, :, hh * LANES:(hh + 1) * LANES], q_ref[0, :, hh * LANES:(hh + 1) * LANES])

    def val(hh, j):
        return vt_ref[0, j, hh * V_ROWS:(hh + 1) * V_ROWS, :]

    def diag(hh, d, st):
        return jnp.where(d * tk + krow <= qcol, st, NEG_INF)

    out = _attend(i, qk, val, lambda hh, j, st: st, diag, tq, scores[:nh], scores[nh:])
    o_ref[0] = out.astype(BF16)


def _flash(q, k, vt, tq, tk, nh=FLASH_HEADS):
    B, S, _ = q.shape
    nk = S // tk
    return pl.pallas_call(
        functools.partial(_flash_kernel, tq=tq, tk=tk, nh=nh), grid=(B, HEADS // nh, S // tq),
        in_specs=[pl.BlockSpec((1, tq, nh * LANES), lambda b, h, i: (b, i, h)),
                  pl.BlockSpec((1, nk, tk, nh * LANES), lambda b, h, i: (b, 0, 0, h)),
                  pl.BlockSpec((1, nk, nh * V_ROWS, tk), lambda b, h, i: (b, 0, h, 0))],
        out_specs=pl.BlockSpec((1, tq, nh * HEAD_DIM), lambda b, h, i: (b, i, h)),
        out_shape=jax.ShapeDtypeStruct((B, S, BRANCH_WIDTH), BF16),
        scratch_shapes=_score_scratch(tq, tk, nh),
        compiler_params=_cparams(("parallel", "parallel", "arbitrary")),
        name="flash",
    )(q, k, vt)


def _moba_kernel(q_ref, k_ref, vt_ref, o_ref, kmean_ref, bias_ref, qm_ref, *scores, nblk, tq, nh):
    i = pl.program_id(2)
    blk = MOBA_BLOCK
    shift = blk.bit_length() - 1
    r = tq // blk

    @pl.when(i == 0)
    def _():
        for n in range(nblk):
            kmean_ref[n:n + 1, :] = jnp.mean(k_ref[0, n].astype(F32), axis=0, keepdims=True)

    lane = lax.broadcasted_iota(jnp.int32, (tq, nh * HEAD_DIM), 1)
    blk_id = lax.broadcasted_iota(jnp.int32, (nblk, tq), 0)
    own = i * r + (lax.broadcasted_iota(jnp.int32, (nblk, tq), 1) >> shift)
    q2 = q_ref[0]
    for hh in range(nh):
        in_head = (lane >= hh * HEAD_DIM) & (lane < (hh + 1) * HEAD_DIM)
        q = jnp.where(in_head, q2, jnp.zeros_like(q2))
        qm_ref[hh] = q
        g3 = _nt_dot(jnp.concatenate([t.astype(BF16) for t in _split3(kmean_ref[...])], axis=0), q)
        g = g3[:nblk] + g3[nblk:2 * nblk] + g3[2 * nblk:]
        g = jnp.where(blk_id < own, g, NEG_INF)
        bias = jnp.full((nblk, tq), NEG_INF, F32)
        for _ in range(MOBA_TOPK):
            mx = jnp.max(g, axis=0, keepdims=True)
            first = jnp.min(jnp.where(g == mx, blk_id, nblk), axis=0, keepdims=True)
            pick = (blk_id == first) & (mx > NEG_INF)
            bias = jnp.where(pick, 0.0, bias)
            g = jnp.where(pick, NEG_INF, g)
        bias_ref[hh] = bias

    krow = lax.broadcasted_iota(jnp.int32, (blk, tq), 0)
    qcol = lax.broadcasted_iota(jnp.int32, (blk, tq), 1)

    def qk(hh, n):
        return _nt_dot(k_ref[0, n], qm_ref[hh])

    def val(hh, n):
        return vt_ref[0, n, hh * V_ROWS:(hh + 1) * V_ROWS, :]

    def past(hh, n, st):
        return st + bias_ref[hh, pl.ds(n, 1), :]

    def diag(hh, d, st):
        own_causal = ((qcol >> shift) == d) & (krow <= (qcol & (blk - 1)))
        return jnp.where(own_causal, st, past(hh, i * r + d, st))

    out = _attend(i, qk, val, past, diag, tq, scores[:nh], scores[nh:])
    o_ref[0] = out.astype(BF16)


def _moba(q, k, vt, tq, nh=MOBA_STEP_HEADS):
    B, S, _ = q.shape
    nblk = S // MOBA_BLOCK
    assert tq == 2 * MOBA_BLOCK and MOBA_BLOCK & (MOBA_BLOCK - 1) == 0
    w = nh * HEAD_DIM
    qc = COL_MQ // w
    kc = COL_MK // w
    return pl.pallas_call(
        functools.partial(_moba_kernel, nblk=nblk, tq=tq, nh=nh), grid=(B, HEADS // nh, S // tq),
        in_specs=[pl.BlockSpec((1, tq, w), lambda b, h, i: (b, i, qc + h)),
                  pl.BlockSpec((1, nblk, MOBA_BLOCK, w), lambda b, h, i: (b, 0, 0, kc + h)),
                  pl.BlockSpec((1, nblk, nh * V_ROWS, MOBA_BLOCK), lambda b, h, i: (b, 0, h, 0))],
        out_specs=pl.BlockSpec((1, tq, w), lambda b, h, i: (b, i, h)),
        out_shape=jax.ShapeDtypeStruct((B, S, BRANCH_WIDTH), BF16),
        scratch_shapes=[pltpu.VMEM((nblk, w), F32), pltpu.VMEM((nh, nblk, tq), F32),
                        pltpu.VMEM((nh, tq, w), BF16)] + _score_scratch(tq, MOBA_BLOCK, nh),
        compiler_params=_cparams(("parallel", "parallel", "arbitrary")),
        name="moba",
    )(q, k, vt)


def _merge_kernel(x_ref, ya_ref, yb_ref, yc_ref, ga_ref, gb_ref, gc_ref, wb_ref, wo_ref, o_ref):
    merged = None
    for n, (y_ref, g_ref) in enumerate(((ya_ref, ga_ref), (yb_ref, gb_ref), (yc_ref, gc_ref))):
        proj = jnp.dot(y_ref[...], wb_ref[n], preferred_element_type=F32)
        term = jax.nn.sigmoid(g_ref[...].astype(F32)) * proj
        merged = term if merged is None else merged + term
    o_ref[...] = x_ref[...] + jnp.dot(merged.astype(BF16), wo_ref[...], preferred_element_type=F32)


def _merge(x, y_mla, y_fox, y_moba, big, w_branch, w_out, tm=256):
    T = x.shape[0]
    g0 = COL_GATES // D_MODEL
    row = lambda n: pl.BlockSpec((tm, n), lambda i: (i, 0))
    gate = lambda n: pl.BlockSpec((tm, D_MODEL), lambda i: (i, g0 + n))
    return pl.pallas_call(
        _merge_kernel, grid=(T // tm,),
        in_specs=[row(D_MODEL), row(BRANCH_WIDTH), row(BRANCH_WIDTH), row(BRANCH_WIDTH),
                  gate(0), gate(1), gate(2),
                  pl.BlockSpec((3, BRANCH_WIDTH, D_MODEL), lambda i: (0, 0, 0)),
                  pl.BlockSpec((D_MODEL, D_MODEL), lambda i: (0, 0))],
        out_specs=row(D_MODEL),
        out_shape=jax.ShapeDtypeStruct((T, D_MODEL), F32),
        compiler_params=_cparams(("parallel",)),
        name="merge",
    )(x, y_mla, y_fox, y_moba, big, big, big, w_branch, w_out)


def _dense_ffn_kernel(x_ref, g_ref, wg_ref, wu_ref, wd_ref, o_ref):
    x = x_ref[...]
    h = _rms(x, g_ref[...]).astype(BF16)
    gate = jnp.dot(h, wg_ref[...], preferred_element_type=F32)
    up = jnp.dot(h, wu_ref[...], preferred_element_type=F32)
    act = (jax.nn.silu(gate) * up).astype(BF16)
    o_ref[...] = x + jnp.dot(act, wd_ref[...], preferred_element_type=F32)


def _resident(shape, index_map):
    return pl.BlockSpec(shape, index_map, pipeline_mode=pl.Buffered(1))


def _dense_ffn(x, g, w_gu, w_down, tm=512):
    T = x.shape[0]
    return pl.pallas_call(
        _dense_ffn_kernel, grid=(T // tm,),
        in_specs=[pl.BlockSpec((tm, D_MODEL), lambda i: (i, 0)),
                  _resident((1, D_MODEL), lambda i: (0, 0)),
                  _resident((D_MODEL, DENSE_FF), lambda i: (0, 0)),
                  _resident((D_MODEL, DENSE_FF), lambda i: (0, 1)),
                  _resident((DENSE_FF, D_MODEL), lambda i: (0, 0))],
        out_specs=pl.BlockSpec((tm, D_MODEL), lambda i: (i, 0)),
        out_shape=jax.ShapeDtypeStruct((T, D_MODEL), F32),
        compiler_params=_cparams(("parallel",)),
        name="dense_ffn",
    )(x, g, w_gu, w_gu, w_down)


def _router_kernel(x_ref, g_ref, wr_ref, h_ref, r_ref):
    h = _rms(x_ref[...], g_ref[...])
    h_ref[...] = h
    logits = jnp.dot(h, wr_ref[...], precision=lax.Precision.HIGHEST, preferred_element_type=F32)
    lane = lax.broadcasted_iota(jnp.int32, logits.shape, 1)
    logits = jnp.where(lane < N_EXPERTS, logits, NEG_INF)
    m1 = jnp.max(logits, axis=-1, keepdims=True)
    i1 = jnp.min(jnp.where(logits == m1, lane, LANES), axis=-1, keepdims=True)
    rest = jnp.where(lane == i1, NEG_INF, logits)
    m2 = jnp.max(rest, axis=-1, keepdims=True)
    i2 = jnp.min(jnp.where(rest == m2, lane, LANES), axis=-1, keepdims=True)
    e2 = jnp.exp(m2 - m1)
    w1 = 1.0 / (1.0 + e2)
    w2 = e2 / (1.0 + e2)
    r_ref[...] = jnp.where(lane == 0, i1.astype(F32), jnp.where(lane == 1, i2.astype(F32),
                           jnp.where(lane == 2, w1, jnp.where(lane == 3, w2, 0.0))))


def _router(x, g, w_router_pad, tm=512):
    T = x.shape[0]
    return pl.pallas_call(
        _router_kernel, grid=(T // tm,),
        in_specs=[pl.BlockSpec((tm, D_MODEL), lambda i: (i, 0)),
                  pl.BlockSpec((1, D_MODEL), lambda i: (0, 0)),
                  pl.BlockSpec((D_MODEL, LANES), lambda i: (0, 0))],
        out_specs=[pl.BlockSpec((tm, D_MODEL), lambda i: (i, 0)), pl.BlockSpec((tm, LANES), lambda i: (i, 0))],
        out_shape=[jax.ShapeDtypeStruct((T, D_MODEL), F32), jax.ShapeDtypeStruct((T, LANES), F32)],
        compiler_params=_cparams(("parallel",)),
        name="router",
    )(x, g, w_router_pad)


GATHER_WINDOW = 128
GATHER_ROWS = 32


def _gather_rows(src, idx):
    M = idx.shape[0]
    C = src.shape[1]
    mesh = plsc.VectorSubcoreMesh(core_axis_name="core", subcore_axis_name="subcore")
    per = M // (mesh.num_cores * mesh.num_subcores)
    assert per * mesh.num_cores * mesh.num_subcores == M and per % GATHER_WINDOW == 0

    @pl.kernel(out_type=jax.ShapeDtypeStruct((M, C), src.dtype), mesh=mesh, name="gather_rows",
               scratch_types=[pltpu.VMEM((GATHER_WINDOW,), jnp.int32), pltpu.VMEM((GATHER_ROWS, C), src.dtype)])
    def gather(x_hbm, i_hbm, o_hbm, idx_v, buf):
        w = lax.axis_index("core") * mesh.num_subcores + lax.axis_index("subcore")

        @pl.loop(0, per // GATHER_WINDOW)
        def _(t):
            base = w * per + t * GATHER_WINDOW
            pltpu.sync_copy(i_hbm.at[pl.ds(base, GATHER_WINDOW)], idx_v)
            for k in range(GATHER_WINDOW // GATHER_ROWS):
                pltpu.sync_copy(x_hbm.at[idx_v.at[pl.ds(k * GATHER_ROWS, GATHER_ROWS)]], buf)
                pltpu.sync_copy(buf, o_hbm.at[pl.ds(base + k * GATHER_ROWS, GATHER_ROWS)])

    return gather(src, idx)


def _scatter_rows(src, dest0, dest1, n_rows):
    T, C = src.shape
    mesh = plsc.VectorSubcoreMesh(core_axis_name="core", subcore_axis_name="subcore")
    per = T // (mesh.num_cores * mesh.num_subcores)
    assert per * mesh.num_cores * mesh.num_subcores == T and per % GATHER_WINDOW == 0

    @pl.kernel(out_type=jax.ShapeDtypeStruct((n_rows, C), src.dtype), mesh=mesh, name="scatter_rows",
               scratch_types=[pltpu.VMEM((GATHER_WINDOW,), jnp.int32), pltpu.VMEM((GATHER_WINDOW,), jnp.int32),
                              pltpu.VMEM((GATHER_ROWS, C), src.dtype)])
    def scatter(x_hbm, d0_hbm, d1_hbm, o_hbm, i0, i1, buf):
        w = lax.axis_index("core") * mesh.num_subcores + lax.axis_index("subcore")

        @pl.loop(0, per // GATHER_WINDOW)
        def _(t):
            base = w * per + t * GATHER_WINDOW
            pltpu.sync_copy(d0_hbm.at[pl.ds(base, GATHER_WINDOW)], i0)
            pltpu.sync_copy(d1_hbm.at[pl.ds(base, GATHER_WINDOW)], i1)
            for k in range(GATHER_WINDOW // GATHER_ROWS):
                pltpu.sync_copy(x_hbm.at[pl.ds(base + k * GATHER_ROWS, GATHER_ROWS)], buf)
                pltpu.sync_copy(buf, o_hbm.at[i0.at[pl.ds(k * GATHER_ROWS, GATHER_ROWS)]])
                pltpu.sync_copy(buf, o_hbm.at[i1.at[pl.ds(k * GATHER_ROWS, GATHER_ROWS)]])

    return scatter(src, dest0, dest1)


def _expert_kernel(be_ref, nused_ref, valid_ref, x_ref, wg_ref, wu_ref, wd_ref, o_ref, xb_ref, acc_ref):
    b = pl.program_id(0)
    f = pl.program_id(1)
    last = pl.num_programs(1) - 1
    used = b < nused_ref[0]

    @pl.when(used)
    def _():
        @pl.when(f == 0)
        def _():
            row = lax.broadcasted_iota(jnp.int32, (x_ref.shape[0], 1), 0)
            xb_ref[...] = jnp.where(row < valid_ref[b], x_ref[...], 0.0).astype(BF16)

        x = xb_ref[...]
        gate = jnp.dot(x, wg_ref[0], preferred_element_type=F32)
        up = jnp.dot(x, wu_ref[0], preferred_element_type=F32)
        act = (jax.nn.silu(gate) * up).astype(BF16)
        y = jnp.dot(act, wd_ref[0], preferred_element_type=F32)

        @pl.when(f == 0)
        def _():
            acc_ref[...] = y

        @pl.when(jnp.logical_and(f > 0, f < last))
        def _():
            acc_ref[...] += y

        @pl.when(f == last)
        def _():
            o_ref[...] = acc_ref[...] + y

    @pl.when(jnp.logical_and(f == last, jnp.logical_not(used)))
    def _():
        o_ref[...] = jnp.zeros_like(o_ref)


def _experts(x_rows, block_e, n_used, valid, w_gu, w_down, bm, tf=1792):
    n_rows = x_rows.shape[0]
    nf = EXPERT_FF // tf
    nb = n_rows // bm

    def fsel(b, f, nused):
        return jnp.where(b < nused[0], f, nf - 1)

    grid_spec = pltpu.PrefetchScalarGridSpec(
        num_scalar_prefetch=3, grid=(nb, nf),
        in_specs=[pl.BlockSpec((bm, D_MODEL), lambda b, f, be, nu, va: (b, 0)),
                  pl.BlockSpec((1, D_MODEL, tf), lambda b, f, be, nu, va: (be[b], 0, fsel(b, f, nu))),
                  pl.BlockSpec((1, D_MODEL, tf), lambda b, f, be, nu, va: (be[b], 0, nf + fsel(b, f, nu))),
                  pl.BlockSpec((1, tf, D_MODEL), lambda b, f, be, nu, va: (be[b], fsel(b, f, nu), 0))],
        out_specs=pl.BlockSpec((bm, D_MODEL), lambda b, f, be, nu, va: (b, 0)),
        scratch_shapes=[pltpu.VMEM((bm, D_MODEL), BF16), pltpu.VMEM((bm, D_MODEL), F32)])
    assert nf >= 2
    return pl.pallas_call(
        _expert_kernel, grid_spec=grid_spec,
        out_shape=jax.ShapeDtypeStruct((n_rows, D_MODEL), F32),
        compiler_params=_cparams(("arbitrary", "arbitrary"), EXPERT_VMEM_LIMIT),
        name="experts",
    )(block_e, n_used, valid, x_rows, w_gu, w_gu, w_down)


def _combine_kernel(x_ref, y0_ref, y1_ref, r_ref, g_ref, o_ref):
    r = r_ref[...]
    x = x_ref[...] + r[:, 2:3] * y0_ref[...] + r[:, 3:4] * y1_ref[...]
    o_ref[...] = _rms(x, g_ref[...])


def _combine(x, y_slots, route, g_final, tm=512):
    T = x.shape[0]
    return pl.pallas_call(
        _combine_kernel, grid=(T // tm,),
        in_specs=[pl.BlockSpec((tm, D_MODEL), lambda i: (i, 0)),
                  pl.BlockSpec((tm, D_MODEL), lambda i: (i, 0)),
                  pl.BlockSpec((tm, D_MODEL), lambda i: (T // tm + i, 0)),
                  pl.BlockSpec((tm, LANES), lambda i: (i, 0)),
                  pl.BlockSpec((1, D_MODEL), lambda i: (0, 0))],
        out_specs=pl.BlockSpec((tm, D_MODEL), lambda i: (i, 0)),
        out_shape=jax.ShapeDtypeStruct((T, D_MODEL), F32),
        compiler_params=_cparams(("parallel",)),
        name="combine",
    )(x, y_slots, y_slots, route, g_final)


def _pad_cols(w, n):
    return jnp.pad(w, ((0, 0), (0, n - w.shape[1])))


def _split_in_weights(w_in):
    sizes = [MLA_Q_LORA, MLA_KV_LORA, MLA_ROPE_DIM, BRANCH_WIDTH, BRANCH_WIDTH, BRANCH_WIDTH, HEADS,
             BRANCH_WIDTH, BRANCH_WIDTH, BRANCH_WIDTH, 3 * D_MODEL]
    pts = np.cumsum(sizes)[:-1]
    c_q, c_kv, k_pe, fq, fk, fv, fl, mq, mk, mv, gates = jnp.split(w_in, pts, axis=1)
    zeros = lambda n: jnp.zeros((D_MODEL, n), w_in.dtype)
    pe_tile = jnp.concatenate([zeros(HEAD_DIM), k_pe, zeros(LANES - HEAD_DIM - MLA_ROPE_DIM)], axis=1)
    w_small = jnp.concatenate([c_q, c_kv, pe_tile, _pad_cols(fl, LANES)], axis=1)
    w_big = jnp.concatenate([fq, mq, fk, mk, fv, mv, gates], axis=1)
    return w_small.astype(BF16), w_big.astype(BF16)


def _mla_up_weights(w_uq, w_ukv):
    dq = HEAD_DIM + MLA_ROPE_DIM
    wq = w_uq.reshape(MLA_Q_LORA, HEADS, dq)
    wq = jnp.pad(wq, ((0, 0), (0, 0), (0, LANES - dq))).reshape(MLA_Q_LORA, HEADS * LANES)
    wkv = w_ukv.reshape(MLA_KV_LORA, HEADS, 2 * HEAD_DIM)
    wk = jnp.pad(wkv[:, :, :HEAD_DIM], ((0, 0), (0, 0), (0, LANES - HEAD_DIM))).reshape(MLA_KV_LORA, HEADS * LANES)
    wv = wkv[:, :, HEAD_DIM:].reshape(MLA_KV_LORA, BRANCH_WIDTH)
    return wq.astype(BF16), wk.astype(BF16), wv.astype(BF16)


def _chunk_rows(a, B, S, t):
    return a.reshape(B, S // t, t, a.shape[-1])


ATT_TQ = 512
ATT_TK = KV_CHUNK
MOBA_TQ = 512
MOE_BM = 512


def _token_mixers(x, B, S, g_mix, w_in, g_q_lat, g_kv_lat, w_uq, w_ukv, b_forget, w_branch, w_out,
                  tabs_mla, tabs_moba):
    T = B * S
    w_small, w_big = _split_in_weights(w_in)
    wq, wk, wv = _mla_up_weights(w_uq, w_ukv)
    g = g_mix[None, :]

    big, vt_fox, vt_moba = _inproj(x, g, w_big, tabs_moba, B, S)
    q_mla, k_mla, vt_mla, f_logit = _mla_prep(
        x, g, w_small, g_q_lat[None, :], g_kv_lat[None, :], wq, wk, wv, tabs_mla,
        (HEAD_DIM + MLA_ROPE_DIM) ** -0.5 * LOG2E, B, S)

    y_mla = _flash(q_mla.reshape(B, S, -1), _chunk_rows(k_mla, B, S, ATT_TK), vt_mla, ATT_TQ, ATT_TK)

    c = _cumlogf(f_logit.reshape(B, S, HEADS).transpose(0, 2, 1), b_forget[:, None])
    c_rows = c.transpose(0, 2, 1).reshape(T, HEADS)
    q_fox, k_fox = _fox_prep(big, c_rows)
    y_fox = _flash(q_fox.reshape(B, S, -1), _chunk_rows(k_fox, B, S, ATT_TK), vt_fox, ATT_TQ, ATT_TK)

    y_moba = _moba(big.reshape(B, S, -1), _chunk_rows(big, B, S, MOBA_BLOCK), vt_moba, MOBA_TQ)

    return _merge(x, y_mla.reshape(T, -1), y_fox.reshape(T, -1), y_moba.reshape(T, -1), big,
                  w_branch.astype(BF16), w_out.astype(BF16))


def _moe(x, g_ffn, w_router, w_gu, w_down, g_final):
    T = x.shape[0]
    A = 2 * T
    bm = MOE_BM
    h, route = _router(x, g_ffn[None, :], _pad_cols(w_router, LANES))
    top_e = route[:, :2].astype(jnp.int32).reshape(A)
    onehot = (top_e[:, None] == jnp.arange(N_EXPERTS)[None, :]).astype(jnp.int32)
    csum = jnp.cumsum(onehot, axis=0)
    counts = csum[-1]
    rank = jnp.sum((csum - onehot) * onehot, axis=1)
    padded = (counts + bm - 1) // bm * bm
    pad_end = jnp.cumsum(padded)
    dest = (pad_end - padded)[top_e] + rank
    n_rows = (A // bm + N_EXPERTS) * bm
    block_start = jnp.arange(n_rows // bm) * bm
    block_e = jnp.minimum(jnp.searchsorted(pad_end, block_start, side="right"), N_EXPERTS - 1).astype(jnp.int32)
    n_used = (pad_end[-1:] // bm).astype(jnp.int32)
    valid = jnp.clip((pad_end - padded + counts)[block_e] - block_start, 0, bm).astype(jnp.int32)
    dest_slots = dest.astype(jnp.int32).reshape(T, 2).T

    x_rows = _scatter_rows(h, dest_slots[0], dest_slots[1], n_rows)
    y_rows = _experts(x_rows, block_e, n_used, valid, w_gu.astype(BF16), w_down.astype(BF16), bm)
    y_slots = _gather_rows(y_rows, dest_slots.reshape(A))
    return _combine(x, y_slots, route, g_final[None, :])


def kernel(x, positions, g_mix, w_in, g_q_lat, g_kv_lat, w_uq, w_ukv, b_forget, w_branch, w_out, g_ffn,
           w_dense_gu, w_dense_down, w_router, w_exp_gu, w_exp_down, g_final):
    B, S, D = x.shape
    T = B * S
    depth = g_mix.shape[0]
    assert depth == 2 and D == D_MODEL and S % MOBA_BLOCK == 0 and KV_CHUNK == MOBA_BLOCK
    pos = positions.reshape(T, 1).astype(jnp.int32)
    tabs_mla = _rope_tables(pos, *_rope_patterns(LANES, HEAD_DIM, MLA_ROPE_DIM // 2))
    tabs_moba = _rope_tables(pos, *_rope_patterns(HEAD_DIM, 0, HEAD_DIM // 2))
    x = x.reshape(T, D)
    for l in range(depth):
        x = _token_mixers(x, B, S, g_mix[l], w_in[l], g_q_lat[l], g_kv_lat[l], w_uq[l], w_ukv[l],
                          b_forget[l], w_branch[l], w_out[l], tabs_mla, tabs_moba)
        if l % 2 == 0:
            x = _dense_ffn(x, g_ffn[l][None, :], w_dense_gu[l // 2].astype(BF16),
                           w_dense_down[l // 2].astype(BF16))
        else:
            x = _moe(x, g_ffn[l], w_router[l // 2], w_exp_gu[l // 2], w_exp_down[l // 2], g_final)
    return x.reshape(B, S, D)
```

```python
import functools
import math

import jax
import jax.numpy as jnp
import numpy as np
from jax import lax
from jax.experimental import pallas as pl
from jax.experimental.pallas import tpu as pltpu
from jax.experimental.pallas import tpu_sc as plsc

F32 = jnp.float32
BF16 = jnp.bfloat16
NEG_INF = float("-inf")
M_INIT = -1e30
LOG2E = math.log2(math.e)

D_MODEL = 1024
RMS_EPS = 1e-6
ROPE_THETA = 10000.0
HEADS = 8
HEAD_DIM = 64
V_ROWS = 80
Q_SCALE = HEAD_DIM ** -0.5 * math.log2(math.e)
MLA_Q_LORA = 256
MLA_KV_LORA = 128
MLA_ROPE_DIM = 32
BRANCH_WIDTH = HEADS * HEAD_DIM
MOBA_BLOCK = 256
MOBA_TOPK = 3
DENSE_FF = 2816
N_EXPERTS = 8
EXPERT_FF = 3584

LANES = 128
VMEM_LIMIT = 48 * 1024 * 1024
EXPERT_VMEM_LIMIT = 58 * 1024 * 1024

COL_FQ, COL_MQ, COL_FK, COL_MK, COL_GATES = (n * BRANCH_WIDTH for n in range(5))
BIG_COLS = COL_GATES + 3 * D_MODEL
BIG_TN = 2 * BRANCH_WIDTH
W_TILES = BIG_COLS // BIG_TN + 1
V_TILE = 2
KV_CHUNK = 256
FLASH_HEADS = 4
MOBA_STEP_HEADS = 2
SMALL_COLS = MLA_Q_LORA + MLA_KV_LORA + 2 * LANES


def _cparams(sem, vmem_limit=VMEM_LIMIT):
    return pltpu.CompilerParams(dimension_semantics=sem, vmem_limit_bytes=vmem_limit)


def _resident(shape, index_map):
    return pl.BlockSpec(shape, index_map, pipeline_mode=pl.Buffered(1))


def _rms(x, g):
    return x * lax.rsqrt(jnp.mean(x * x, axis=-1, keepdims=True) + RMS_EPS) * g


def _rope_table_kernel(pos_ref, f_ref, mc_ref, m1_ref, m2_ref, c_ref, s1_ref, s2_ref):
    ang = pos_ref[...].astype(F32) * f_ref[...]
    cos = jnp.cos(ang)
    sin = jnp.sin(ang)
    mc = mc_ref[...]
    c_ref[...] = cos * mc + (1.0 - mc)
    s1_ref[...] = sin * m1_ref[...]
    s2_ref[...] = sin * m2_ref[...]


def _rope_tables(pos, freq, mc, m1, m2, tm=1024):
    T = pos.shape[0]
    row = pl.BlockSpec((tm, 1), lambda i: (i, 0))
    pat = pl.BlockSpec((1, LANES), lambda i: (0, 0))
    out = pl.BlockSpec((tm, LANES), lambda i: (i, 0))
    shp = jax.ShapeDtypeStruct((T, LANES), F32)
    return pl.pallas_call(
        _rope_table_kernel, grid=(T // tm,),
        in_specs=[row, pat, pat, pat, pat], out_specs=[out, out, out],
        out_shape=[shp, shp, shp], compiler_params=_cparams(("parallel",)),
        name="rope_tables",
    )(pos, freq, mc, m1, m2)


def _rope_patterns(group, x1_lo, half):
    d = 2 * half
    inv_freq = jnp.exp(-math.log(ROPE_THETA) * jnp.arange(half, dtype=F32) * 2.0 / d)
    lane = np.arange(LANES) % group
    in_x1 = (lane >= x1_lo) & (lane < x1_lo + half)
    in_x2 = (lane >= x1_lo + half) & (lane < x1_lo + d)
    k = np.where(in_x1, lane - x1_lo, np.where(in_x2, lane - x1_lo - half, 0))
    freq = jnp.where(jnp.asarray(in_x1 | in_x2), inv_freq[k], 0.0)[None, :].astype(F32)
    mc = jnp.asarray((in_x1 | in_x2).astype(np.float32))[None, :]
    m1 = jnp.asarray(-(in_x1.astype(np.float32)))[None, :]
    m2 = jnp.asarray(in_x2.astype(np.float32))[None, :]
    return freq, mc, m1, m2


def _apply_rope(x, c, s1, s2, half):
    n = x.shape[-1]
    reps = n // LANES
    c, s1, s2 = (jnp.tile(t, (1, reps)) if reps > 1 else t for t in (c, s1, s2))
    return x * c + pltpu.roll(x, n - half, 1) * s1 + pltpu.roll(x, half, 1) * s2


def _store_value_tiles(vt_ref, a):
    row = lax.broadcasted_iota(jnp.int32, (V_ROWS - HEAD_DIM, KV_CHUNK), 0)
    pad = jnp.where(row == 0, 1.0, 0.0).astype(BF16)
    for c in range(a.shape[0] // KV_CHUNK):
        at = a[c * KV_CHUNK:(c + 1) * KV_CHUNK, :].T.astype(BF16)
        for h in range(HEADS):
            vt_ref[0, c, h * V_ROWS:h * V_ROWS + HEAD_DIM, :] = at[h * HEAD_DIM:(h + 1) * HEAD_DIM]
            vt_ref[0, c, h * V_ROWS + HEAD_DIM:(h + 1) * V_ROWS, :] = pad


def _value_tile_spec(tm, S, grid_rank):
    per = S // tm
    imap = (lambda i: (i // per, i % per, 0, 0)) if grid_rank == 1 else (lambda i, j: (i // per, i % per, 0, 0))
    return pl.BlockSpec((1, tm // KV_CHUNK, HEADS * V_ROWS, KV_CHUNK), imap)


def _value_tile_shape(B, S):
    return jax.ShapeDtypeStruct((B, S // KV_CHUNK, HEADS * V_ROWS, KV_CHUNK), BF16)


def _inproj_kernel(x_ref, g_ref, w_ref, c_ref, s1_ref, s2_ref, o_ref, vf_ref, vm_ref):
    h = _rms(x_ref[...], g_ref[...]).astype(BF16)
    c, s1, s2 = c_ref[...], s1_ref[...], s2_ref[...]

    def tile(t):
        return jnp.dot(h, w_ref[:, t * BIG_TN:(t + 1) * BIG_TN], preferred_element_type=F32)

    for t in range(V_TILE):
        a = tile(t) * Q_SCALE if t == 0 else tile(t)
        o_ref[:, t * BIG_TN:t * BIG_TN + BRANCH_WIDTH] = a[:, :BRANCH_WIDTH].astype(BF16)
        o_ref[:, t * BIG_TN + BRANCH_WIDTH:(t + 1) * BIG_TN] = _apply_rope(
            a[:, BRANCH_WIDTH:], c, s1, s2, HEAD_DIM // 2).astype(BF16)
    a = tile(V_TILE)
    _store_value_tiles(vf_ref, a[:, :BRANCH_WIDTH])
    _store_value_tiles(vm_ref, a[:, BRANCH_WIDTH:])
    for t in range(V_TILE + 1, W_TILES):
        o_ref[:, (t - 1) * BIG_TN:t * BIG_TN] = tile(t).astype(BF16)


def _inproj(x, g, w_big, tabs, B, S, tm=512):
    T = x.shape[0]
    tab = pl.BlockSpec((tm, LANES), lambda i: (i, 0))
    vspec = _value_tile_spec(tm, S, 1)
    return pl.pallas_call(
        _inproj_kernel, grid=(T // tm,),
        in_specs=[pl.BlockSpec((tm, D_MODEL), lambda i: (i, 0)),
                  _resident((1, D_MODEL), lambda i: (0, 0)),
                  _resident((D_MODEL, W_TILES * BIG_TN), lambda i: (0, 0)),
                  tab, tab, tab],
        out_specs=[pl.BlockSpec((tm, BIG_COLS), lambda i: (i, 0)), vspec, vspec],
        out_shape=[jax.ShapeDtypeStruct((T, BIG_COLS), BF16), _value_tile_shape(B, S), _value_tile_shape(B, S)],
        compiler_params=_cparams(("parallel",)),
        name="inproj",
    )(x, g, w_big, *tabs)


def _mla_prep_kernel(x_ref, g_ref, ws_ref, gq_ref, gkv_ref, wq_ref, wk_ref, wv_ref,
                     c_ref, s1_ref, s2_ref, q_ref, k_ref, v_ref, fl_ref, *, scale):
    h = _rms(x_ref[...], g_ref[...]).astype(BF16)
    small = jnp.dot(h, ws_ref[...], preferred_element_type=F32)
    c_q = small[:, :MLA_Q_LORA]
    c_kv = small[:, MLA_Q_LORA:MLA_Q_LORA + MLA_KV_LORA]
    k_pe = small[:, MLA_Q_LORA + MLA_KV_LORA:MLA_Q_LORA + MLA_KV_LORA + LANES]
    fl_ref[...] = small[:, SMALL_COLS - LANES:SMALL_COLS - LANES + HEADS]
    c, s1, s2 = c_ref[...], s1_ref[...], s2_ref[...]
    half = MLA_ROPE_DIM // 2
    qn = _rms(c_q, gq_ref[...]).astype(BF16)
    q = jnp.dot(qn, wq_ref[...], preferred_element_type=F32) * scale
    q_ref[...] = _apply_rope(q, c, s1, s2, half).astype(BF16)
    kvn = _rms(c_kv, gkv_ref[...]).astype(BF16)
    k_nope = jnp.dot(kvn, wk_ref[...], preferred_element_type=F32)
    k_rot = _apply_rope(k_pe, c, s1, s2, half)
    k_ref[...] = (k_nope + jnp.tile(k_rot, (1, HEADS))).astype(BF16)
    _store_value_tiles(v_ref, jnp.dot(kvn, wv_ref[...], preferred_element_type=F32))


def _mla_prep(x, g, w_small, g_q, g_kv, wq, wk, wv, tabs, scale, B, S, tm=512):
    T = x.shape[0]
    full = lambda shape: pl.BlockSpec(shape, lambda i: (0,) * len(shape))
    row = lambda n: pl.BlockSpec((tm, n), lambda i: (i, 0))
    qk = HEADS * LANES
    return pl.pallas_call(
        functools.partial(_mla_prep_kernel, scale=scale), grid=(T // tm,),
        in_specs=[row(D_MODEL), full((1, D_MODEL)), full((D_MODEL, SMALL_COLS)),
                  full((1, MLA_Q_LORA)), full((1, MLA_KV_LORA)),
                  full((MLA_Q_LORA, qk)), full((MLA_KV_LORA, qk)), full((MLA_KV_LORA, BRANCH_WIDTH)),
                  row(LANES), row(LANES), row(LANES)],
        out_specs=[row(qk), row(qk), _value_tile_spec(tm, S, 1), row(HEADS)],
        out_shape=[jax.ShapeDtypeStruct((T, qk), BF16), jax.ShapeDtypeStruct((T, qk), BF16),
                   _value_tile_shape(B, S), jax.ShapeDtypeStruct((T, HEADS), F32)],
        compiler_params=_cparams(("parallel",)),
        name="mla_prep",
    )(x, g, w_small, g_q, g_kv, wq, wk, wv, *tabs)


def _cumlogf_kernel(fl_ref, b_ref, c_ref):
    z = fl_ref[0] + b_ref[...]
    x = jnp.minimum(z, 0.0) - jnp.log1p(jnp.exp(-jnp.abs(z)))
    n = x.shape[-1]
    lane = lax.broadcasted_iota(jnp.int32, x.shape, 1)
    d = 1
    while d < n:
        x = x + jnp.where(lane >= d, pltpu.roll(x, d, 1), 0.0)
        d *= 2
    c_ref[0] = x


def _cumlogf(fl_t, b_col):
    B, H, S = fl_t.shape
    return pl.pallas_call(
        _cumlogf_kernel, grid=(B,),
        in_specs=[pl.BlockSpec((1, H, S), lambda b: (b, 0, 0)), pl.BlockSpec((H, 1), lambda b: (0, 0))],
        out_specs=pl.BlockSpec((1, H, S), lambda b: (b, 0, 0)),
        out_shape=jax.ShapeDtypeStruct((B, H, S), F32),
        compiler_params=_cparams(("parallel",)),
        name="cumlogf",
    )(fl_t, b_col)


def _split3(c):
    hi = c.astype(BF16)
    r = c - hi.astype(F32)
    mid = r.astype(BF16)
    lo = (r - mid.astype(F32)).astype(BF16)
    return hi.astype(F32), mid.astype(F32), lo.astype(F32)


def _fox_prep_kernel(q_ref, k_ref, c_ref, qo_ref, ko_ref):
    tm = q_ref.shape[0]
    lane = lax.broadcasted_iota(jnp.int32, (tm, LANES), 1)
    c = c_ref[...] * LOG2E
    for hp in range(HEADS // 2):
        q2 = q_ref[:, hp * LANES:(hp + 1) * LANES].astype(F32)
        k2 = k_ref[:, hp * LANES:(hp + 1) * LANES].astype(F32)
        for hh in range(2):
            h = 2 * hp + hh
            hi, mid, lo = _split3(c[:, h:h + 1])
            aug_c = jnp.where(lane == HEAD_DIM, hi, jnp.where(lane == HEAD_DIM + 1, mid,
                              jnp.where(lane == HEAD_DIM + 2, lo, 0.0)))
            ones_a = jnp.where((lane >= HEAD_DIM) & (lane < HEAD_DIM + 3), 1.0, 0.0)
            qh = q2 if hh == 0 else pltpu.roll(q2, HEAD_DIM, 1)
            kh = k2 if hh == 0 else pltpu.roll(k2, HEAD_DIM, 1)
            q_aug = jnp.where(lane < HEAD_DIM, qh, ones_a + pltpu.roll(aug_c, 3, 1))
            k_aug = jnp.where(lane < HEAD_DIM, kh, pltpu.roll(ones_a, 3, 1) - aug_c)
            qo_ref[:, h * LANES:(h + 1) * LANES] = q_aug.astype(BF16)
            ko_ref[:, h * LANES:(h + 1) * LANES] = k_aug.astype(BF16)


def _fox_prep(big, c_rows, tm=512):
    T = big.shape[0]
    qk = HEADS * LANES
    return pl.pallas_call(
        _fox_prep_kernel, grid=(T // tm,),
        in_specs=[pl.BlockSpec((tm, BRANCH_WIDTH), lambda i: (i, COL_FQ // BRANCH_WIDTH)),
                  pl.BlockSpec((tm, BRANCH_WIDTH), lambda i: (i, COL_FK // BRANCH_WIDTH)),
                  pl.BlockSpec((tm, HEADS), lambda i: (i, 0))],
        out_specs=[pl.BlockSpec((tm, qk), lambda i: (i, 0)), pl.BlockSpec((tm, qk), lambda i: (i, 0))],
        out_shape=[jax.ShapeDtypeStruct((T, qk), BF16), jax.ShapeDtypeStruct((T, qk), BF16)],
        compiler_params=_cparams(("parallel",)),
        name="fox_prep",
    )(big, big, c_rows)


def _nt_dot(a, b):
    return lax.dot_general(a, b, (((1,), (1,)), ((), ())), preferred_element_type=F32)


def _softmax_step(st, m, acc, vt):
    m_new = jnp.maximum(m, jnp.max(st, axis=0, keepdims=True))
    upd = jnp.dot(vt, jnp.exp2(st - m_new).astype(BF16), preferred_element_type=F32)
    return m_new, jnp.exp2(m - m_new) * acc + upd


def _softmax_init(tq):
    return jnp.full((1, tq), M_INIT, F32), jnp.zeros((V_ROWS, tq), F32)


def _softmax_finish(acc):
    return (acc[:HEAD_DIM] / acc[HEAD_DIM:HEAD_DIM + 1]).T


def _attend(npairs, qk, val, past, diag, tq, sa, sb):
    heads = range(len(sa))

    def put(dst, j):
        for hh in heads:
            dst[hh][...] = qk(hh, j)

    def advance(carry, src, j, fn):
        return tuple(_softmax_step(fn(hh, j, src[hh][...]), *carry[hh], val(hh, j)) for hh in heads)

    put(sa, 0)

    def body(jj, carry):
        j0 = 2 * jj
        put(sb, j0 + 1)
        carry = advance(carry, sa, j0, past)
        put(sa, j0 + 2)
        return advance(carry, sb, j0 + 1, past)

    carry = lax.fori_loop(0, npairs, body, tuple(_softmax_init(tq) for _ in heads))
    j0 = 2 * npairs
    put(sb, j0 + 1)
    carry = advance(carry, sa, j0, lambda hh, j, st: diag(hh, 0, st))
    carry = advance(carry, sb, j0 + 1, lambda hh, j, st: diag(hh, 1, st))
    return jnp.concatenate([_softmax_finish(acc) for _, acc in carry], axis=1)


def _score_scratch(tq, tk, nh):
    return [pltpu.VMEM((tk, tq), F32) for _ in range(2 * nh)]


def _flash_kernel(q_ref, k_ref, vt_ref, o_ref, *scores, tq, tk, nh):
    i = pl.program_id(2)
    assert tq == 2 * tk
    krow = lax.broadcasted_iota(jnp.int32, (tk, tq), 0)
    qcol = lax.broadcasted_iota(jnp.int32, (tk, tq), 1)

    def qk(hh, j):
        return _nt_dot(k_ref[0, j, :, hh * LANES:(hh + 1) * LANES], q_ref[0, :, hh * LANES:(hh + 1) * LANES])

    def val(hh, j):
        return vt_ref[0, j, hh * V_ROWS:(hh + 1) * V_ROWS, :]

    def diag(hh, d, st):
        return jnp.where(d * tk + krow <= qcol, st, NEG_INF)

    out = _attend(i, qk, val, lambda hh, j, st: st, diag, tq, scores[:nh], scores[nh:])
    o_ref[0] = out.astype(BF16)


def _flash(q, k, vt, tq, tk, nh=FLASH_HEADS):
    B, S, _ = q.shape
    nk = S // tk
    return pl.pallas_call(
        functools.partial(_flash_kernel, tq=tq, tk=tk, nh=nh), grid=(B, HEADS // nh, S // tq),
        in_specs=[pl.BlockSpec((1, tq, nh * LANES), lambda b, h, i: (b, i, h)),
                  pl.BlockSpec((1, nk, tk, nh * LANES), lambda b, h, i: (b, 0, 0, h)),
                  pl.BlockSpec((1, nk, nh * V_ROWS, tk), lambda b, h, i: (b, 0, h, 0))],
        out_specs=pl.BlockSpec((1, tq, nh * HEAD_DIM), lambda b, h, i: (b, i, h)),
        out_shape=jax.ShapeDtypeStruct((B, S, BRANCH_WIDTH), BF16),
        scratch_shapes=_score_scratch(tq, tk, nh),
        compiler_params=_cparams(("parallel", "parallel", "arbitrary")),
        name="flash",
    )(q, k, vt)


def _moba_kernel(q_ref, k_ref, vt_ref, o_ref, kmean_ref, bias_ref, qm_ref, *scores, nblk, tq, nh):
    i = pl.program_id(2)
    blk = MOBA_BLOCK
    shift = blk.bit_length() - 1
    r = tq // blk

    @pl.when(i == 0)
    def _():
        for n in range(nblk):
            kmean_ref[n:n + 1, :] = jnp.mean(k_ref[0, n].astype(F32), axis=0, keepdims=True)

    lane = lax.broadcasted_iota(jnp.int32, (tq, nh * HEAD_DIM), 1)
    blk_id = lax.broadcasted_iota(jnp.int32, (nblk, tq), 0)
    own = i * r + (lax.broadcasted_iota(jnp.int32, (nblk, tq), 1) >> shift)
    q2 = q_ref[0]
    for hh in range(nh):
        in_head = (lane >= hh * HEAD_DIM) & (lane < (hh + 1) * HEAD_DIM)
        q = jnp.where(in_head, q2, jnp.zeros_like(q2))
        qm_ref[hh] = q
        g3 = _nt_dot(jnp.concatenate([t.astype(BF16) for t in _split3(kmean_ref[...])], axis=0), q)
        g = g3[:nblk] + g3[nblk:2 * nblk] + g3[2 * nblk:]
        g = jnp.where(blk_id < own, g, NEG_INF)
        bias = jnp.full((nblk, tq), NEG_INF, F32)
        for _ in range(MOBA_TOPK):
            mx = jnp.max(g, axis=0, keepdims=True)
            first = jnp.min(jnp.where(g == mx, blk_id, nblk), axis=0, keepdims=True)
            pick = (blk_id == first) & (mx > NEG_INF)
            bias = jnp.where(pick, 0.0, bias)
            g = jnp.where(pick, NEG_INF, g)
        bias_ref[hh] = bias

    krow = lax.broadcasted_iota(jnp.int32, (blk, tq), 0)
    qcol = lax.broadcasted_iota(jnp.int32, (blk, tq), 1)

    def qk(hh, n):
        return _nt_dot(k_ref[0, n], qm_ref[hh])

    def val(hh, n):
        return vt_ref[0, n, hh * V_ROWS:(hh + 1) * V_ROWS, :]

    def past(hh, n, st):
        return st + bias_ref[hh, pl.ds(n, 1), :]

    def diag(hh, d, st):
        own_causal = ((qcol >> shift) == d) & (krow <= (qcol & (blk - 1)))
        return jnp.where(own_causal, st, past(hh, i * r + d, st))

    out = _attend(i, qk, val, past, diag, tq, scores[:nh], scores[nh:])
    o_ref[0] = out.astype(BF16)


def _moba(q, k, vt, tq, nh=MOBA_STEP_HEADS):
    B, S, _ = q.shape
    nblk = S // MOBA_BLOCK
    assert tq == 2 * MOBA_BLOCK and MOBA_BLOCK & (MOBA_BLOCK - 1) == 0
    w = nh * HEAD_DIM
    qc = COL_MQ // w
    kc = COL_MK // w
    return pl.pallas_call(
        functools.partial(_moba_kernel, nblk=nblk, tq=tq, nh=nh), grid=(B, HEADS // nh, S // tq),
        in_specs=[pl.BlockSpec((1, tq, w), lambda b, h, i: (b, i, qc + h)),
                  pl.BlockSpec((1, nblk, MOBA_BLOCK, w), lambda b, h, i: (b, 0, 0, kc + h)),
                  pl.BlockSpec((1, nblk, nh * V_ROWS, MOBA_BLOCK), lambda b, h, i: (b, 0, h, 0))],
        out_specs=pl.BlockSpec((1, tq, w), lambda b, h, i: (b, i, h)),
        out_shape=jax.ShapeDtypeStruct((B, S, BRANCH_WIDTH), BF16),
        scratch_shapes=[pltpu.VMEM((nblk, w), F32), pltpu.VMEM((nh, nblk, tq), F32),
                        pltpu.VMEM((nh, tq, w), BF16)] + _score_scratch(tq, MOBA_BLOCK, nh),
        compiler_params=_cparams(("parallel", "parallel", "arbitrary")),
        name="moba",
    )(q, k, vt)


def _merge_kernel(x_ref, ya_ref, yb_ref, yc_ref, ga_ref, gb_ref, gc_ref, wb_ref, wo_ref, o_ref):
    merged = None
    for n, (y_ref, g_ref) in enumerate(((ya_ref, ga_ref), (yb_ref, gb_ref), (yc_ref, gc_ref))):
        proj = jnp.dot(y_ref[...], wb_ref[n], preferred_element_type=F32)
        term = jax.nn.sigmoid(g_ref[...].astype(F32)) * proj
        merged = term if merged is None else merged + term
    o_ref[...] = x_ref[...] + jnp.dot(merged.astype(BF16), wo_ref[...], preferred_element_type=F32)


def _merge(x, y_mla, y_fox, y_moba, big, w_branch, w_out, tm=256):
    T = x.shape[0]
    g0 = COL_GATES // D_MODEL
    row = lambda n: pl.BlockSpec((tm, n), lambda i: (i, 0))
    gate = lambda n: pl.BlockSpec((tm, D_MODEL), lambda i: (i, g0 + n))
    return pl.pallas_call(
        _merge_kernel, grid=(T // tm,),
        in_specs=[row(D_MODEL), row(BRANCH_WIDTH), row(BRANCH_WIDTH), row(BRANCH_WIDTH),
                  gate(0), gate(1), gate(2),
                  pl.BlockSpec((3, BRANCH_WIDTH, D_MODEL), lambda i: (0, 0, 0)),
                  pl.BlockSpec((D_MODEL, D_MODEL), lambda i: (0, 0))],
        out_specs=row(D_MODEL),
        out_shape=jax.ShapeDtypeStruct((T, D_MODEL), F32),
        compiler_params=_cparams(("parallel",)),
        name="merge",
    )(x, y_mla, y_fox, y_moba, big, big, big, w_branch, w_out)


def _dense_ffn_kernel(x_ref, g_ref, wg_ref, wu_ref, wd_ref, o_ref):
    x = x_ref[...]
    h = _rms(x, g_ref[...]).astype(BF16)
    gate = jnp.dot(h, wg_ref[...], preferred_element_type=F32)
    up = jnp.dot(h, wu_ref[...], preferred_element_type=F32)
    act = (jax.nn.silu(gate) * up).astype(BF16)
    o_ref[...] = x + jnp.dot(act, wd_ref[...], preferred_element_type=F32)


def _dense_ffn(x, g, w_gu, w_down, tm=512):
    T = x.shape[0]
    return pl.pallas_call(
        _dense_ffn_kernel, grid=(T // tm,),
        in_specs=[pl.BlockSpec((tm, D_MODEL), lambda i: (i, 0)),
                  _resident((1, D_MODEL), lambda i: (0, 0)),
                  _resident((D_MODEL, DENSE_FF), lambda i: (0, 0)),
                  _resident((D_MODEL, DENSE_FF), lambda i: (0, 1)),
                  _resident((DENSE_FF, D_MODEL), lambda i: (0, 0))],
        out_specs=pl.BlockSpec((tm, D_MODEL), lambda i: (i, 0)),
        out_shape=jax.ShapeDtypeStruct((T, D_MODEL), F32),
        compiler_params=_cparams(("parallel",)),
        name="dense_ffn",
    )(x, g, w_gu, w_gu, w_down)


def _router_kernel(x_ref, g_ref, wr_ref, h_ref, r_ref):
    h = _rms(x_ref[...], g_ref[...])
    h_ref[...] = h
    logits = jnp.dot(h, wr_ref[...], precision=lax.Precision.HIGHEST, preferred_element_type=F32)
    lane = lax.broadcasted_iota(jnp.int32, logits.shape, 1)
    logits = jnp.where(lane < N_EXPERTS, logits, NEG_INF)
    m1 = jnp.max(logits, axis=-1, keepdims=True)
    i1 = jnp.min(jnp.where(logits == m1, lane, LANES), axis=-1, keepdims=True)
    rest = jnp.where(lane == i1, NEG_INF, logits)
    m2 = jnp.max(rest, axis=-1, keepdims=True)
    i2 = jnp.min(jnp.where(rest == m2, lane, LANES), axis=-1, keepdims=True)
    e2 = jnp.exp(m2 - m1)
    w1 = 1.0 / (1.0 + e2)
    w2 = e2 / (1.0 + e2)
    r_ref[...] = jnp.where(lane == 0, i1.astype(F32), jnp.where(lane == 1, i2.astype(F32),
                           jnp.where(lane == 2, w1, jnp.where(lane == 3, w2, 0.0))))


def _router(x, g, w_router_pad, tm=512):
    T = x.shape[0]
    return pl.pallas_call(
        _router_kernel, grid=(T // tm,),
        in_specs=[pl.BlockSpec((tm, D_MODEL), lambda i: (i, 0)),
                  pl.BlockSpec((1, D_MODEL), lambda i: (0, 0)),
                  pl.BlockSpec((D_MODEL, LANES), lambda i: (0, 0))],
        out_specs=[pl.BlockSpec((tm, D_MODEL), lambda i: (i, 0)), pl.BlockSpec((tm, LANES), lambda i: (i, 0))],
        out_shape=[jax.ShapeDtypeStruct((T, D_MODEL), F32), jax.ShapeDtypeStruct((T, LANES), F32)],
        compiler_params=_cparams(("parallel",)),
        name="router",
    )(x, g, w_router_pad)


GATHER_WINDOW = 128
GATHER_ROWS = 32


def _gather_rows(src, idx):
    M = idx.shape[0]
    C = src.shape[1]
    mesh = plsc.VectorSubcoreMesh(core_axis_name="core", subcore_axis_name="subcore")
    per = M // (mesh.num_cores * mesh.num_subcores)
    assert per * mesh.num_cores * mesh.num_subcores == M and per % GATHER_WINDOW == 0

    @pl.kernel(out_type=jax.ShapeDtypeStruct((M, C), src.dtype), mesh=mesh, name="gather_rows",
               scratch_types=[pltpu.VMEM((GATHER_WINDOW,), jnp.int32), pltpu.VMEM((GATHER_ROWS, C), src.dtype)])
    def gather(x_hbm, i_hbm, o_hbm, idx_v, buf):
        w = lax.axis_index("core") * mesh.num_subcores + lax.axis_index("subcore")

        @pl.loop(0, per // GATHER_WINDOW)
        def _(t):
            base = w * per + t * GATHER_WINDOW
            pltpu.sync_copy(i_hbm.at[pl.ds(base, GATHER_WINDOW)], idx_v)
            for k in range(GATHER_WINDOW // GATHER_ROWS):
                pltpu.sync_copy(x_hbm.at[idx_v.at[pl.ds(k * GATHER_ROWS, GATHER_ROWS)]], buf)
                pltpu.sync_copy(buf, o_hbm.at[pl.ds(base + k * GATHER_ROWS, GATHER_ROWS)])

    return gather(src, idx)


def _scatter_rows(src, dest0, dest1, n_rows):
    T, C = src.shape
    mesh = plsc.VectorSubcoreMesh(core_axis_name="core", subcore_axis_name="subcore")
    per = T // (mesh.num_cores * mesh.num_subcores)
    assert per * mesh.num_cores * mesh.num_subcores == T and per % GATHER_WINDOW == 0

    @pl.kernel(out_type=jax.ShapeDtypeStruct((n_rows, C), src.dtype), mesh=mesh, name="scatter_rows",
               scratch_types=[pltpu.VMEM((GATHER_WINDOW,), jnp.int32), pltpu.VMEM((GATHER_WINDOW,), jnp.int32),
                              pltpu.VMEM((GATHER_ROWS, C), src.dtype)])
    def scatter(x_hbm, d0_hbm, d1_hbm, o_hbm, i0, i1, buf):
        w = lax.axis_index("core") * mesh.num_subcores + lax.axis_index("subcore")

        @pl.loop(0, per // GATHER_WINDOW)
        def _(t):
            base = w * per + t * GATHER_WINDOW
            pltpu.sync_copy(d0_hbm.at[pl.ds(base, GATHER_WINDOW)], i0)
            pltpu.sync_copy(d1_hbm.at[pl.ds(base, GATHER_WINDOW)], i1)
            for k in range(GATHER_WINDOW // GATHER_ROWS):
                pltpu.sync_copy(x_hbm.at[pl.ds(base + k * GATHER_ROWS, GATHER_ROWS)], buf)
                pltpu.sync_copy(buf, o_hbm.at[i0.at[pl.ds(k * GATHER_ROWS, GATHER_ROWS)]])
                pltpu.sync_copy(buf, o_hbm.at[i1.at[pl.ds(k * GATHER_ROWS, GATHER_ROWS)]])

    return scatter(src, dest0, dest1)


def _expert_kernel(be_ref, nused_ref, valid_ref, x_ref, wg_ref, wu_ref, wd_ref, o_ref, xb_ref, acc_ref):
    b = pl.program_id(0)
    f = pl.program_id(1)
    last = pl.num_programs(1) - 1
    used = b < nused_ref[0]

    @pl.when(used)
    def _():
        @pl.when(f == 0)
        def _():
            row = lax.broadcasted_iota(jnp.int32, (x_ref.shape[0], 1), 0)
            xb_ref[...] = jnp.where(row < valid_ref[b], x_ref[...], 0.0).astype(BF16)

        x = xb_ref[...]
        gate = jnp.dot(x, wg_ref[0], preferred_element_type=F32)
        up = jnp.dot(x, wu_ref[0], preferred_element_type=F32)
        act = (jax.nn.silu(gate) * up).astype(BF16)
        y = jnp.dot(act, wd_ref[0], preferred_element_type=F32)

        @pl.when(f == 0)
        def _():
            acc_ref[...] = y

        @pl.when(jnp.logical_and(f > 0, f < last))
        def _():
            acc_ref[...] += y

        @pl.when(f == last)
        def _():
            o_ref[...] = acc_ref[...] + y

    @pl.when(jnp.logical_and(f == last, jnp.logical_not(used)))
    def _():
        o_ref[...] = jnp.zeros_like(o_ref)


def _experts(x_rows, block_e, n_used, valid, w_gu, w_down, bm, tf=1792):
    n_rows = x_rows.shape[0]
    nf = EXPERT_FF // tf
    nb = n_rows // bm

    def fsel(b, f, nused):
        return jnp.where(b < nused[0], f, nf - 1)

    grid_spec = pltpu.PrefetchScalarGridSpec(
        num_scalar_prefetch=3, grid=(nb, nf),
        in_specs=[pl.BlockSpec((bm, D_MODEL), lambda b, f, be, nu, va: (b, 0)),
                  pl.BlockSpec((1, D_MODEL, tf), lambda b, f, be, nu, va: (be[b], 0, fsel(b, f, nu))),
                  pl.BlockSpec((1, D_MODEL, tf), lambda b, f, be, nu, va: (be[b], 0, nf + fsel(b, f, nu))),
                  pl.BlockSpec((1, tf, D_MODEL), lambda b, f, be, nu, va: (be[b], fsel(b, f, nu), 0))],
        out_specs=pl.BlockSpec((bm, D_MODEL), lambda b, f, be, nu, va: (b, 0)),
        scratch_shapes=[pltpu.VMEM((bm, D_MODEL), BF16), pltpu.VMEM((bm, D_MODEL), F32)])
    assert nf >= 2
    return pl.pallas_call(
        _expert_kernel, grid_spec=grid_spec,
        out_shape=jax.ShapeDtypeStruct((n_rows, D_MODEL), F32),
        compiler_params=_cparams(("arbitrary", "arbitrary"), EXPERT_VMEM_LIMIT),
        name="experts",
    )(block_e, n_used, valid, x_rows, w_gu, w_gu, w_down)


def _combine_kernel(x_ref, y0_ref, y1_ref, r_ref, g_ref, o_ref):
    r = r_ref[...]
    x = x_ref[...] + r[:, 2:3] * y0_ref[...] + r[:, 3:4] * y1_ref[...]
    o_ref[...] = _rms(x, g_ref[...])


def _combine(x, y_slots, route, g_final, tm=512):
    T = x.shape[0]
    return pl.pallas_call(
        _combine_kernel, grid=(T // tm,),
        in_specs=[pl.BlockSpec((tm, D_MODEL), lambda i: (i, 0)),
                  pl.BlockSpec((tm, D_MODEL), lambda i: (i, 0)),
                  pl.BlockSpec((tm, D_MODEL), lambda i: (T // tm + i, 0)),
                  pl.BlockSpec((tm, LANES), lambda i: (i, 0)),
                  pl.BlockSpec((1, D_MODEL), lambda i: (0, 0))],
        out_specs=pl.BlockSpec((tm, D_MODEL), lambda i: (i, 0)),
        out_shape=jax.ShapeDtypeStruct((T, D_MODEL), F32),
        compiler_params=_cparams(("parallel",)),
        name="combine",
    )(x, y_slots, y_slots, route, g_final)


def _pad_cols(w, n):
    return jnp.pad(w, ((0, 0), (0, n - w.shape[1])))


def _split_in_weights(w_in):
    sizes = [MLA_Q_LORA, MLA_KV_LORA, MLA_ROPE_DIM, BRANCH_WIDTH, BRANCH_WIDTH, BRANCH_WIDTH, HEADS,
             BRANCH_WIDTH, BRANCH_WIDTH, BRANCH_WIDTH, 3 * D_MODEL]
    pts = np.cumsum(sizes)[:-1]
    c_q, c_kv, k_pe, fq, fk, fv, fl, mq, mk, mv, gates = jnp.split(w_in, pts, axis=1)
    zeros = lambda n: jnp.zeros((D_MODEL, n), w_in.dtype)
    pe_tile = jnp.concatenate([zeros(HEAD_DIM), k_pe, zeros(LANES - HEAD_DIM - MLA_ROPE_DIM)], axis=1)
    w_small = jnp.concatenate([c_q, c_kv, pe_tile, _pad_cols(fl, LANES)], axis=1)
    w_big = jnp.concatenate([fq, mq, fk, mk, fv, mv, gates], axis=1)
    return w_small.astype(BF16), w_big.astype(BF16)


def _mla_up_weights(w_uq, w_ukv):
    dq = HEAD_DIM + MLA_ROPE_DIM
    wq = w_uq.reshape(MLA_Q_LORA, HEADS, dq)
    wq = jnp.pad(wq, ((0, 0), (0, 0), (0, LANES - dq))).reshape(MLA_Q_LORA, HEADS * LANES)
    wkv = w_ukv.reshape(MLA_KV_LORA, HEADS, 2 * HEAD_DIM)
    wk = jnp.pad(wkv[:, :, :HEAD_DIM], ((0, 0), (0, 0), (0, LANES - HEAD_DIM))).reshape(MLA_KV_LORA, HEADS * LANES)
    wv = wkv[:, :, HEAD_DIM:].reshape(MLA_KV_LORA, BRANCH_WIDTH)
    return wq.astype(BF16), wk.astype(BF16), wv.astype(BF16)


def _chunk_rows(a, B, S, t):
    return a.reshape(B, S // t, t, a.shape[-1])


ATT_TQ = 512
ATT_TK = KV_CHUNK
MOBA_TQ = 512
MOE_BM = 512


def _token_mixers(x, B, S, g_mix, w_in, g_q_lat, g_kv_lat, w_uq, w_ukv, b_forget, w_branch, w_out,
                  tabs_mla, tabs_moba):
    T = B * S
    w_small, w_big = _split_in_weights(w_in)
    wq, wk, wv = _mla_up_weights(w_uq, w_ukv)
    g = g_mix[None, :]

    big, vt_fox, vt_moba = _inproj(x, g, w_big, tabs_moba, B, S)
    q_mla, k_mla, vt_mla, f_logit = _mla_prep(
        x, g, w_small, g_q_lat[None, :], g_kv_lat[None, :], wq, wk, wv, tabs_mla,
        (HEAD_DIM + MLA_ROPE_DIM) ** -0.5 * LOG2E, B, S)

    y_mla = _flash(q_mla.reshape(B, S, -1), _chunk_rows(k_mla, B, S, ATT_TK), vt_mla, ATT_TQ, ATT_TK)

    c = _cumlogf(f_logit.reshape(B, S, HEADS).transpose(0, 2, 1), b_forget[:, None])
    c_rows = c.transpose(0, 2, 1).reshape(T, HEADS)
    q_fox, k_fox = _fox_prep(big, c_rows)
    y_fox = _flash(q_fox.reshape(B, S, -1), _chunk_rows(k_fox, B, S, ATT_TK), vt_fox, ATT_TQ, ATT_TK)

    y_moba = _moba(big.reshape(B, S, -1), _chunk_rows(big, B, S, MOBA_BLOCK), vt_moba, MOBA_TQ)

    return _merge(x, y_mla.reshape(T, -1), y_fox.reshape(T, -1), y_moba.reshape(T, -1), big,
                  w_branch.astype(BF16), w_out.astype(BF16))


def _moe(x, g_ffn, w_router, w_gu, w_down, g_final):
    T = x.shape[0]
    A = 2 * T
    bm = MOE_BM
    h, route = _router(x, g_ffn[None, :], _pad_cols(w_router, LANES))
    top_e = route[:, :2].astype(jnp.int32).reshape(A)
    onehot = (top_e[:, None] == jnp.arange(N_EXPERTS)[None, :]).astype(jnp.int32)
    csum = jnp.cumsum(onehot, axis=0)
    counts = csum[-1]
    rank = jnp.sum((csum - onehot) * onehot, axis=1)
    padded = (counts + bm - 1) // bm * bm
    pad_end = jnp.cumsum(padded)
    dest = (pad_end - padded)[top_e] + rank
    n_rows = (A // bm + N_EXPERTS) * bm
    block_start = jnp.arange(n_rows // bm) * bm
    block_e = jnp.minimum(jnp.searchsorted(pad_end, block_start, side="right"), N_EXPERTS - 1).astype(jnp.int32)
    n_used = (pad_end[-1:] // bm).astype(jnp.int32)
    valid = jnp.clip((pad_end - padded + counts)[block_e] - block_start, 0, bm).astype(jnp.int32)
    dest_slots = dest.astype(jnp.int32).reshape(T, 2).T

    x_rows = _scatter_rows(h, dest_slots[0], dest_slots[1], n_rows)
    y_rows = _experts(x_rows, block_e, n_used, valid, w_gu.astype(BF16), w_down.astype(BF16), bm)
    y_slots = _gather_rows(y_rows, dest_slots.reshape(A))
    return _combine(x, y_slots, route, g_final[None, :])


def kernel(x, positions, g_mix, w_in, g_q_lat, g_kv_lat, w_uq, w_ukv, b_forget, w_branch, w_out, g_ffn,
           w_dense_gu, w_dense_down, w_router, w_exp_gu, w_exp_down, g_final):
    B, S, D = x.shape
    T = B * S
    depth = g_mix.shape[0]
    assert depth == 2 and D == D_MODEL and S % MOBA_BLOCK == 0 and KV_CHUNK == MOBA_BLOCK
    pos = positions.reshape(T, 1).astype(jnp.int32)
    tabs_mla = _rope_tables(pos, *_rope_patterns(LANES, HEAD_DIM, MLA_ROPE_DIM // 2))
    tabs_moba = _rope_tables(pos, *_rope_patterns(HEAD_DIM, 0, HEAD_DIM // 2))
    x = x.reshape(T, D)
    for l in range(depth):
        x = _token_mixers(x, B, S, g_mix[l], w_in[l], g_q_lat[l], g_kv_lat[l], w_uq[l], w_ukv[l],
                          b_forget[l], w_branch[l], w_out[l], tabs_mla, tabs_moba)
        if l % 2 == 0:
            x = _dense_ffn(x, g_ffn[l][None, :], w_dense_gu[l // 2].astype(BF16),
                           w_dense_down[l // 2].astype(BF16))
        else:
            x = _moe(x, g_ffn[l], w_router[l // 2], w_exp_gu[l // 2], w_exp_down[l // 2], g_final)
    return x.reshape(B, S, D)
```

```python
import functools
import math

import jax
import jax.numpy as jnp
import numpy as np
from jax import lax
from jax.experimental import pallas as pl
from jax.experimental.pallas import tpu as pltpu
from jax.experimental.pallas import tpu_sc as plsc

F32 = jnp.float32
BF16 = jnp.bfloat16
NEG_INF = float("-inf")
M_INIT = -1e30
LOG2E = math.log2(math.e)

D_MODEL = 1024
RMS_EPS = 1e-6
ROPE_THETA = 10000.0
HEADS = 8
HEAD_DIM = 64
V_ROWS = 80
Q_SCALE = HEAD_DIM ** -0.5 * math.log2(math.e)
MLA_Q_LORA = 256
MLA_KV_LORA = 128
MLA_ROPE_DIM = 32
BRANCH_WIDTH = HEADS * HEAD_DIM
MOBA_BLOCK = 256
MOBA_TOPK = 3
DENSE_FF = 2816
N_EXPERTS = 8
EXPERT_FF = 3584

LANES = 128
VMEM_LIMIT = 48 * 1024 * 1024
EXPERT_VMEM_LIMIT = 58 * 1024 * 1024

COL_FQ, COL_MQ, COL_FK, COL_MK, COL_GATES = (n * BRANCH_WIDTH for n in range(5))
BIG_COLS = COL_GATES + 3 * D_MODEL
BIG_TN = 2 * BRANCH_WIDTH
W_TILES = BIG_COLS // BIG_TN + 1
V_TILE = 2
KV_CHUNK = 256
FLASH_HEADS = 4
MOBA_STEP_HEADS = 2
SMALL_COLS = MLA_Q_LORA + MLA_KV_LORA + 2 * LANES


def _cparams(sem, vmem_limit=VMEM_LIMIT):
    return pltpu.CompilerParams(dimension_semantics=sem, vmem_limit_bytes=vmem_limit)


def _resident(shape, index_map):
    return pl.BlockSpec(shape, index_map, pipeline_mode=pl.Buffered(1))


def _rms(x, g):
    return x * lax.rsqrt(jnp.mean(x * x, axis=-1, keepdims=True) + RMS_EPS) * g


def _rope_table_kernel(pos_ref, f_ref, mc_ref, m1_ref, m2_ref, c_ref, s1_ref, s2_ref):
    ang = pos_ref[...].astype(F32) * f_ref[...]
    cos = jnp.cos(ang)
    sin = jnp.sin(ang)
    mc = mc_ref[...]
    c_ref[...] = cos * mc + (1.0 - mc)
    s1_ref[...] = sin * m1_ref[...]
    s2_ref[...] = sin * m2_ref[...]


def _rope_tables(pos, freq, mc, m1, m2, tm=1024):
    T = pos.shape[0]
    row = pl.BlockSpec((tm, 1), lambda i: (i, 0))
    pat = pl.BlockSpec((1, LANES), lambda i: (0, 0))
    out = pl.BlockSpec((tm, LANES), lambda i: (i, 0))
    shp = jax.ShapeDtypeStruct((T, LANES), F32)
    return pl.pallas_call(
        _rope_table_kernel, grid=(T // tm,),
        in_specs=[row, pat, pat, pat, pat], out_specs=[out, out, out],
        out_shape=[shp, shp, shp], compiler_params=_cparams(("parallel",)),
        name="rope_tables",
    )(pos, freq, mc, m1, m2)


def _rope_patterns(group, x1_lo, half):
    d = 2 * half
    inv_freq = jnp.exp(-math.log(ROPE_THETA) * jnp.arange(half, dtype=F32) * 2.0 / d)
    lane = np.arange(LANES) % group
    in_x1 = (lane >= x1_lo) & (lane < x1_lo + half)
    in_x2 = (lane >= x1_lo + half) & (lane < x1_lo + d)
    k = np.where(in_x1, lane - x1_lo, np.where(in_x2, lane - x1_lo - half, 0))
    freq = jnp.where(jnp.asarray(in_x1 | in_x2), inv_freq[k], 0.0)[None, :].astype(F32)
    mc = jnp.asarray((in_x1 | in_x2).astype(np.float32))[None, :]
    m1 = jnp.asarray(-(in_x1.astype(np.float32)))[None, :]
    m2 = jnp.asarray(in_x2.astype(np.float32))[None, :]
    return freq, mc, m1, m2


def _apply_rope(x, c, s1, s2, half):
    n = x.shape[-1]
    reps = n // LANES
    c, s1, s2 = (jnp.tile(t, (1, reps)) if reps > 1 else t for t in (c, s1, s2))
    return x * c + pltpu.roll(x, n - half, 1) * s1 + pltpu.roll(x, half, 1) * s2


def _store_value_tiles(vt_ref, a):
    row = lax.broadcasted_iota(jnp.int32, (V_ROWS - HEAD_DIM, KV_CHUNK), 0)
    pad = jnp.where(row == 0, 1.0, 0.0).astype(BF16)
    for c in range(a.shape[0] // KV_CHUNK):
        at = a[c * KV_CHUNK:(c + 1) * KV_CHUNK, :].T.astype(BF16)
        for h in range(HEADS):
            vt_ref[0, c, h * V_ROWS:h * V_ROWS + HEAD_DIM, :] = at[h * HEAD_DIM:(h + 1) * HEAD_DIM]
            vt_ref[0, c, h * V_ROWS + HEAD_DIM:(h + 1) * V_ROWS, :] = pad


def _value_tile_spec(tm, S, grid_rank):
    per = S // tm
    imap = (lambda i: (i // per, i % per, 0, 0)) if grid_rank == 1 else (lambda i, j: (i // per, i % per, 0, 0))
    return pl.BlockSpec((1, tm // KV_CHUNK, HEADS * V_ROWS, KV_CHUNK), imap)


def _value_tile_shape(B, S):
    return jax.ShapeDtypeStruct((B, S // KV_CHUNK, HEADS * V_ROWS, KV_CHUNK), BF16)


def _inproj_kernel(x_ref, g_ref, w_ref, c_ref, s1_ref, s2_ref, o_ref, vf_ref, vm_ref):
    h = _rms(x_ref[...], g_ref[...]).astype(BF16)
    c, s1, s2 = c_ref[...], s1_ref[...], s2_ref[...]

    def tile(t):
        return jnp.dot(h, w_ref[:, t * BIG_TN:(t + 1) * BIG_TN], preferred_element_type=F32)

    for t in range(V_TILE):
        a = tile(t) * Q_SCALE if t == 0 else tile(t)
        o_ref[:, t * BIG_TN:t * BIG_TN + BRANCH_WIDTH] = a[:, :BRANCH_WIDTH].astype(BF16)
        o_ref[:, t * BIG_TN + BRANCH_WIDTH:(t + 1) * BIG_TN] = _apply_rope(
            a[:, BRANCH_WIDTH:], c, s1, s2, HEAD_DIM // 2).astype(BF16)
    a = tile(V_TILE)
    _store_value_tiles(vf_ref, a[:, :BRANCH_WIDTH])
    _store_value_tiles(vm_ref, a[:, BRANCH_WIDTH:])
    for t in range(V_TILE + 1, W_TILES):
        o_ref[:, (t - 1) * BIG_TN:t * BIG_TN] = tile(t).astype(BF16)


def _inproj(x, g, w_big, tabs, B, S, tm=512):
    T = x.shape[0]
    tab = pl.BlockSpec((tm, LANES), lambda i: (i, 0))
    vspec = _value_tile_spec(tm, S, 1)
    return pl.pallas_call(
        _inproj_kernel, grid=(T // tm,),
        in_specs=[pl.BlockSpec((tm, D_MODEL), lambda i: (i, 0)),
                  _resident((1, D_MODEL), lambda i: (0, 0)),
                  _resident((D_MODEL, W_TILES * BIG_TN), lambda i: (0, 0)),
                  tab, tab, tab],
        out_specs=[pl.BlockSpec((tm, BIG_COLS), lambda i: (i, 0)), vspec, vspec],
        out_shape=[jax.ShapeDtypeStruct((T, BIG_COLS), BF16), _value_tile_shape(B, S), _value_tile_shape(B, S)],
        compiler_params=_cparams(("parallel",)),
        name="inproj",
    )(x, g, w_big, *tabs)


def _mla_prep_kernel(x_ref, g_ref, ws_ref, gq_ref, gkv_ref, wq_ref, wk_ref, wv_ref,
                     c_ref, s1_ref, s2_ref, q_ref, k_ref, v_ref, fl_ref, *, scale):
    h = _rms(x_ref[...], g_ref[...]).astype(BF16)
    small = jnp.dot(h, ws_ref[...], preferred_element_type=F32)
    c_q = small[:, :MLA_Q_LORA]
    c_kv = small[:, MLA_Q_LORA:MLA_Q_LORA + MLA_KV_LORA]
    k_pe = small[:, MLA_Q_LORA + MLA_KV_LORA:MLA_Q_LORA + MLA_KV_LORA + LANES]
    fl_ref[...] = small[:, SMALL_COLS - LANES:SMALL_COLS - LANES + HEADS]
    c, s1, s2 = c_ref[...], s1_ref[...], s2_ref[...]
    half = MLA_ROPE_DIM // 2
    qn = _rms(c_q, gq_ref[...]).astype(BF16)
    q = jnp.dot(qn, wq_ref[...], preferred_element_type=F32) * scale
    q_ref[...] = _apply_rope(q, c, s1, s2, half).astype(BF16)
    kvn = _rms(c_kv, gkv_ref[...]).astype(BF16)
    k_nope = jnp.dot(kvn, wk_ref[...], preferred_element_type=F32)
    k_rot = _apply_rope(k_pe, c, s1, s2, half)
    k_ref[...] = (k_nope + jnp.tile(k_rot, (1, HEADS))).astype(BF16)
    _store_value_tiles(v_ref, jnp.dot(kvn, wv_ref[...], preferred_element_type=F32))


def _mla_prep(x, g, w_small, g_q, g_kv, wq, wk, wv, tabs, scale, B, S, tm=512):
    T = x.shape[0]
    full = lambda shape: pl.BlockSpec(shape, lambda i: (0,) * len(shape))
    row = lambda n: pl.BlockSpec((tm, n), lambda i: (i, 0))
    qk = HEADS * LANES
    return pl.pallas_call(
        functools.partial(_mla_prep_kernel, scale=scale), grid=(T // tm,),
        in_specs=[row(D_MODEL), full((1, D_MODEL)), full((D_MODEL, SMALL_COLS)),
                  full((1, MLA_Q_LORA)), full((1, MLA_KV_LORA)),
                  full((MLA_Q_LORA, qk)), full((MLA_KV_LORA, qk)), full((MLA_KV_LORA, BRANCH_WIDTH)),
                  row(LANES), row(LANES), row(LANES)],
        out_specs=[row(qk), row(qk), _value_tile_spec(tm, S, 1), row(HEADS)],
        out_shape=[jax.ShapeDtypeStruct((T, qk), BF16), jax.ShapeDtypeStruct((T, qk), BF16),
                   _value_tile_shape(B, S), jax.ShapeDtypeStruct((T, HEADS), F32)],
        compiler_params=_cparams(("parallel",)),
        name="mla_prep",
    )(x, g, w_small, g_q, g_kv, wq, wk, wv, *tabs)


def _cumlogf_kernel(fl_ref, b_ref, c_ref):
    z = fl_ref[0] + b_ref[...]
    x = jnp.minimum(z, 0.0) - jnp.log1p(jnp.exp(-jnp.abs(z)))
    n = x.shape[-1]
    lane = lax.broadcasted_iota(jnp.int32, x.shape, 1)
    d = 1
    while d < n:
        x = x + jnp.where(lane >= d, pltpu.roll(x, d, 1), 0.0)
        d *= 2
    c_ref[0] = x


def _cumlogf(fl_t, b_col):
    B, H, S = fl_t.shape
    return pl.pallas_call(
        _cumlogf_kernel, grid=(B,),
        in_specs=[pl.BlockSpec((1, H, S), lambda b: (b, 0, 0)), pl.BlockSpec((H, 1), lambda b: (0, 0))],
        out_specs=pl.BlockSpec((1, H, S), lambda b: (b, 0, 0)),
        out_shape=jax.ShapeDtypeStruct((B, H, S), F32),
        compiler_params=_cparams(("parallel",)),
        name="cumlogf",
    )(fl_t, b_col)


def _split3(c):
    hi = c.astype(BF16)
    r = c - hi.astype(F32)
    mid = r.astype(BF16)
    lo = (r - mid.astype(F32)).astype(BF16)
    return hi.astype(F32), mid.astype(F32), lo.astype(F32)


def _fox_prep_kernel(q_ref, k_ref, c_ref, qo_ref, ko_ref):
    tm = q_ref.shape[0]
    lane = lax.broadcasted_iota(jnp.int32, (tm, LANES), 1)
    c = c_ref[...] * LOG2E
    for hp in range(HEADS // 2):
        q2 = q_ref[:, hp * LANES:(hp + 1) * LANES].astype(F32)
        k2 = k_ref[:, hp * LANES:(hp + 1) * LANES].astype(F32)
        for hh in range(2):
            h = 2 * hp + hh
            terms = [jnp.broadcast_to(t, (tm, LANES)) for t in _split3(c[:, h:h + 1])]
            qh = q2 if hh == 0 else pltpu.roll(q2, HEAD_DIM, 1)
            kh = k2 if hh == 0 else pltpu.roll(k2, HEAD_DIM, 1)
            q_aug = jnp.where(lane < HEAD_DIM + 3, 1.0, 0.0)
            k_aug = jnp.where((lane >= HEAD_DIM + 3) & (lane < HEAD_DIM + 6), 1.0, 0.0)
            for n, t in enumerate(terms):
                q_aug = jnp.where(lane == HEAD_DIM + 3 + n, t, q_aug)
                k_aug = jnp.where(lane == HEAD_DIM + n, -t, k_aug)
            q_aug = jnp.where(lane < HEAD_DIM, qh, q_aug)
            k_aug = jnp.where(lane < HEAD_DIM, kh, k_aug)
            qo_ref[:, h * LANES:(h + 1) * LANES] = q_aug.astype(BF16)
            ko_ref[:, h * LANES:(h + 1) * LANES] = k_aug.astype(BF16)


def _fox_prep(big, c_rows, tm=512):
    T = big.shape[0]
    qk = HEADS * LANES
    return pl.pallas_call(
        _fox_prep_kernel, grid=(T // tm,),
        in_specs=[pl.BlockSpec((tm, BRANCH_WIDTH), lambda i: (i, COL_FQ // BRANCH_WIDTH)),
                  pl.BlockSpec((tm, BRANCH_WIDTH), lambda i: (i, COL_FK // BRANCH_WIDTH)),
                  pl.BlockSpec((tm, HEADS), lambda i: (i, 0))],
        out_specs=[pl.BlockSpec((tm, qk), lambda i: (i, 0)), pl.BlockSpec((tm, qk), lambda i: (i, 0))],
        out_shape=[jax.ShapeDtypeStruct((T, qk), BF16), jax.ShapeDtypeStruct((T, qk), BF16)],
        compiler_params=_cparams(("parallel",)),
        name="fox_prep",
    )(big, big, c_rows)


def _nt_dot(a, b):
    return lax.dot_general(a, b, (((1,), (1,)), ((), ())), preferred_element_type=F32)


def _softmax_step(st, m, acc, vt):
    m_new = jnp.maximum(m, jnp.max(st, axis=0, keepdims=True))
    upd = jnp.dot(vt, jnp.exp2(st - m_new).astype(BF16), preferred_element_type=F32)
    return m_new, jnp.exp2(m - m_new) * acc + upd


def _softmax_init(tq):
    return jnp.full((1, tq), M_INIT, F32), jnp.zeros((V_ROWS, tq), F32)


def _softmax_finish(acc):
    return (acc[:HEAD_DIM] / acc[HEAD_DIM:HEAD_DIM + 1]).T


def _attend(npairs, qk, val, past, diag, tq, sa, sb):
    heads = range(len(sa))

    def put(dst, j):
        for hh in heads:
            dst[hh][...] = qk(hh, j)

    def advance(carry, src, j, fn):
        return tuple(_softmax_step(fn(hh, j, src[hh][...]), *carry[hh], val(hh, j)) for hh in heads)

    put(sa, 0)

    def body(jj, carry):
        j0 = 2 * jj
        put(sb, j0 + 1)
        carry = advance(carry, sa, j0, past)
        put(sa, j0 + 2)
        return advance(carry, sb, j0 + 1, past)

    carry = lax.fori_loop(0, npairs, body, tuple(_softmax_init(tq) for _ in heads))
    j0 = 2 * npairs
    put(sb, j0 + 1)
    carry = advance(carry, sa, j0, lambda hh, j, st: diag(hh, 0, st))
    carry = advance(carry, sb, j0 + 1, lambda hh, j, st: diag(hh, 1, st))
    return jnp.concatenate([_softmax_finish(acc) for _, acc in carry], axis=1)


def _score_scratch(tq, tk, nh):
    return [pltpu.VMEM((tk, tq), F32) for _ in range(2 * nh)]


def _flash_kernel(q_ref, k_ref, vt_ref, o_ref, *scores, tq, tk, nh):
    i = pl.program_id(2)
    assert tq == 2 * tk
    krow = lax.broadcasted_iota(jnp.int32, (tk, tq), 0)
    qcol = lax.broadcasted_iota(jnp.int32, (tk, tq), 1)

    def qk(hh, j):
        return _nt_dot(k_ref[0, j, :, hh * LANES:(hh + 1) * LANES], q_ref[0, :, hh * LANES:(hh + 1) * LANES])

    def val(hh, j):
        return vt_ref[0, j, hh * V_ROWS:(hh + 1) * V_ROWS, :]

    def diag(hh, d, st):
        return jnp.where(d * tk + krow <= qcol, st, NEG_INF)

    out = _attend(i, qk, val, lambda hh, j, st: st, diag, tq, scores[:nh], scores[nh:])
    o_ref[0] = out.astype(BF16)


def _flash(q, k, vt, tq, tk, nh=FLASH_HEADS):
    B, S, _ = q.shape
    nk = S // tk
    return pl.pallas_call(
        functools.partial(_flash_kernel, tq=tq, tk=tk, nh=nh), grid=(B, HEADS // nh, S // tq),
        in_specs=[pl.BlockSpec((1, tq, nh * LANES), lambda b, h, i: (b, i, h)),
                  pl.BlockSpec((1, nk, tk, nh * LANES), lambda b, h, i: (b, 0, 0, h)),
                  pl.BlockSpec((1, nk, nh * V_ROWS, tk), lambda b, h, i: (b, 0, h, 0))],
        out_specs=pl.BlockSpec((1, tq, nh * HEAD_DIM), lambda b, h, i: (b, i, h)),
        out_shape=jax.ShapeDtypeStruct((B, S, BRANCH_WIDTH), BF16),
        scratch_shapes=_score_scratch(tq, tk, nh),
        compiler_params=_cparams(("parallel", "parallel", "arbitrary")),
        name="flash",
    )(q, k, vt)


def _moba_kernel(q_ref, k_ref, vt_ref, o_ref, kmean_ref, bias_ref, qm_ref, *scores, nblk, tq, nh):
    i = pl.program_id(2)
    blk = MOBA_BLOCK
    shift = blk.bit_length() - 1
    r = tq // blk

    @pl.when(i == 0)
    def _():
        for n in range(nblk):
            kmean_ref[n:n + 1, :] = jnp.mean(k_ref[0, n].astype(F32), axis=0, keepdims=True)

    lane = lax.broadcasted_iota(jnp.int32, (tq, nh * HEAD_DIM), 1)
    blk_id = lax.broadcasted_iota(jnp.int32, (nblk, tq), 0)
    own = i * r + (lax.broadcasted_iota(jnp.int32, (nblk, tq), 1) >> shift)
    q2 = q_ref[0]
    for hh in range(nh):
        in_head = (lane >= hh * HEAD_DIM) & (lane < (hh + 1) * HEAD_DIM)
        q = jnp.where(in_head, q2, jnp.zeros_like(q2))
        qm_ref[hh] = q
        g3 = _nt_dot(jnp.concatenate([t.astype(BF16) for t in _split3(kmean_ref[...])], axis=0), q)
        g = g3[:nblk] + g3[nblk:2 * nblk] + g3[2 * nblk:]
        g = jnp.where(blk_id < own, g, NEG_INF)
        bias = jnp.full((nblk, tq), NEG_INF, F32)
        for _ in range(MOBA_TOPK):
            mx = jnp.max(g, axis=0, keepdims=True)
            first = jnp.min(jnp.where(g == mx, blk_id, nblk), axis=0, keepdims=True)
            pick = (blk_id == first) & (mx > NEG_INF)
            bias = jnp.where(pick, 0.0, bias)
            g = jnp.where(pick, NEG_INF, g)
        bias_ref[hh] = bias

    krow = lax.broadcasted_iota(jnp.int32, (blk, tq), 0)
    qcol = lax.broadcasted_iota(jnp.int32, (blk, tq), 1)

    def qk(hh, n):
        return _nt_dot(k_ref[0, n], qm_ref[hh])

    def val(hh, n):
        return vt_ref[0, n, hh * V_ROWS:(hh + 1) * V_ROWS, :]

    def past(hh, n, st):
        return st + bias_ref[hh, pl.ds(n, 1), :]

    def diag(hh, d, st):
        own_causal = ((qcol >> shift) == d) & (krow <= (qcol & (blk - 1)))
        return jnp.where(own_causal, st, past(hh, i * r + d, st))

    out = _attend(i, qk, val, past, diag, tq, scores[:nh], scores[nh:])
    o_ref[0] = out.astype(BF16)


def _moba(q, k, vt, tq, nh=MOBA_STEP_HEADS):
    B, S, _ = q.shape
    nblk = S // MOBA_BLOCK
    assert tq == 2 * MOBA_BLOCK and MOBA_BLOCK & (MOBA_BLOCK - 1) == 0
    w = nh * HEAD_DIM
    qc = COL_MQ // w
    kc = COL_MK // w
    return pl.pallas_call(
        functools.partial(_moba_kernel, nblk=nblk, tq=tq, nh=nh), grid=(B, HEADS // nh, S // tq),
        in_specs=[pl.BlockSpec((1, tq, w), lambda b, h, i: (b, i, qc + h)),
                  pl.BlockSpec((1, nblk, MOBA_BLOCK, w), lambda b, h, i: (b, 0, 0, kc + h)),
                  pl.BlockSpec((1, nblk, nh * V_ROWS, MOBA_BLOCK), lambda b, h, i: (b, 0, h, 0))],
        out_specs=pl.BlockSpec((1, tq, w), lambda b, h, i: (b, i, h)),
        out_shape=jax.ShapeDtypeStruct((B, S, BRANCH_WIDTH), BF16),
        scratch_shapes=[pltpu.VMEM((nblk, w), F32), pltpu.VMEM((nh, nblk, tq), F32),
                        pltpu.VMEM((nh, tq, w), BF16)] + _score_scratch(tq, MOBA_BLOCK, nh),
        compiler_params=_cparams(("parallel", "parallel", "arbitrary")),
        name="moba",
    )(q, k, vt)


def _merge_kernel(x_ref, ya_ref, yb_ref, yc_ref, ga_ref, gb_ref, gc_ref, wb_ref, wo_ref, o_ref):
    merged = None
    for n, (y_ref, g_ref) in enumerate(((ya_ref, ga_ref), (yb_ref, gb_ref), (yc_ref, gc_ref))):
        proj = jnp.dot(y_ref[...], wb_ref[n], preferred_element_type=F32)
        term = jax.nn.sigmoid(g_ref[...].astype(F32)) * proj
        merged = term if merged is None else merged + term
    o_ref[...] = x_ref[...] + jnp.dot(merged.astype(BF16), wo_ref[...], preferred_element_type=F32)


def _merge(x, y_mla, y_fox, y_moba, big, w_branch, w_out, tm=512):
    T = x.shape[0]
    g0 = COL_GATES // D_MODEL
    row = lambda n: pl.BlockSpec((tm, n), lambda i: (i, 0))
    gate = lambda n: pl.BlockSpec((tm, D_MODEL), lambda i: (i, g0 + n))
    return pl.pallas_call(
        _merge_kernel, grid=(T // tm,),
        in_specs=[row(D_MODEL), row(BRANCH_WIDTH), row(BRANCH_WIDTH), row(BRANCH_WIDTH),
                  gate(0), gate(1), gate(2),
                  _resident((3, BRANCH_WIDTH, D_MODEL), lambda i: (0, 0, 0)),
                  _resident((D_MODEL, D_MODEL), lambda i: (0, 0))],
        out_specs=row(D_MODEL),
        out_shape=jax.ShapeDtypeStruct((T, D_MODEL), F32),
        compiler_params=_cparams(("parallel",)),
        name="merge",
    )(x, y_mla, y_fox, y_moba, big, big, big, w_branch, w_out)


def _dense_ffn_kernel(x_ref, g_ref, wg_ref, wu_ref, wd_ref, o_ref):
    x = x_ref[...]
    h = _rms(x, g_ref[...]).astype(BF16)
    gate = jnp.dot(h, wg_ref[...], preferred_element_type=F32)
    up = jnp.dot(h, wu_ref[...], preferred_element_type=F32)
    act = (jax.nn.silu(gate) * up).astype(BF16)
    o_ref[...] = x + jnp.dot(act, wd_ref[...], preferred_element_type=F32)


def _dense_ffn(x, g, w_gu, w_down, tm=512):
    T = x.shape[0]
    return pl.pallas_call(
        _dense_ffn_kernel, grid=(T // tm,),
        in_specs=[pl.BlockSpec((tm, D_MODEL), lambda i: (i, 0)),
                  _resident((1, D_MODEL), lambda i: (0, 0)),
                  _resident((D_MODEL, DENSE_FF), lambda i: (0, 0)),
                  _resident((D_MODEL, DENSE_FF), lambda i: (0, 1)),
                  _resident((DENSE_FF, D_MODEL), lambda i: (0, 0))],
        out_specs=pl.BlockSpec((tm, D_MODEL), lambda i: (i, 0)),
        out_shape=jax.ShapeDtypeStruct((T, D_MODEL), F32),
        compiler_params=_cparams(("parallel",)),
        name="dense_ffn",
    )(x, g, w_gu, w_gu, w_down)


def _router_kernel(x_ref, g_ref, wr_ref, h_ref, r_ref):
    h = _rms(x_ref[...], g_ref[...])
    h_ref[...] = h
    logits = jnp.dot(h, wr_ref[...], precision=lax.Precision.HIGHEST, preferred_element_type=F32)
    lane = lax.broadcasted_iota(jnp.int32, logits.shape, 1)
    logits = jnp.where(lane < N_EXPERTS, logits, NEG_INF)
    m1 = jnp.max(logits, axis=-1, keepdims=True)
    i1 = jnp.min(jnp.where(logits == m1, lane, LANES), axis=-1, keepdims=True)
    rest = jnp.where(lane == i1, NEG_INF, logits)
    m2 = jnp.max(rest, axis=-1, keepdims=True)
    i2 = jnp.min(jnp.where(rest == m2, lane, LANES), axis=-1, keepdims=True)
    e2 = jnp.exp(m2 - m1)
    w1 = 1.0 / (1.0 + e2)
    w2 = e2 / (1.0 + e2)
    r_ref[...] = jnp.where(lane == 0, i1.astype(F32), jnp.where(lane == 1, i2.astype(F32),
                           jnp.where(lane == 2, w1, jnp.where(lane == 3, w2, 0.0))))


def _router(x, g, w_router_pad, tm=512):
    T = x.shape[0]
    return pl.pallas_call(
        _router_kernel, grid=(T // tm,),
        in_specs=[pl.BlockSpec((tm, D_MODEL), lambda i: (i, 0)),
                  pl.BlockSpec((1, D_MODEL), lambda i: (0, 0)),
                  pl.BlockSpec((D_MODEL, LANES), lambda i: (0, 0))],
        out_specs=[pl.BlockSpec((tm, D_MODEL), lambda i: (i, 0)), pl.BlockSpec((tm, LANES), lambda i: (i, 0))],
        out_shape=[jax.ShapeDtypeStruct((T, D_MODEL), F32), jax.ShapeDtypeStruct((T, LANES), F32)],
        compiler_params=_cparams(("parallel",)),
        name="router",
    )(x, g, w_router_pad)


GATHER_WINDOW = 128
GATHER_ROWS = 32


def _gather_rows(src, idx):
    M = idx.shape[0]
    C = src.shape[1]
    mesh = plsc.VectorSubcoreMesh(core_axis_name="core", subcore_axis_name="subcore")
    per = M // (mesh.num_cores * mesh.num_subcores)
    assert per * mesh.num_cores * mesh.num_subcores == M and per % GATHER_WINDOW == 0

    @pl.kernel(out_type=jax.ShapeDtypeStruct((M, C), src.dtype), mesh=mesh, name="gather_rows",
               scratch_types=[pltpu.VMEM((GATHER_WINDOW,), jnp.int32), pltpu.VMEM((GATHER_ROWS, C), src.dtype)])
    def gather(x_hbm, i_hbm, o_hbm, idx_v, buf):
        w = lax.axis_index("core") * mesh.num_subcores + lax.axis_index("subcore")

        @pl.loop(0, per // GATHER_WINDOW)
        def _(t):
            base = w * per + t * GATHER_WINDOW
            pltpu.sync_copy(i_hbm.at[pl.ds(base, GATHER_WINDOW)], idx_v)
            for k in range(GATHER_WINDOW // GATHER_ROWS):
                pltpu.sync_copy(x_hbm.at[idx_v.at[pl.ds(k * GATHER_ROWS, GATHER_ROWS)]], buf)
                pltpu.sync_copy(buf, o_hbm.at[pl.ds(base + k * GATHER_ROWS, GATHER_ROWS)])

    return gather(src, idx)


def _scatter_rows(src, dest0, dest1, n_rows):
    T, C = src.shape
    mesh = plsc.VectorSubcoreMesh(core_axis_name="core", subcore_axis_name="subcore")
    per = T // (mesh.num_cores * mesh.num_subcores)
    assert per * mesh.num_cores * mesh.num_subcores == T and per % GATHER_WINDOW == 0

    @pl.kernel(out_type=jax.ShapeDtypeStruct((n_rows, C), src.dtype), mesh=mesh, name="scatter_rows",
               scratch_types=[pltpu.VMEM((GATHER_WINDOW,), jnp.int32), pltpu.VMEM((GATHER_WINDOW,), jnp.int32),
                              pltpu.VMEM((GATHER_ROWS, C), src.dtype)])
    def scatter(x_hbm, d0_hbm, d1_hbm, o_hbm, i0, i1, buf):
        w = lax.axis_index("core") * mesh.num_subcores + lax.axis_index("subcore")

        @pl.loop(0, per // GATHER_WINDOW)
        def _(t):
            base = w * per + t * GATHER_WINDOW
            pltpu.sync_copy(d0_hbm.at[pl.ds(base, GATHER_WINDOW)], i0)
            pltpu.sync_copy(d1_hbm.at[pl.ds(base, GATHER_WINDOW)], i1)
            for k in range(GATHER_WINDOW // GATHER_ROWS):
                pltpu.sync_copy(x_hbm.at[pl.ds(base + k * GATHER_ROWS, GATHER_ROWS)], buf)
                pltpu.sync_copy(buf, o_hbm.at[i0.at[pl.ds(k * GATHER_ROWS, GATHER_ROWS)]])
                pltpu.sync_copy(buf, o_hbm.at[i1.at[pl.ds(k * GATHER_ROWS, GATHER_ROWS)]])

    return scatter(src, dest0, dest1)


EXPERT_SPLIT = 2


def _expert_kernel(be_ref, nused_ref, valid_ref, x_ref, wgu_ref, wd_ref, o_ref):
    b = pl.program_id(0)
    used = b < nused_ref[0]
    tf = EXPERT_FF // EXPERT_SPLIT

    @pl.when(used)
    def _():
        row = lax.broadcasted_iota(jnp.int32, (x_ref.shape[0], 1), 0)
        x = jnp.where(row < valid_ref[b], x_ref[...], 0.0).astype(BF16)
        y = None
        for f in range(EXPERT_SPLIT):
            gate = jnp.dot(x, wgu_ref[0, :, f * tf:(f + 1) * tf], preferred_element_type=F32)
            up = jnp.dot(x, wgu_ref[0, :, EXPERT_FF + f * tf:EXPERT_FF + (f + 1) * tf], preferred_element_type=F32)
            act = (jax.nn.silu(gate) * up).astype(BF16)
            part = jnp.dot(act, wd_ref[0, f * tf:(f + 1) * tf, :], preferred_element_type=F32)
            y = part if y is None else y + part
        o_ref[...] = y

    @pl.when(jnp.logical_not(used))
    def _():
        o_ref[...] = jnp.zeros_like(o_ref)


def _experts(x_rows, block_e, n_used, valid, w_gu, w_down, bm):
    n_rows = x_rows.shape[0]
    grid_spec = pltpu.PrefetchScalarGridSpec(
        num_scalar_prefetch=3, grid=(n_rows // bm,),
        in_specs=[pl.BlockSpec((bm, D_MODEL), lambda b, be, nu, va: (b, 0)),
                  _resident((1, D_MODEL, 2 * EXPERT_FF), lambda b, be, nu, va: (be[b], 0, 0)),
                  _resident((1, EXPERT_FF, D_MODEL), lambda b, be, nu, va: (be[b], 0, 0))],
        out_specs=pl.BlockSpec((bm, D_MODEL), lambda b, be, nu, va: (b, 0)))
    return pl.pallas_call(
        _expert_kernel, grid_spec=grid_spec,
        out_shape=jax.ShapeDtypeStruct((n_rows, D_MODEL), F32),
        compiler_params=_cparams(("arbitrary",), EXPERT_VMEM_LIMIT),
        name="experts",
    )(block_e, n_used, valid, x_rows, w_gu, w_down)


def _combine_kernel(x_ref, y0_ref, y1_ref, r_ref, g_ref, o_ref):
    r = r_ref[...]
    x = x_ref[...] + r[:, 2:3] * y0_ref[...] + r[:, 3:4] * y1_ref[...]
    o_ref[...] = _rms(x, g_ref[...])


def _combine(x, y_slots, route, g_final, tm=512):
    T = x.shape[0]
    return pl.pallas_call(
        _combine_kernel, grid=(T // tm,),
        in_specs=[pl.BlockSpec((tm, D_MODEL), lambda i: (i, 0)),
                  pl.BlockSpec((tm, D_MODEL), lambda i: (i, 0)),
                  pl.BlockSpec((tm, D_MODEL), lambda i: (T // tm + i, 0)),
                  pl.BlockSpec((tm, LANES), lambda i: (i, 0)),
                  pl.BlockSpec((1, D_MODEL), lambda i: (0, 0))],
        out_specs=pl.BlockSpec((tm, D_MODEL), lambda i: (i, 0)),
        out_shape=jax.ShapeDtypeStruct((T, D_MODEL), F32),
        compiler_params=_cparams(("parallel",)),
        name="combine",
    )(x, y_slots, y_slots, route, g_final)


def _pad_cols(w, n):
    return jnp.pad(w, ((0, 0), (0, n - w.shape[1])))


def _split_in_weights(w_in):
    sizes = [MLA_Q_LORA, MLA_KV_LORA, MLA_ROPE_DIM, BRANCH_WIDTH, BRANCH_WIDTH, BRANCH_WIDTH, HEADS,
             BRANCH_WIDTH, BRANCH_WIDTH, BRANCH_WIDTH, 3 * D_MODEL]
    pts = np.cumsum(sizes)[:-1]
    c_q, c_kv, k_pe, fq, fk, fv, fl, mq, mk, mv, gates = jnp.split(w_in, pts, axis=1)
    zeros = lambda n: jnp.zeros((D_MODEL, n), w_in.dtype)
    pe_tile = jnp.concatenate([zeros(HEAD_DIM), k_pe, zeros(LANES - HEAD_DIM - MLA_ROPE_DIM)], axis=1)
    w_small = jnp.concatenate([c_q, c_kv, pe_tile, _pad_cols(fl, LANES)], axis=1)
    w_big = jnp.concatenate([fq, mq, fk, mk, fv, mv, gates], axis=1)
    return w_small.astype(BF16), w_big.astype(BF16)


def _mla_up_weights(w_uq, w_ukv):
    dq = HEAD_DIM + MLA_ROPE_DIM
    wq = w_uq.reshape(MLA_Q_LORA, HEADS, dq)
    wq = jnp.pad(wq, ((0, 0), (0, 0), (0, LANES - dq))).reshape(MLA_Q_LORA, HEADS * LANES)
    wkv = w_ukv.reshape(MLA_KV_LORA, HEADS, 2 * HEAD_DIM)
    wk = jnp.pad(wkv[:, :, :HEAD_DIM], ((0, 0), (0, 0), (0, LANES - HEAD_DIM))).reshape(MLA_KV_LORA, HEADS * LANES)
    wv = wkv[:, :, HEAD_DIM:].reshape(MLA_KV_LORA, BRANCH_WIDTH)
    return wq.astype(BF16), wk.astype(BF16), wv.astype(BF16)


def _chunk_rows(a, B, S, t):
    return a.reshape(B, S // t, t, a.shape[-1])


ATT_TQ = 512
ATT_TK = KV_CHUNK
MOBA_TQ = 512
MOE_BM = 512


def _token_mixers(x, B, S, g_mix, w_in, g_q_lat, g_kv_lat, w_uq, w_ukv, b_forget, w_branch, w_out,
                  tabs_mla, tabs_moba):
    T = B * S
    w_small, w_big = _split_in_weights(w_in)
    wq, wk, wv = _mla_up_weights(w_uq, w_ukv)
    g = g_mix[None, :]

    big, vt_fox, vt_moba = _inproj(x, g, w_big, tabs_moba, B, S)
    q_mla, k_mla, vt_mla, f_logit = _mla_prep(
        x, g, w_small, g_q_lat[None, :], g_kv_lat[None, :], wq, wk, wv, tabs_mla,
        (HEAD_DIM + MLA_ROPE_DIM) ** -0.5 * LOG2E, B, S)

    y_mla = _flash(q_mla.reshape(B, S, -1), _chunk_rows(k_mla, B, S, ATT_TK), vt_mla, ATT_TQ, ATT_TK)

    c = _cumlogf(f_logit.reshape(B, S, HEADS).transpose(0, 2, 1), b_forget[:, None])
    c_rows = c.transpose(0, 2, 1).reshape(T, HEADS)
    q_fox, k_fox = _fox_prep(big, c_rows)
    y_fox = _flash(q_fox.reshape(B, S, -1), _chunk_rows(k_fox, B, S, ATT_TK), vt_fox, ATT_TQ, ATT_TK)

    y_moba = _moba(big.reshape(B, S, -1), _chunk_rows(big, B, S, MOBA_BLOCK), vt_moba, MOBA_TQ)

    return _merge(x, y_mla.reshape(T, -1), y_fox.reshape(T, -1), y_moba.reshape(T, -1), big,
                  w_branch.astype(BF16), w_out.astype(BF16))


def _moe(x, g_ffn, w_router, w_gu, w_down, g_final):
    T = x.shape[0]
    A = 2 * T
    bm = MOE_BM
    h, route = _router(x, g_ffn[None, :], _pad_cols(w_router, LANES))
    top_e = route[:, :2].astype(jnp.int32).reshape(A)
    onehot = (top_e[:, None] == jnp.arange(N_EXPERTS)[None, :]).astype(jnp.int32)
    csum = jnp.cumsum(onehot, axis=0)
    counts = csum[-1]
    rank = jnp.sum((csum - onehot) * onehot, axis=1)
    padded = (counts + bm - 1) // bm * bm
    pad_end = jnp.cumsum(padded)
    dest = (pad_end - padded)[top_e] + rank
    n_rows = (A // bm + N_EXPERTS) * bm
    block_start = jnp.arange(n_rows // bm) * bm
    block_e = jnp.minimum(jnp.searchsorted(pad_end, block_start, side="right"), N_EXPERTS - 1).astype(jnp.int32)
    n_used = (pad_end[-1:] // bm).astype(jnp.int32)
    valid = jnp.clip((pad_end - padded + counts)[block_e] - block_start, 0, bm).astype(jnp.int32)
    dest_slots = dest.astype(jnp.int32).reshape(T, 2).T

    x_rows = _scatter_rows(h, dest_slots[0], dest_slots[1], n_rows)
    y_rows = _experts(x_rows, block_e, n_used, valid, w_gu.astype(BF16), w_down.astype(BF16), bm)
    y_slots = _gather_rows(y_rows, dest_slots.reshape(A))
    return _combine(x, y_slots, route, g_final[None, :])


def kernel(x, positions, g_mix, w_in, g_q_lat, g_kv_lat, w_uq, w_ukv, b_forget, w_branch, w_out, g_ffn,
           w_dense_gu, w_dense_down, w_router, w_exp_gu, w_exp_down, g_final):
    B, S, D = x.shape
    T = B * S
    depth = g_mix.shape[0]
    assert depth == 2 and D == D_MODEL and S % MOBA_BLOCK == 0 and KV_CHUNK == MOBA_BLOCK
    pos = positions.reshape(T, 1).astype(jnp.int32)
    tabs_mla = _rope_tables(pos, *_rope_patterns(LANES, HEAD_DIM, MLA_ROPE_DIM // 2))
    tabs_moba = _rope_tables(pos, *_rope_patterns(HEAD_DIM, 0, HEAD_DIM // 2))
    x = x.reshape(T, D)
    for l in range(depth):
        x = _token_mixers(x, B, S, g_mix[l], w_in[l], g_q_lat[l], g_kv_lat[l], w_uq[l], w_ukv[l],
                          b_forget[l], w_branch[l], w_out[l], tabs_mla, tabs_moba)
        if l % 2 == 0:
            x = _dense_ffn(x, g_ffn[l][None, :], w_dense_gu[l // 2].astype(BF16),
                           w_dense_down[l // 2].astype(BF16))
        else:
            x = _moe(x, g_ffn[l], w_router[l // 2], w_exp_gu[l // 2], w_exp_down[l // 2], g_final)
    return x.reshape(B, S, D)
```

```python
import functools
import math

import jax
import jax.numpy as jnp
import numpy as np
from jax import lax
from jax.experimental import pallas as pl
from jax.experimental.pallas import tpu as pltpu
from jax.experimental.pallas import tpu_sc as plsc

F32 = jnp.float32
BF16 = jnp.bfloat16
NEG_INF = float("-inf")
M_INIT = -1e30
LOG2E = math.log2(math.e)

D_MODEL = 1024
RMS_EPS = 1e-6
ROPE_THETA = 10000.0
HEADS = 8
HEAD_DIM = 64
V_ROWS = 80
Q_SCALE = HEAD_DIM ** -0.5 * math.log2(math.e)
MLA_Q_LORA = 256
MLA_KV_LORA = 128
MLA_ROPE_DIM = 32
BRANCH_WIDTH = HEADS * HEAD_DIM
MOBA_BLOCK = 256
MOBA_TOPK = 3
DENSE_FF = 2816
N_EXPERTS = 8
EXPERT_FF = 3584

LANES = 128
VMEM_LIMIT = 48 * 1024 * 1024
EXPERT_VMEM_LIMIT = 62 * 1024 * 1024

COL_FQ, COL_MQ, COL_FK, COL_MK, COL_GATES = (n * BRANCH_WIDTH for n in range(5))
BIG_COLS = COL_GATES + 3 * D_MODEL
BIG_TN = 2 * BRANCH_WIDTH
W_TILES = BIG_COLS // BIG_TN + 1
V_TILE = 2
KV_CHUNK = 256
FLASH_HEADS = 4
MOBA_STEP_HEADS = 2
SMALL_COLS = MLA_Q_LORA + MLA_KV_LORA + 2 * LANES


def _cparams(sem, vmem_limit=VMEM_LIMIT):
    return pltpu.CompilerParams(dimension_semantics=sem, vmem_limit_bytes=vmem_limit)


def _resident(shape, index_map):
    return pl.BlockSpec(shape, index_map, pipeline_mode=pl.Buffered(1))


def _rms(x, g):
    return x * lax.rsqrt(jnp.mean(x * x, axis=-1, keepdims=True) + RMS_EPS) * g


def _rope_table_kernel(pos_ref, f_ref, mc_ref, m1_ref, m2_ref, c_ref, s1_ref, s2_ref):
    ang = pos_ref[...].astype(F32) * f_ref[...]
    cos = jnp.cos(ang)
    sin = jnp.sin(ang)
    mc = mc_ref[...]
    c_ref[...] = cos * mc + (1.0 - mc)
    s1_ref[...] = sin * m1_ref[...]
    s2_ref[...] = sin * m2_ref[...]


def _rope_tables(pos, freq, mc, m1, m2, tm=1024):
    T = pos.shape[0]
    row = pl.BlockSpec((tm, 1), lambda i: (i, 0))
    pat = pl.BlockSpec((1, LANES), lambda i: (0, 0))
    out = pl.BlockSpec((tm, LANES), lambda i: (i, 0))
    shp = jax.ShapeDtypeStruct((T, LANES), F32)
    return pl.pallas_call(
        _rope_table_kernel, grid=(T // tm,),
        in_specs=[row, pat, pat, pat, pat], out_specs=[out, out, out],
        out_shape=[shp, shp, shp], compiler_params=_cparams(("parallel",)),
        name="rope_tables",
    )(pos, freq, mc, m1, m2)


def _rope_patterns(group, x1_lo, half):
    d = 2 * half
    inv_freq = jnp.exp(-math.log(ROPE_THETA) * jnp.arange(half, dtype=F32) * 2.0 / d)
    lane = np.arange(LANES) % group
    in_x1 = (lane >= x1_lo) & (lane < x1_lo + half)
    in_x2 = (lane >= x1_lo + half) & (lane < x1_lo + d)
    k = np.where(in_x1, lane - x1_lo, np.where(in_x2, lane - x1_lo - half, 0))
    freq = jnp.where(jnp.asarray(in_x1 | in_x2), inv_freq[k], 0.0)[None, :].astype(F32)
    mc = jnp.asarray((in_x1 | in_x2).astype(np.float32))[None, :]
    m1 = jnp.asarray(-(in_x1.astype(np.float32)))[None, :]
    m2 = jnp.asarray(in_x2.astype(np.float32))[None, :]
    return freq, mc, m1, m2


def _apply_rope(x, c, s1, s2, half):
    n = x.shape[-1]
    reps = n // LANES
    c, s1, s2 = (jnp.tile(t, (1, reps)) if reps > 1 else t for t in (c, s1, s2))
    return x * c + pltpu.roll(x, n - half, 1) * s1 + pltpu.roll(x, half, 1) * s2


def _store_value_tiles(vt_ref, a):
    row = lax.broadcasted_iota(jnp.int32, (V_ROWS - HEAD_DIM, KV_CHUNK), 0)
    pad = jnp.where(row == 0, 1.0, 0.0).astype(BF16)
    for c in range(a.shape[0] // KV_CHUNK):
        at = a[c * KV_CHUNK:(c + 1) * KV_CHUNK, :].T.astype(BF16)
        for h in range(HEADS):
            vt_ref[0, c, h * V_ROWS:h * V_ROWS + HEAD_DIM, :] = at[h * HEAD_DIM:(h + 1) * HEAD_DIM]
            vt_ref[0, c, h * V_ROWS + HEAD_DIM:(h + 1) * V_ROWS, :] = pad


def _value_tile_spec(tm, S, grid_rank):
    per = S // tm
    imap = (lambda i: (i // per, i % per, 0, 0)) if grid_rank == 1 else (lambda i, j: (i // per, i % per, 0, 0))
    return pl.BlockSpec((1, tm // KV_CHUNK, HEADS * V_ROWS, KV_CHUNK), imap)


def _value_tile_shape(B, S):
    return jax.ShapeDtypeStruct((B, S // KV_CHUNK, HEADS * V_ROWS, KV_CHUNK), BF16)


def _inproj_kernel(x_ref, g_ref, w_ref, c_ref, s1_ref, s2_ref, o_ref, vf_ref, vm_ref):
    h = _rms(x_ref[...], g_ref[...]).astype(BF16)
    c, s1, s2 = c_ref[...], s1_ref[...], s2_ref[...]

    def tile(t):
        return jnp.dot(h, w_ref[:, t * BIG_TN:(t + 1) * BIG_TN], preferred_element_type=F32)

    for t in range(V_TILE):
        a = tile(t) * Q_SCALE if t == 0 else tile(t)
        o_ref[:, t * BIG_TN:t * BIG_TN + BRANCH_WIDTH] = a[:, :BRANCH_WIDTH].astype(BF16)
        o_ref[:, t * BIG_TN + BRANCH_WIDTH:(t + 1) * BIG_TN] = _apply_rope(
            a[:, BRANCH_WIDTH:], c, s1, s2, HEAD_DIM // 2).astype(BF16)
    a = tile(V_TILE)
    _store_value_tiles(vf_ref, a[:, :BRANCH_WIDTH])
    _store_value_tiles(vm_ref, a[:, BRANCH_WIDTH:])
    for t in range(V_TILE + 1, W_TILES):
        o_ref[:, (t - 1) * BIG_TN:t * BIG_TN] = tile(t).astype(BF16)


def _inproj(x, g, w_big, tabs, B, S, tm=512):
    T = x.shape[0]
    tab = pl.BlockSpec((tm, LANES), lambda i: (i, 0))
    vspec = _value_tile_spec(tm, S, 1)
    return pl.pallas_call(
        _inproj_kernel, grid=(T // tm,),
        in_specs=[pl.BlockSpec((tm, D_MODEL), lambda i: (i, 0)),
                  _resident((1, D_MODEL), lambda i: (0, 0)),
                  _resident((D_MODEL, W_TILES * BIG_TN), lambda i: (0, 0)),
                  tab, tab, tab],
        out_specs=[pl.BlockSpec((tm, BIG_COLS), lambda i: (i, 0)), vspec, vspec],
        out_shape=[jax.ShapeDtypeStruct((T, BIG_COLS), BF16), _value_tile_shape(B, S), _value_tile_shape(B, S)],
        compiler_params=_cparams(("parallel",)),
        name="inproj",
    )(x, g, w_big, *tabs)


def _mla_prep_kernel(x_ref, g_ref, ws_ref, gq_ref, gkv_ref, wq_ref, wk_ref, wv_ref,
                     c_ref, s1_ref, s2_ref, q_ref, k_ref, v_ref, fl_ref, *, scale):
    h = _rms(x_ref[...], g_ref[...]).astype(BF16)
    small = jnp.dot(h, ws_ref[...], preferred_element_type=F32)
    c_q = small[:, :MLA_Q_LORA]
    c_kv = small[:, MLA_Q_LORA:MLA_Q_LORA + MLA_KV_LORA]
    k_pe = small[:, MLA_Q_LORA + MLA_KV_LORA:MLA_Q_LORA + MLA_KV_LORA + LANES]
    fl_ref[...] = small[:, SMALL_COLS - LANES:SMALL_COLS - LANES + HEADS]
    c, s1, s2 = c_ref[...], s1_ref[...], s2_ref[...]
    half = MLA_ROPE_DIM // 2
    qn = _rms(c_q, gq_ref[...]).astype(BF16)
    q = jnp.dot(qn, wq_ref[...], preferred_element_type=F32) * scale
    q_ref[...] = _apply_rope(q, c, s1, s2, half).astype(BF16)
    kvn = _rms(c_kv, gkv_ref[...]).astype(BF16)
    k_nope = jnp.dot(kvn, wk_ref[...], preferred_element_type=F32)
    k_rot = _apply_rope(k_pe, c, s1, s2, half)
    k_ref[...] = (k_nope + jnp.tile(k_rot, (1, HEADS))).astype(BF16)
    _store_value_tiles(v_ref, jnp.dot(kvn, wv_ref[...], preferred_element_type=F32))


def _mla_prep(x, g, w_small, g_q, g_kv, wq, wk, wv, tabs, scale, B, S, tm=512):
    T = x.shape[0]
    full = lambda shape: pl.BlockSpec(shape, lambda i: (0,) * len(shape))
    row = lambda n: pl.BlockSpec((tm, n), lambda i: (i, 0))
    qk = HEADS * LANES
    return pl.pallas_call(
        functools.partial(_mla_prep_kernel, scale=scale), grid=(T // tm,),
        in_specs=[row(D_MODEL), full((1, D_MODEL)), full((D_MODEL, SMALL_COLS)),
                  full((1, MLA_Q_LORA)), full((1, MLA_KV_LORA)),
                  full((MLA_Q_LORA, qk)), full((MLA_KV_LORA, qk)), full((MLA_KV_LORA, BRANCH_WIDTH)),
                  row(LANES), row(LANES), row(LANES)],
        out_specs=[row(qk), row(qk), _value_tile_spec(tm, S, 1), row(HEADS)],
        out_shape=[jax.ShapeDtypeStruct((T, qk), BF16), jax.ShapeDtypeStruct((T, qk), BF16),
                   _value_tile_shape(B, S), jax.ShapeDtypeStruct((T, HEADS), F32)],
        compiler_params=_cparams(("parallel",)),
        name="mla_prep",
    )(x, g, w_small, g_q, g_kv, wq, wk, wv, *tabs)


def _cumlogf_kernel(fl_ref, b_ref, c_ref):
    z = fl_ref[0] + b_ref[...]
    x = jnp.minimum(z, 0.0) - jnp.log1p(jnp.exp(-jnp.abs(z)))
    n = x.shape[-1]
    lane = lax.broadcasted_iota(jnp.int32, x.shape, 1)
    d = 1
    while d < n:
        x = x + jnp.where(lane >= d, pltpu.roll(x, d, 1), 0.0)
        d *= 2
    c_ref[0] = x


def _cumlogf(fl_t, b_col):
    B, H, S = fl_t.shape
    return pl.pallas_call(
        _cumlogf_kernel, grid=(B,),
        in_specs=[pl.BlockSpec((1, H, S), lambda b: (b, 0, 0)), pl.BlockSpec((H, 1), lambda b: (0, 0))],
        out_specs=pl.BlockSpec((1, H, S), lambda b: (b, 0, 0)),
        out_shape=jax.ShapeDtypeStruct((B, H, S), F32),
        compiler_params=_cparams(("parallel",)),
        name="cumlogf",
    )(fl_t, b_col)


def _split3(c):
    hi = c.astype(BF16)
    r = c - hi.astype(F32)
    mid = r.astype(BF16)
    lo = (r - mid.astype(F32)).astype(BF16)
    return hi.astype(F32), mid.astype(F32), lo.astype(F32)


def _fox_prep_kernel(q_ref, k_ref, c_ref, qo_ref, ko_ref):
    tm = q_ref.shape[0]
    lane = lax.broadcasted_iota(jnp.int32, (tm, LANES), 1)
    c = c_ref[...] * LOG2E
    for hp in range(HEADS // 2):
        q2 = q_ref[:, hp * LANES:(hp + 1) * LANES].astype(F32)
        k2 = k_ref[:, hp * LANES:(hp + 1) * LANES].astype(F32)
        for hh in range(2):
            h = 2 * hp + hh
            terms = [jnp.broadcast_to(t, (tm, LANES)) for t in _split3(c[:, h:h + 1])]
            qh = q2 if hh == 0 else pltpu.roll(q2, HEAD_DIM, 1)
            kh = k2 if hh == 0 else pltpu.roll(k2, HEAD_DIM, 1)
            q_aug = jnp.where(lane < HEAD_DIM + 3, 1.0, 0.0)
            k_aug = jnp.where((lane >= HEAD_DIM + 3) & (lane < HEAD_DIM + 6), 1.0, 0.0)
            for n, t in enumerate(terms):
                q_aug = jnp.where(lane == HEAD_DIM + 3 + n, t, q_aug)
                k_aug = jnp.where(lane == HEAD_DIM + n, -t, k_aug)
            q_aug = jnp.where(lane < HEAD_DIM, qh, q_aug)
            k_aug = jnp.where(lane < HEAD_DIM, kh, k_aug)
            qo_ref[:, h * LANES:(h + 1) * LANES] = q_aug.astype(BF16)
            ko_ref[:, h * LANES:(h + 1) * LANES] = k_aug.astype(BF16)


def _fox_prep(big, c_rows, tm=512):
    T = big.shape[0]
    qk = HEADS * LANES
    return pl.pallas_call(
        _fox_prep_kernel, grid=(T // tm,),
        in_specs=[pl.BlockSpec((tm, BRANCH_WIDTH), lambda i: (i, COL_FQ // BRANCH_WIDTH)),
                  pl.BlockSpec((tm, BRANCH_WIDTH), lambda i: (i, COL_FK // BRANCH_WIDTH)),
                  pl.BlockSpec((tm, HEADS), lambda i: (i, 0))],
        out_specs=[pl.BlockSpec((tm, qk), lambda i: (i, 0)), pl.BlockSpec((tm, qk), lambda i: (i, 0))],
        out_shape=[jax.ShapeDtypeStruct((T, qk), BF16), jax.ShapeDtypeStruct((T, qk), BF16)],
        compiler_params=_cparams(("parallel",)),
        name="fox_prep",
    )(big, big, c_rows)


def _nt_dot(a, b):
    return lax.dot_general(a, b, (((1,), (1,)), ((), ())), preferred_element_type=F32)


def _softmax_step(st, m, acc, vt):
    m_new = jnp.maximum(m, jnp.max(st, axis=0, keepdims=True))
    upd = jnp.dot(vt, jnp.exp2(st - m_new).astype(BF16), preferred_element_type=F32)
    return m_new, jnp.exp2(m - m_new) * acc + upd


def _softmax_init(tq):
    return jnp.full((1, tq), M_INIT, F32), jnp.zeros((V_ROWS, tq), F32)


def _softmax_finish(acc):
    return (acc[:HEAD_DIM] / acc[HEAD_DIM:HEAD_DIM + 1]).T


def _attend(npairs, qk, val, past, diag, tq, sa, sb):
    heads = range(len(sa))

    def put(dst, j):
        for hh in heads:
            dst[hh][...] = qk(hh, j)

    def advance(carry, src, j, fn):
        return tuple(_softmax_step(fn(hh, j, src[hh][...]), *carry[hh], val(hh, j)) for hh in heads)

    put(sa, 0)

    def body(jj, carry):
        j0 = 2 * jj
        put(sb, j0 + 1)
        carry = advance(carry, sa, j0, past)
        put(sa, j0 + 2)
        return advance(carry, sb, j0 + 1, past)

    carry = lax.fori_loop(0, npairs, body, tuple(_softmax_init(tq) for _ in heads))
    j0 = 2 * npairs
    put(sb, j0 + 1)
    carry = advance(carry, sa, j0, lambda hh, j, st: diag(hh, 0, st))
    carry = advance(carry, sb, j0 + 1, lambda hh, j, st: diag(hh, 1, st))
    return jnp.concatenate([_softmax_finish(acc) for _, acc in carry], axis=1)


def _score_scratch(tq, tk, nh):
    return [pltpu.VMEM((tk, tq), F32) for _ in range(2 * nh)]


def _flash_kernel(q_ref, k_ref, vt_ref, o_ref, *scores, tq, tk, nh):
    i = pl.program_id(2)
    assert tq == 2 * tk
    krow = lax.broadcasted_iota(jnp.int32, (tk, tq), 0)
    qcol = lax.broadcasted_iota(jnp.int32, (tk, tq), 1)

    def qk(hh, j):
        return _nt_dot(k_ref[0, j, :, hh * LANES:(hh + 1) * LANES], q_ref[0, :, hh * LANES:(hh + 1) * LANES])

    def val(hh, j):
        return vt_ref[0, j, hh * V_ROWS:(hh + 1) * V_ROWS, :]

    def diag(hh, d, st):
        return jnp.where(d * tk + krow <= qcol, st, NEG_INF)

    out = _attend(i, qk, val, lambda hh, j, st: st, diag, tq, scores[:nh], scores[nh:])
    o_ref[0] = out.astype(BF16)


def _flash(q, k, vt, tq, tk, nh=FLASH_HEADS):
    B, S, _ = q.shape
    nk = S // tk
    return pl.pallas_call(
        functools.partial(_flash_kernel, tq=tq, tk=tk, nh=nh), grid=(B, HEADS // nh, S // tq),
        in_specs=[pl.BlockSpec((1, tq, nh * LANES), lambda b, h, i: (b, i, h)),
                  pl.BlockSpec((1, nk, tk, nh * LANES), lambda b, h, i: (b, 0, 0, h)),
                  pl.BlockSpec((1, nk, nh * V_ROWS, tk), lambda b, h, i: (b, 0, h, 0))],
        out_specs=pl.BlockSpec((1, tq, nh * HEAD_DIM), lambda b, h, i: (b, i, h)),
        out_shape=jax.ShapeDtypeStruct((B, S, BRANCH_WIDTH), BF16),
        scratch_shapes=_score_scratch(tq, tk, nh),
        compiler_params=_cparams(("parallel", "parallel", "arbitrary")),
        name="flash",
    )(q, k, vt)


def _moba_kernel(q_ref, k_ref, vt_ref, o_ref, kmean_ref, bias_ref, qm_ref, *scores, nblk, tq, nh):
    i = pl.program_id(2)
    blk = MOBA_BLOCK
    shift = blk.bit_length() - 1
    r = tq // blk

    @pl.when(i == 0)
    def _():
        for n in range(nblk):
            kmean_ref[n:n + 1, :] = jnp.mean(k_ref[0, n].astype(F32), axis=0, keepdims=True)

    lane = lax.broadcasted_iota(jnp.int32, (tq, nh * HEAD_DIM), 1)
    blk_id = lax.broadcasted_iota(jnp.int32, (nblk, tq), 0)
    own = i * r + (lax.broadcasted_iota(jnp.int32, (nblk, tq), 1) >> shift)
    q2 = q_ref[0]
    for hh in range(nh):
        in_head = (lane >= hh * HEAD_DIM) & (lane < (hh + 1) * HEAD_DIM)
        q = jnp.where(in_head, q2, jnp.zeros_like(q2))
        qm_ref[hh] = q
        g3 = _nt_dot(jnp.concatenate([t.astype(BF16) for t in _split3(kmean_ref[...])], axis=0), q)
        g = g3[:nblk] + g3[nblk:2 * nblk] + g3[2 * nblk:]
        g = jnp.where(blk_id < own, g, NEG_INF)
        bias = jnp.full((nblk, tq), NEG_INF, F32)
        for _ in range(MOBA_TOPK):
            mx = jnp.max(g, axis=0, keepdims=True)
            first = jnp.min(jnp.where(g == mx, blk_id, nblk), axis=0, keepdims=True)
            pick = (blk_id == first) & (mx > NEG_INF)
            bias = jnp.where(pick, 0.0, bias)
            g = jnp.where(pick, NEG_INF, g)
        bias_ref[hh] = bias

    krow = lax.broadcasted_iota(jnp.int32, (blk, tq), 0)
    qcol = lax.broadcasted_iota(jnp.int32, (blk, tq), 1)

    def qk(hh, n):
        return _nt_dot(k_ref[0, n], qm_ref[hh])

    def val(hh, n):
        return vt_ref[0, n, hh * V_ROWS:(hh + 1) * V_ROWS, :]

    def past(hh, n, st):
        return st + bias_ref[hh, pl.ds(n, 1), :]

    def diag(hh, d, st):
        own_causal = ((qcol >> shift) == d) & (krow <= (qcol & (blk - 1)))
        return jnp.where(own_causal, st, past(hh, i * r + d, st))

    out = _attend(i, qk, val, past, diag, tq, scores[:nh], scores[nh:])
    o_ref[0] = out.astype(BF16)


def _moba(q, k, vt, tq, nh=MOBA_STEP_HEADS):
    B, S, _ = q.shape
    nblk = S // MOBA_BLOCK
    assert tq == 2 * MOBA_BLOCK and MOBA_BLOCK & (MOBA_BLOCK - 1) == 0
    w = nh * HEAD_DIM
    qc = COL_MQ // w
    kc = COL_MK // w
    return pl.pallas_call(
        functools.partial(_moba_kernel, nblk=nblk, tq=tq, nh=nh), grid=(B, HEADS // nh, S // tq),
        in_specs=[pl.BlockSpec((1, tq, w), lambda b, h, i: (b, i, qc + h)),
                  pl.BlockSpec((1, nblk, MOBA_BLOCK, w), lambda b, h, i: (b, 0, 0, kc + h)),
                  pl.BlockSpec((1, nblk, nh * V_ROWS, MOBA_BLOCK), lambda b, h, i: (b, 0, h, 0))],
        out_specs=pl.BlockSpec((1, tq, w), lambda b, h, i: (b, i, h)),
        out_shape=jax.ShapeDtypeStruct((B, S, BRANCH_WIDTH), BF16),
        scratch_shapes=[pltpu.VMEM((nblk, w), F32), pltpu.VMEM((nh, nblk, tq), F32),
                        pltpu.VMEM((nh, tq, w), BF16)] + _score_scratch(tq, MOBA_BLOCK, nh),
        compiler_params=_cparams(("parallel", "parallel", "arbitrary")),
        name="moba",
    )(q, k, vt)


def _merge_kernel(x_ref, ya_ref, yb_ref, yc_ref, ga_ref, gb_ref, gc_ref, wb_ref, wo_ref, o_ref):
    merged = None
    for n, (y_ref, g_ref) in enumerate(((ya_ref, ga_ref), (yb_ref, gb_ref), (yc_ref, gc_ref))):
        proj = jnp.dot(y_ref[...], wb_ref[n], preferred_element_type=F32)
        term = jax.nn.sigmoid(g_ref[...].astype(F32)) * proj
        merged = term if merged is None else merged + term
    o_ref[...] = x_ref[...] + jnp.dot(merged.astype(BF16), wo_ref[...], preferred_element_type=F32)


def _merge(x, y_mla, y_fox, y_moba, big, w_branch, w_out, tm=512):
    T = x.shape[0]
    g0 = COL_GATES // D_MODEL
    row = lambda n: pl.BlockSpec((tm, n), lambda i: (i, 0))
    gate = lambda n: pl.BlockSpec((tm, D_MODEL), lambda i: (i, g0 + n))
    return pl.pallas_call(
        _merge_kernel, grid=(T // tm,),
        in_specs=[row(D_MODEL), row(BRANCH_WIDTH), row(BRANCH_WIDTH), row(BRANCH_WIDTH),
                  gate(0), gate(1), gate(2),
                  _resident((3, BRANCH_WIDTH, D_MODEL), lambda i: (0, 0, 0)),
                  _resident((D_MODEL, D_MODEL), lambda i: (0, 0))],
        out_specs=row(D_MODEL),
        out_shape=jax.ShapeDtypeStruct((T, D_MODEL), F32),
        compiler_params=_cparams(("parallel",)),
        name="merge",
    )(x, y_mla, y_fox, y_moba, big, big, big, w_branch, w_out)


def _dense_ffn_kernel(x_ref, g_ref, wg_ref, wu_ref, wd_ref, o_ref):
    x = x_ref[...]
    h = _rms(x, g_ref[...]).astype(BF16)
    gate = jnp.dot(h, wg_ref[...], preferred_element_type=F32)
    up = jnp.dot(h, wu_ref[...], preferred_element_type=F32)
    act = (jax.nn.silu(gate) * up).astype(BF16)
    o_ref[...] = x + jnp.dot(act, wd_ref[...], preferred_element_type=F32)


def _dense_ffn(x, g, w_gu, w_down, tm=512):
    T = x.shape[0]
    return pl.pallas_call(
        _dense_ffn_kernel, grid=(T // tm,),
        in_specs=[pl.BlockSpec((tm, D_MODEL), lambda i: (i, 0)),
                  _resident((1, D_MODEL), lambda i: (0, 0)),
                  _resident((D_MODEL, DENSE_FF), lambda i: (0, 0)),
                  _resident((D_MODEL, DENSE_FF), lambda i: (0, 1)),
                  _resident((DENSE_FF, D_MODEL), lambda i: (0, 0))],
        out_specs=pl.BlockSpec((tm, D_MODEL), lambda i: (i, 0)),
        out_shape=jax.ShapeDtypeStruct((T, D_MODEL), F32),
        compiler_params=_cparams(("parallel",)),
        name="dense_ffn",
    )(x, g, w_gu, w_gu, w_down)


def _router_kernel(x_ref, g_ref, wr_ref, h_ref, r_ref):
    h = _rms(x_ref[...], g_ref[...])
    h_ref[...] = h
    logits = jnp.dot(h, wr_ref[...], precision=lax.Precision.HIGHEST, preferred_element_type=F32)
    lane = lax.broadcasted_iota(jnp.int32, logits.shape, 1)
    logits = jnp.where(lane < N_EXPERTS, logits, NEG_INF)
    m1 = jnp.max(logits, axis=-1, keepdims=True)
    i1 = jnp.min(jnp.where(logits == m1, lane, LANES), axis=-1, keepdims=True)
    rest = jnp.where(lane == i1, NEG_INF, logits)
    m2 = jnp.max(rest, axis=-1, keepdims=True)
    i2 = jnp.min(jnp.where(rest == m2, lane, LANES), axis=-1, keepdims=True)
    e2 = jnp.exp(m2 - m1)
    w1 = 1.0 / (1.0 + e2)
    w2 = e2 / (1.0 + e2)
    r_ref[...] = jnp.where(lane == 0, i1.astype(F32), jnp.where(lane == 1, i2.astype(F32),
                           jnp.where(lane == 2, w1, jnp.where(lane == 3, w2, 0.0))))


def _router(x, g, w_router_pad, tm=512):
    T = x.shape[0]
    return pl.pallas_call(
        _router_kernel, grid=(T // tm,),
        in_specs=[pl.BlockSpec((tm, D_MODEL), lambda i: (i, 0)),
                  pl.BlockSpec((1, D_MODEL), lambda i: (0, 0)),
                  pl.BlockSpec((D_MODEL, LANES), lambda i: (0, 0))],
        out_specs=[pl.BlockSpec((tm, D_MODEL), lambda i: (i, 0)), pl.BlockSpec((tm, LANES), lambda i: (i, 0))],
        out_shape=[jax.ShapeDtypeStruct((T, D_MODEL), F32), jax.ShapeDtypeStruct((T, LANES), F32)],
        compiler_params=_cparams(("parallel",)),
        name="router",
    )(x, g, w_router_pad)


GATHER_WINDOW = 128
GATHER_ROWS = 32


def _gather_rows(src, idx):
    M = idx.shape[0]
    C = src.shape[1]
    mesh = plsc.VectorSubcoreMesh(core_axis_name="core", subcore_axis_name="subcore")
    per = M // (mesh.num_cores * mesh.num_subcores)
    assert per * mesh.num_cores * mesh.num_subcores == M and per % GATHER_WINDOW == 0

    @pl.kernel(out_type=jax.ShapeDtypeStruct((M, C), src.dtype), mesh=mesh, name="gather_rows",
               scratch_types=[pltpu.VMEM((GATHER_WINDOW,), jnp.int32), pltpu.VMEM((GATHER_ROWS, C), src.dtype)])
    def gather(x_hbm, i_hbm, o_hbm, idx_v, buf):
        w = lax.axis_index("core") * mesh.num_subcores + lax.axis_index("subcore")

        @pl.loop(0, per // GATHER_WINDOW)
        def _(t):
            base = w * per + t * GATHER_WINDOW
            pltpu.sync_copy(i_hbm.at[pl.ds(base, GATHER_WINDOW)], idx_v)
            for k in range(GATHER_WINDOW // GATHER_ROWS):
                pltpu.sync_copy(x_hbm.at[idx_v.at[pl.ds(k * GATHER_ROWS, GATHER_ROWS)]], buf)
                pltpu.sync_copy(buf, o_hbm.at[pl.ds(base + k * GATHER_ROWS, GATHER_ROWS)])

    return gather(src, idx)


def _scatter_rows(src, dest0, dest1, n_rows):
    T, C = src.shape
    mesh = plsc.VectorSubcoreMesh(core_axis_name="core", subcore_axis_name="subcore")
    per = T // (mesh.num_cores * mesh.num_subcores)
    assert per * mesh.num_cores * mesh.num_subcores == T and per % GATHER_WINDOW == 0

    @pl.kernel(out_type=jax.ShapeDtypeStruct((n_rows, C), src.dtype), mesh=mesh, name="scatter_rows",
               scratch_types=[pltpu.VMEM((GATHER_WINDOW,), jnp.int32), pltpu.VMEM((GATHER_WINDOW,), jnp.int32),
                              pltpu.VMEM((GATHER_ROWS, C), src.dtype)])
    def scatter(x_hbm, d0_hbm, d1_hbm, o_hbm, i0, i1, buf):
        w = lax.axis_index("core") * mesh.num_subcores + lax.axis_index("subcore")

        @pl.loop(0, per // GATHER_WINDOW)
        def _(t):
            base = w * per + t * GATHER_WINDOW
            pltpu.sync_copy(d0_hbm.at[pl.ds(base, GATHER_WINDOW)], i0)
            pltpu.sync_copy(d1_hbm.at[pl.ds(base, GATHER_WINDOW)], i1)
            for k in range(GATHER_WINDOW // GATHER_ROWS):
                pltpu.sync_copy(x_hbm.at[pl.ds(base + k * GATHER_ROWS, GATHER_ROWS)], buf)
                pltpu.sync_copy(buf, o_hbm.at[i0.at[pl.ds(k * GATHER_ROWS, GATHER_ROWS)]])
                pltpu.sync_copy(buf, o_hbm.at[i1.at[pl.ds(k * GATHER_ROWS, GATHER_ROWS)]])

    return scatter(src, dest0, dest1)


EXPERT_SPLIT = 2


def _expert_kernel(be_ref, nused_ref, valid_ref, x_ref, wgu_ref, wd_ref, o_ref):
    b = pl.program_id(0)
    used = b < nused_ref[0]
    tf = EXPERT_FF // EXPERT_SPLIT

    @pl.when(used)
    def _():
        row = lax.broadcasted_iota(jnp.int32, (x_ref.shape[0], 1), 0)
        x = jnp.where(row < valid_ref[b], x_ref[...], 0.0).astype(BF16)
        y = None
        for f in range(EXPERT_SPLIT):
            gate = jnp.dot(x, wgu_ref[0, :, f * tf:(f + 1) * tf], preferred_element_type=F32)
            up = jnp.dot(x, wgu_ref[0, :, EXPERT_FF + f * tf:EXPERT_FF + (f + 1) * tf], preferred_element_type=F32)
            act = (jax.nn.silu(gate) * up).astype(BF16)
            part = jnp.dot(act, wd_ref[0, f * tf:(f + 1) * tf, :], preferred_element_type=F32)
            y = part if y is None else y + part
        o_ref[...] = y

    @pl.when(jnp.logical_not(used))
    def _():
        o_ref[...] = jnp.zeros_like(o_ref)


def _experts(x_rows, block_e, n_used, valid, w_gu, w_down, bm):
    n_rows = x_rows.shape[0]
    grid_spec = pltpu.PrefetchScalarGridSpec(
        num_scalar_prefetch=3, grid=(n_rows // bm,),
        in_specs=[pl.BlockSpec((bm, D_MODEL), lambda b, be, nu, va: (b, 0)),
                  pl.BlockSpec((1, D_MODEL, 2 * EXPERT_FF), lambda b, be, nu, va: (be[b], 0, 0)),
                  pl.BlockSpec((1, EXPERT_FF, D_MODEL), lambda b, be, nu, va: (be[b], 0, 0))],
        out_specs=pl.BlockSpec((bm, D_MODEL), lambda b, be, nu, va: (b, 0)))
    return pl.pallas_call(
        _expert_kernel, grid_spec=grid_spec,
        out_shape=jax.ShapeDtypeStruct((n_rows, D_MODEL), F32),
        compiler_params=_cparams(("arbitrary",), EXPERT_VMEM_LIMIT),
        name="experts",
    )(block_e, n_used, valid, x_rows, w_gu, w_down)


def _combine_kernel(x_ref, y0_ref, y1_ref, r_ref, g_ref, o_ref):
    r = r_ref[...]
    x = x_ref[...] + r[:, 2:3] * y0_ref[...] + r[:, 3:4] * y1_ref[...]
    o_ref[...] = _rms(x, g_ref[...])


def _combine(x, y_slots, route, g_final, tm=512):
    T = x.shape[0]
    return pl.pallas_call(
        _combine_kernel, grid=(T // tm,),
        in_specs=[pl.BlockSpec((tm, D_MODEL), lambda i: (i, 0)),
                  pl.BlockSpec((tm, D_MODEL), lambda i: (i, 0)),
                  pl.BlockSpec((tm, D_MODEL), lambda i: (T // tm + i, 0)),
                  pl.BlockSpec((tm, LANES), lambda i: (i, 0)),
                  pl.BlockSpec((1, D_MODEL), lambda i: (0, 0))],
        out_specs=pl.BlockSpec((tm, D_MODEL), lambda i: (i, 0)),
        out_shape=jax.ShapeDtypeStruct((T, D_MODEL), F32),
        compiler_params=_cparams(("parallel",)),
        name="combine",
    )(x, y_slots, y_slots, route, g_final)


def _pad_cols(w, n):
    return jnp.pad(w, ((0, 0), (0, n - w.shape[1])))


def _split_in_weights(w_in):
    sizes = [MLA_Q_LORA, MLA_KV_LORA, MLA_ROPE_DIM, BRANCH_WIDTH, BRANCH_WIDTH, BRANCH_WIDTH, HEADS,
             BRANCH_WIDTH, BRANCH_WIDTH, BRANCH_WIDTH, 3 * D_MODEL]
    pts = np.cumsum(sizes)[:-1]
    c_q, c_kv, k_pe, fq, fk, fv, fl, mq, mk, mv, gates = jnp.split(w_in, pts, axis=1)
    zeros = lambda n: jnp.zeros((D_MODEL, n), w_in.dtype)
    pe_tile = jnp.concatenate([zeros(HEAD_DIM), k_pe, zeros(LANES - HEAD_DIM - MLA_ROPE_DIM)], axis=1)
    w_small = jnp.concatenate([c_q, c_kv, pe_tile, _pad_cols(fl, LANES)], axis=1)
    w_big = jnp.concatenate([fq, mq, fk, mk, fv, mv, gates], axis=1)
    return w_small.astype(BF16), w_big.astype(BF16)


def _mla_up_weights(w_uq, w_ukv):
    dq = HEAD_DIM + MLA_ROPE_DIM
    wq = w_uq.reshape(MLA_Q_LORA, HEADS, dq)
    wq = jnp.pad(wq, ((0, 0), (0, 0), (0, LANES - dq))).reshape(MLA_Q_LORA, HEADS * LANES)
    wkv = w_ukv.reshape(MLA_KV_LORA, HEADS, 2 * HEAD_DIM)
    wk = jnp.pad(wkv[:, :, :HEAD_DIM], ((0, 0), (0, 0), (0, LANES - HEAD_DIM))).reshape(MLA_KV_LORA, HEADS * LANES)
    wv = wkv[:, :, HEAD_DIM:].reshape(MLA_KV_LORA, BRANCH_WIDTH)
    return wq.astype(BF16), wk.astype(BF16), wv.astype(BF16)


def _chunk_rows(a, B, S, t):
    return a.reshape(B, S // t, t, a.shape[-1])


ATT_TQ = 512
ATT_TK = KV_CHUNK
MOBA_TQ = 512
MOE_BM = 512


def _token_mixers(x, B, S, g_mix, w_in, g_q_lat, g_kv_lat, w_uq, w_ukv, b_forget, w_branch, w_out,
                  tabs_mla, tabs_moba):
    T = B * S
    w_small, w_big = _split_in_weights(w_in)
    wq, wk, wv = _mla_up_weights(w_uq, w_ukv)
    g = g_mix[None, :]

    big, vt_fox, vt_moba = _inproj(x, g, w_big, tabs_moba, B, S)
    q_mla, k_mla, vt_mla, f_logit = _mla_prep(
        x, g, w_small, g_q_lat[None, :], g_kv_lat[None, :], wq, wk, wv, tabs_mla,
        (HEAD_DIM + MLA_ROPE_DIM) ** -0.5 * LOG2E, B, S)

    y_mla = _flash(q_mla.reshape(B, S, -1), _chunk_rows(k_mla, B, S, ATT_TK), vt_mla, ATT_TQ, ATT_TK)

    c = _cumlogf(f_logit.reshape(B, S, HEADS).transpose(0, 2, 1), b_forget[:, None])
    c_rows = c.transpose(0, 2, 1).reshape(T, HEADS)
    q_fox, k_fox = _fox_prep(big, c_rows)
    y_fox = _flash(q_fox.reshape(B, S, -1), _chunk_rows(k_fox, B, S, ATT_TK), vt_fox, ATT_TQ, ATT_TK)

    y_moba = _moba(big.reshape(B, S, -1), _chunk_rows(big, B, S, MOBA_BLOCK), vt_moba, MOBA_TQ)

    return _merge(x, y_mla.reshape(T, -1), y_fox.reshape(T, -1), y_moba.reshape(T, -1), big,
                  w_branch.astype(BF16), w_out.astype(BF16))


def _moe(x, g_ffn, w_router, w_gu, w_down, g_final):
    T = x.shape[0]
    A = 2 * T
    bm = MOE_BM
    h, route = _router(x, g_ffn[None, :], _pad_cols(w_router, LANES))
    top_e = route[:, :2].astype(jnp.int32).reshape(A)
    onehot = (top_e[:, None] == jnp.arange(N_EXPERTS)[None, :]).astype(jnp.int32)
    csum = jnp.cumsum(onehot, axis=0)
    counts = csum[-1]
    rank = jnp.sum((csum - onehot) * onehot, axis=1)
    padded = (counts + bm - 1) // bm * bm
    pad_end = jnp.cumsum(padded)
    dest = (pad_end - padded)[top_e] + rank
    n_rows = (A // bm + N_EXPERTS) * bm
    block_start = jnp.arange(n_rows // bm) * bm
    block_e = jnp.minimum(jnp.searchsorted(pad_end, block_start, side="right"), N_EXPERTS - 1).astype(jnp.int32)
    n_used = (pad_end[-1:] // bm).astype(jnp.int32)
    valid = jnp.clip((pad_end - padded + counts)[block_e] - block_start, 0, bm).astype(jnp.int32)
    dest_slots = dest.astype(jnp.int32).reshape(T, 2).T

    x_rows = _scatter_rows(h, dest_slots[0], dest_slots[1], n_rows)
    y_rows = _experts(x_rows, block_e, n_used, valid, w_gu.astype(BF16), w_down.astype(BF16), bm)
    y_slots = _gather_rows(y_rows, dest_slots.reshape(A))
    return _combine(x, y_slots, route, g_final[None, :])


def kernel(x, positions, g_mix, w_in, g_q_lat, g_kv_lat, w_uq, w_ukv, b_forget, w_branch, w_out, g_ffn,
           w_dense_gu, w_dense_down, w_router, w_exp_gu, w_exp_down, g_final):
    B, S, D = x.shape
    T = B * S
    depth = g_mix.shape[0]
    assert depth == 2 and D == D_MODEL and S % MOBA_BLOCK == 0 and KV_CHUNK == MOBA_BLOCK
    pos = positions.reshape(T, 1).astype(jnp.int32)
    tabs_mla = _rope_tables(pos, *_rope_patterns(LANES, HEAD_DIM, MLA_ROPE_DIM // 2))
    tabs_moba = _rope_tables(pos, *_rope_patterns(HEAD_DIM, 0, HEAD_DIM // 2))
    x = x.reshape(T, D)
    for l in range(depth):
        x = _token_mixers(x, B, S, g_mix[l], w_in[l], g_q_lat[l], g_kv_lat[l], w_uq[l], w_ukv[l],
                          b_forget[l], w_branch[l], w_out[l], tabs_mla, tabs_moba)
        if l % 2 == 0:
            x = _dense_ffn(x, g_ffn[l][None, :], w_dense_gu[l // 2].astype(BF16),
                           w_dense_down[l // 2].astype(BF16))
        else:
            x = _moe(x, g_ffn[l], w_router[l // 2], w_exp_gu[l // 2], w_exp_down[l // 2], g_final)
    return x.reshape(B, S, D)
```

```python
import functools
import math

import jax
import jax.numpy as jnp
import numpy as np
from jax import lax
from jax.experimental import pallas as pl
from jax.experimental.pallas import tpu as pltpu
from jax.experimental.pallas import tpu_sc as plsc

F32 = jnp.float32
BF16 = jnp.bfloat16
NEG_INF = float("-inf")
M_INIT = -1e30
LOG2E = math.log2(math.e)

D_MODEL = 1024
RMS_EPS = 1e-6
ROPE_THETA = 10000.0
HEADS = 8
HEAD_DIM = 64
V_ROWS = 80
Q_SCALE = HEAD_DIM ** -0.5 * math.log2(math.e)
MLA_Q_LORA = 256
MLA_KV_LORA = 128
MLA_ROPE_DIM = 32
BRANCH_WIDTH = HEADS * HEAD_DIM
MOBA_BLOCK = 256
MOBA_TOPK = 3
DENSE_FF = 2816
N_EXPERTS = 8
EXPERT_FF = 3584

LANES = 128
VMEM_LIMIT = 48 * 1024 * 1024
EXPERT_VMEM_LIMIT = 62 * 1024 * 1024

COL_FQ, COL_MQ, COL_FK, COL_MK, COL_GATES = (n * BRANCH_WIDTH for n in range(5))
BIG_COLS = COL_GATES + 3 * D_MODEL
BIG_TN = 2 * BRANCH_WIDTH
W_TILES = BIG_COLS // BIG_TN + 1
V_TILE = 2
KV_CHUNK = 256
FLASH_HEADS = 4
MOBA_STEP_HEADS = 2
SMALL_COLS = MLA_Q_LORA + MLA_KV_LORA + 2 * LANES


ROW_TILE = 512
LIGHT_ROW_TILE = 1024


def _cparams(sem, vmem_limit=VMEM_LIMIT):
    return pltpu.CompilerParams(dimension_semantics=sem, vmem_limit_bytes=vmem_limit)


def _resident(shape, index_map):
    return pl.BlockSpec(shape, index_map, pipeline_mode=pl.Buffered(1))


def _rms(x, g):
    return x * lax.rsqrt(jnp.mean(x * x, axis=-1, keepdims=True) + RMS_EPS) * g


def _rope_table_kernel(pos_ref, f_ref, mc_ref, m1_ref, m2_ref, c_ref, s1_ref, s2_ref):
    ang = pos_ref[...].astype(F32) * f_ref[...]
    cos = jnp.cos(ang)
    sin = jnp.sin(ang)
    mc = mc_ref[...]
    c_ref[...] = cos * mc + (1.0 - mc)
    s1_ref[...] = sin * m1_ref[...]
    s2_ref[...] = sin * m2_ref[...]


def _rope_tables(pos, freq, mc, m1, m2, tm=LIGHT_ROW_TILE):
    T = pos.shape[0]
    row = pl.BlockSpec((tm, 1), lambda i: (i, 0))
    pat = pl.BlockSpec((1, LANES), lambda i: (0, 0))
    out = pl.BlockSpec((tm, LANES), lambda i: (i, 0))
    shp = jax.ShapeDtypeStruct((T, LANES), F32)
    return pl.pallas_call(
        _rope_table_kernel, grid=(T // tm,),
        in_specs=[row, pat, pat, pat, pat], out_specs=[out, out, out],
        out_shape=[shp, shp, shp], compiler_params=_cparams(("parallel",)),
        name="rope_tables",
    )(pos, freq, mc, m1, m2)


def _rope_patterns(group, x1_lo, half):
    d = 2 * half
    inv_freq = jnp.exp(-math.log(ROPE_THETA) * jnp.arange(half, dtype=F32) * 2.0 / d)
    lane = np.arange(LANES) % group
    in_x1 = (lane >= x1_lo) & (lane < x1_lo + half)
    in_x2 = (lane >= x1_lo + half) & (lane < x1_lo + d)
    k = np.where(in_x1, lane - x1_lo, np.where(in_x2, lane - x1_lo - half, 0))
    freq = jnp.where(jnp.asarray(in_x1 | in_x2), inv_freq[k], 0.0)[None, :].astype(F32)
    mc = jnp.asarray((in_x1 | in_x2).astype(np.float32))[None, :]
    m1 = jnp.asarray(-(in_x1.astype(np.float32)))[None, :]
    m2 = jnp.asarray(in_x2.astype(np.float32))[None, :]
    return freq, mc, m1, m2


def _apply_rope(x, c, s1, s2, half):
    n = x.shape[-1]
    reps = n // LANES
    c, s1, s2 = (jnp.tile(t, (1, reps)) if reps > 1 else t for t in (c, s1, s2))
    return x * c + pltpu.roll(x, n - half, 1) * s1 + pltpu.roll(x, half, 1) * s2


def _store_value_tiles(vt_ref, a):
    row = lax.broadcasted_iota(jnp.int32, (V_ROWS - HEAD_DIM, KV_CHUNK), 0)
    pad = jnp.where(row == 0, 1.0, 0.0).astype(BF16)
    for c in range(a.shape[0] // KV_CHUNK):
        at = a[c * KV_CHUNK:(c + 1) * KV_CHUNK, :].T.astype(BF16)
        for h in range(HEADS):
            vt_ref[0, c, h * V_ROWS:h * V_ROWS + HEAD_DIM, :] = at[h * HEAD_DIM:(h + 1) * HEAD_DIM]
            vt_ref[0, c, h * V_ROWS + HEAD_DIM:(h + 1) * V_ROWS, :] = pad


def _value_tile_spec(tm, S, grid_rank):
    per = S // tm
    imap = (lambda i: (i // per, i % per, 0, 0)) if grid_rank == 1 else (lambda i, j: (i // per, i % per, 0, 0))
    return pl.BlockSpec((1, tm // KV_CHUNK, HEADS * V_ROWS, KV_CHUNK), imap)


def _value_tile_shape(B, S):
    return jax.ShapeDtypeStruct((B, S // KV_CHUNK, HEADS * V_ROWS, KV_CHUNK), BF16)


def _inproj_kernel(x_ref, g_ref, w_ref, c_ref, s1_ref, s2_ref, o_ref, vf_ref, vm_ref):
    h = _rms(x_ref[...], g_ref[...]).astype(BF16)
    c, s1, s2 = c_ref[...], s1_ref[...], s2_ref[...]

    def tile(t):
        return jnp.dot(h, w_ref[:, t * BIG_TN:(t + 1) * BIG_TN], preferred_element_type=F32)

    for t in range(V_TILE):
        a = tile(t) * Q_SCALE if t == 0 else tile(t)
        o_ref[:, t * BIG_TN:t * BIG_TN + BRANCH_WIDTH] = a[:, :BRANCH_WIDTH].astype(BF16)
        o_ref[:, t * BIG_TN + BRANCH_WIDTH:(t + 1) * BIG_TN] = _apply_rope(
            a[:, BRANCH_WIDTH:], c, s1, s2, HEAD_DIM // 2).astype(BF16)
    a = tile(V_TILE)
    _store_value_tiles(vf_ref, a[:, :BRANCH_WIDTH])
    _store_value_tiles(vm_ref, a[:, BRANCH_WIDTH:])
    for t in range(V_TILE + 1, W_TILES):
        o_ref[:, (t - 1) * BIG_TN:t * BIG_TN] = tile(t).astype(BF16)


def _inproj(x, g, w_big, tabs, B, S, tm=ROW_TILE):
    T = x.shape[0]
    tab = pl.BlockSpec((tm, LANES), lambda i: (i, 0))
    vspec = _value_tile_spec(tm, S, 1)
    return pl.pallas_call(
        _inproj_kernel, grid=(T // tm,),
        in_specs=[pl.BlockSpec((tm, D_MODEL), lambda i: (i, 0)),
                  _resident((1, D_MODEL), lambda i: (0, 0)),
                  _resident((D_MODEL, W_TILES * BIG_TN), lambda i: (0, 0)),
                  tab, tab, tab],
        out_specs=[pl.BlockSpec((tm, BIG_COLS), lambda i: (i, 0)), vspec, vspec],
        out_shape=[jax.ShapeDtypeStruct((T, BIG_COLS), BF16), _value_tile_shape(B, S), _value_tile_shape(B, S)],
        compiler_params=_cparams(("parallel",)),
        name="inproj",
    )(x, g, w_big, *tabs)


def _mla_prep_kernel(x_ref, g_ref, ws_ref, gq_ref, gkv_ref, wq_ref, wk_ref, wv_ref,
                     c_ref, s1_ref, s2_ref, q_ref, k_ref, v_ref, fl_ref, *, scale):
    h = _rms(x_ref[...], g_ref[...]).astype(BF16)
    small = jnp.dot(h, ws_ref[...], preferred_element_type=F32)
    c_q = small[:, :MLA_Q_LORA]
    c_kv = small[:, MLA_Q_LORA:MLA_Q_LORA + MLA_KV_LORA]
    k_pe = small[:, MLA_Q_LORA + MLA_KV_LORA:MLA_Q_LORA + MLA_KV_LORA + LANES]
    fl_ref[...] = small[:, SMALL_COLS - LANES:SMALL_COLS - LANES + HEADS]
    c, s1, s2 = c_ref[...], s1_ref[...], s2_ref[...]
    half = MLA_ROPE_DIM // 2
    qn = _rms(c_q, gq_ref[...]).astype(BF16)
    q = jnp.dot(qn, wq_ref[...], preferred_element_type=F32) * scale
    q_ref[...] = _apply_rope(q, c, s1, s2, half).astype(BF16)
    kvn = _rms(c_kv, gkv_ref[...]).astype(BF16)
    k_nope = jnp.dot(kvn, wk_ref[...], preferred_element_type=F32)
    k_rot = _apply_rope(k_pe, c, s1, s2, half)
    k_ref[...] = (k_nope + jnp.tile(k_rot, (1, HEADS))).astype(BF16)
    _store_value_tiles(v_ref, jnp.dot(kvn, wv_ref[...], preferred_element_type=F32))


def _mla_prep(x, g, w_small, g_q, g_kv, wq, wk, wv, tabs, scale, B, S, tm=LIGHT_ROW_TILE):
    T = x.shape[0]
    full = lambda shape: pl.BlockSpec(shape, lambda i: (0,) * len(shape))
    row = lambda n: pl.BlockSpec((tm, n), lambda i: (i, 0))
    qk = HEADS * LANES
    return pl.pallas_call(
        functools.partial(_mla_prep_kernel, scale=scale), grid=(T // tm,),
        in_specs=[row(D_MODEL), full((1, D_MODEL)), full((D_MODEL, SMALL_COLS)),
                  full((1, MLA_Q_LORA)), full((1, MLA_KV_LORA)),
                  full((MLA_Q_LORA, qk)), full((MLA_KV_LORA, qk)), full((MLA_KV_LORA, BRANCH_WIDTH)),
                  row(LANES), row(LANES), row(LANES)],
        out_specs=[row(qk), row(qk), _value_tile_spec(tm, S, 1), row(HEADS)],
        out_shape=[jax.ShapeDtypeStruct((T, qk), BF16), jax.ShapeDtypeStruct((T, qk), BF16),
                   _value_tile_shape(B, S), jax.ShapeDtypeStruct((T, HEADS), F32)],
        compiler_params=_cparams(("parallel",)),
        name="mla_prep",
    )(x, g, w_small, g_q, g_kv, wq, wk, wv, *tabs)


def _cumlogf_kernel(fl_ref, b_ref, c_ref):
    z = fl_ref[0] + b_ref[...]
    x = jnp.minimum(z, 0.0) - jnp.log1p(jnp.exp(-jnp.abs(z)))
    n = x.shape[-1]
    lane = lax.broadcasted_iota(jnp.int32, x.shape, 1)
    d = 1
    while d < n:
        x = x + jnp.where(lane >= d, pltpu.roll(x, d, 1), 0.0)
        d *= 2
    c_ref[0] = x


def _cumlogf(fl_t, b_col):
    B, H, S = fl_t.shape
    return pl.pallas_call(
        _cumlogf_kernel, grid=(B,),
        in_specs=[pl.BlockSpec((1, H, S), lambda b: (b, 0, 0)), pl.BlockSpec((H, 1), lambda b: (0, 0))],
        out_specs=pl.BlockSpec((1, H, S), lambda b: (b, 0, 0)),
        out_shape=jax.ShapeDtypeStruct((B, H, S), F32),
        compiler_params=_cparams(("parallel",)),
        name="cumlogf",
    )(fl_t, b_col)


def _split3(c):
    hi = c.astype(BF16)
    r = c - hi.astype(F32)
    mid = r.astype(BF16)
    lo = (r - mid.astype(F32)).astype(BF16)
    return hi.astype(F32), mid.astype(F32), lo.astype(F32)


def _fox_prep_kernel(q_ref, k_ref, c_ref, qo_ref, ko_ref):
    tm = q_ref.shape[0]
    lane = lax.broadcasted_iota(jnp.int32, (tm, LANES), 1)
    c = c_ref[...] * LOG2E
    for hp in range(HEADS // 2):
        q2 = q_ref[:, hp * LANES:(hp + 1) * LANES].astype(F32)
        k2 = k_ref[:, hp * LANES:(hp + 1) * LANES].astype(F32)
        for hh in range(2):
            h = 2 * hp + hh
            terms = [jnp.broadcast_to(t, (tm, LANES)) for t in _split3(c[:, h:h + 1])]
            qh = q2 if hh == 0 else pltpu.roll(q2, HEAD_DIM, 1)
            kh = k2 if hh == 0 else pltpu.roll(k2, HEAD_DIM, 1)
            q_aug = jnp.where(lane < HEAD_DIM + 3, 1.0, 0.0)
            k_aug = jnp.where((lane >= HEAD_DIM + 3) & (lane < HEAD_DIM + 6), 1.0, 0.0)
            for n, t in enumerate(terms):
                q_aug = jnp.where(lane == HEAD_DIM + 3 + n, t, q_aug)
                k_aug = jnp.where(lane == HEAD_DIM + n, -t, k_aug)
            q_aug = jnp.where(lane < HEAD_DIM, qh, q_aug)
            k_aug = jnp.where(lane < HEAD_DIM, kh, k_aug)
            qo_ref[:, h * LANES:(h + 1) * LANES] = q_aug.astype(BF16)
            ko_ref[:, h * LANES:(h + 1) * LANES] = k_aug.astype(BF16)


def _fox_prep(big, c_rows, tm=LIGHT_ROW_TILE):
    T = big.shape[0]
    qk = HEADS * LANES
    return pl.pallas_call(
        _fox_prep_kernel, grid=(T // tm,),
        in_specs=[pl.BlockSpec((tm, BRANCH_WIDTH), lambda i: (i, COL_FQ // BRANCH_WIDTH)),
                  pl.BlockSpec((tm, BRANCH_WIDTH), lambda i: (i, COL_FK // BRANCH_WIDTH)),
                  pl.BlockSpec((tm, HEADS), lambda i: (i, 0))],
        out_specs=[pl.BlockSpec((tm, qk), lambda i: (i, 0)), pl.BlockSpec((tm, qk), lambda i: (i, 0))],
        out_shape=[jax.ShapeDtypeStruct((T, qk), BF16), jax.ShapeDtypeStruct((T, qk), BF16)],
        compiler_params=_cparams(("parallel",)),
        name="fox_prep",
    )(big, big, c_rows)


def _nt_dot(a, b):
    return lax.dot_general(a, b, (((1,), (1,)), ((), ())), preferred_element_type=F32)


def _softmax_step(st, m, acc, vt):
    m_new = jnp.maximum(m, jnp.max(st, axis=0, keepdims=True))
    upd = jnp.dot(vt, jnp.exp2(st - m_new).astype(BF16), preferred_element_type=F32)
    return m_new, jnp.exp2(m - m_new) * acc + upd


def _softmax_init(tq):
    return jnp.full((1, tq), M_INIT, F32), jnp.zeros((V_ROWS, tq), F32)


def _softmax_finish(acc):
    return (acc[:HEAD_DIM] / acc[HEAD_DIM:HEAD_DIM + 1]).T


def _attend(npairs, qk, val, past, diag, tq, sa, sb):
    heads = range(len(sa))

    def put(dst, j):
        for hh in heads:
            dst[hh][...] = qk(hh, j)

    def advance(carry, src, j, fn):
        return tuple(_softmax_step(fn(hh, j, src[hh][...]), *carry[hh], val(hh, j)) for hh in heads)

    put(sa, 0)

    def body(jj, carry):
        j0 = 2 * jj
        put(sb, j0 + 1)
        carry = advance(carry, sa, j0, past)
        put(sa, j0 + 2)
        return advance(carry, sb, j0 + 1, past)

    carry = lax.fori_loop(0, npairs, body, tuple(_softmax_init(tq) for _ in heads))
    j0 = 2 * npairs
    put(sb, j0 + 1)
    carry = advance(carry, sa, j0, lambda hh, j, st: diag(hh, 0, st))
    carry = advance(carry, sb, j0 + 1, lambda hh, j, st: diag(hh, 1, st))
    return jnp.concatenate([_softmax_finish(acc) for _, acc in carry], axis=1)


def _score_scratch(tq, tk, nh):
    return [pltpu.VMEM((tk, tq), F32) for _ in range(2 * nh)]


def _flash_kernel(q_ref, k_ref, vt_ref, o_ref, *scores, tq, tk, nh):
    i = pl.program_id(2)
    assert tq == 2 * tk
    krow = lax.broadcasted_iota(jnp.int32, (tk, tq), 0)
    qcol = lax.broadcasted_iota(jnp.int32, (tk, tq), 1)

    def qk(hh, j):
        return _nt_dot(k_ref[0, j, :, hh * LANES:(hh + 1) * LANES], q_ref[0, :, hh * LANES:(hh + 1) * LANES])

    def val(hh, j):
        return vt_ref[0, j, hh * V_ROWS:(hh + 1) * V_ROWS, :]

    def diag(hh, d, st):
        return jnp.where(d * tk + krow <= qcol, st, NEG_INF)

    out = _attend(i, qk, val, lambda hh, j, st: st, diag, tq, scores[:nh], scores[nh:])
    o_ref[0] = out.astype(BF16)


def _flash(q, k, vt, tq, tk, nh=FLASH_HEADS):
    B, S, _ = q.shape
    nk = S // tk
    return pl.pallas_call(
        functools.partial(_flash_kernel, tq=tq, tk=tk, nh=nh), grid=(B, HEADS // nh, S // tq),
        in_specs=[pl.BlockSpec((1, tq, nh * LANES), lambda b, h, i: (b, i, h)),
                  pl.BlockSpec((1, nk, tk, nh * LANES), lambda b, h, i: (b, 0, 0, h)),
                  pl.BlockSpec((1, nk, nh * V_ROWS, tk), lambda b, h, i: (b, 0, h, 0))],
        out_specs=pl.BlockSpec((1, tq, nh * HEAD_DIM), lambda b, h, i: (b, i, h)),
        out_shape=jax.ShapeDtypeStruct((B, S, BRANCH_WIDTH), BF16),
        scratch_shapes=_score_scratch(tq, tk, nh),
        compiler_params=_cparams(("parallel", "parallel", "arbitrary")),
        name="flash",
    )(q, k, vt)


def _moba_kernel(q_ref, k_ref, vt_ref, o_ref, kmean_ref, bias_ref, qm_ref, *scores, nblk, tq, nh):
    i = pl.program_id(2)
    blk = MOBA_BLOCK
    shift = blk.bit_length() - 1
    r = tq // blk

    @pl.when(i == 0)
    def _():
        for n in range(nblk):
            kmean_ref[n:n + 1, :] = jnp.mean(k_ref[0, n].astype(F32), axis=0, keepdims=True)

    lane = lax.broadcasted_iota(jnp.int32, (tq, nh * HEAD_DIM), 1)
    blk_id = lax.broadcasted_iota(jnp.int32, (nblk, tq), 0)
    own = i * r + (lax.broadcasted_iota(jnp.int32, (nblk, tq), 1) >> shift)
    q2 = q_ref[0]
    for hh in range(nh):
        in_head = (lane >= hh * HEAD_DIM) & (lane < (hh + 1) * HEAD_DIM)
        q = jnp.where(in_head, q2, jnp.zeros_like(q2))
        qm_ref[hh] = q
        g3 = _nt_dot(jnp.concatenate([t.astype(BF16) for t in _split3(kmean_ref[...])], axis=0), q)
        g = g3[:nblk] + g3[nblk:2 * nblk] + g3[2 * nblk:]
        g = jnp.where(blk_id < own, g, NEG_INF)
        bias = jnp.full((nblk, tq), NEG_INF, F32)
        for _ in range(MOBA_TOPK):
            mx = jnp.max(g, axis=0, keepdims=True)
            first = jnp.min(jnp.where(g == mx, blk_id, nblk), axis=0, keepdims=True)
            pick = (blk_id == first) & (mx > NEG_INF)
            bias = jnp.where(pick, 0.0, bias)
            g = jnp.where(pick, NEG_INF, g)
        bias_ref[hh] = bias

    krow = lax.broadcasted_iota(jnp.int32, (blk, tq), 0)
    qcol = lax.broadcasted_iota(jnp.int32, (blk, tq), 1)

    def qk(hh, n):
        return _nt_dot(k_ref[0, n], qm_ref[hh])

    def val(hh, n):
        return vt_ref[0, n, hh * V_ROWS:(hh + 1) * V_ROWS, :]

    def past(hh, n, st):
        return st + bias_ref[hh, pl.ds(n, 1), :]

    def diag(hh, d, st):
        own_causal = ((qcol >> shift) == d) & (krow <= (qcol & (blk - 1)))
        return jnp.where(own_causal, st, past(hh, i * r + d, st))

    out = _attend(i, qk, val, past, diag, tq, scores[:nh], scores[nh:])
    o_ref[0] = out.astype(BF16)


def _moba(q, k, vt, tq, nh=MOBA_STEP_HEADS):
    B, S, _ = q.shape
    nblk = S // MOBA_BLOCK
    assert tq == 2 * MOBA_BLOCK and MOBA_BLOCK & (MOBA_BLOCK - 1) == 0
    w = nh * HEAD_DIM
    qc = COL_MQ // w
    kc = COL_MK // w
    return pl.pallas_call(
        functools.partial(_moba_kernel, nblk=nblk, tq=tq, nh=nh), grid=(B, HEADS // nh, S // tq),
        in_specs=[pl.BlockSpec((1, tq, w), lambda b, h, i: (b, i, qc + h)),
                  pl.BlockSpec((1, nblk, MOBA_BLOCK, w), lambda b, h, i: (b, 0, 0, kc + h)),
                  pl.BlockSpec((1, nblk, nh * V_ROWS, MOBA_BLOCK), lambda b, h, i: (b, 0, h, 0))],
        out_specs=pl.BlockSpec((1, tq, w), lambda b, h, i: (b, i, h)),
        out_shape=jax.ShapeDtypeStruct((B, S, BRANCH_WIDTH), BF16),
        scratch_shapes=[pltpu.VMEM((nblk, w), F32), pltpu.VMEM((nh, nblk, tq), F32),
                        pltpu.VMEM((nh, tq, w), BF16)] + _score_scratch(tq, MOBA_BLOCK, nh),
        compiler_params=_cparams(("parallel", "parallel", "arbitrary")),
        name="moba",
    )(q, k, vt)


def _merge_kernel(x_ref, ya_ref, yb_ref, yc_ref, ga_ref, gb_ref, gc_ref, wb_ref, wo_ref, o_ref):
    merged = None
    for n, (y_ref, g_ref) in enumerate(((ya_ref, ga_ref), (yb_ref, gb_ref), (yc_ref, gc_ref))):
        proj = jnp.dot(y_ref[...], wb_ref[n], preferred_element_type=F32)
        term = jax.nn.sigmoid(g_ref[...].astype(F32)) * proj
        merged = term if merged is None else merged + term
    o_ref[...] = x_ref[...] + jnp.dot(merged.astype(BF16), wo_ref[...], preferred_element_type=F32)


def _merge(x, y_mla, y_fox, y_moba, big, w_branch, w_out, tm=ROW_TILE):
    T = x.shape[0]
    g0 = COL_GATES // D_MODEL
    row = lambda n: pl.BlockSpec((tm, n), lambda i: (i, 0))
    gate = lambda n: pl.BlockSpec((tm, D_MODEL), lambda i: (i, g0 + n))
    return pl.pallas_call(
        _merge_kernel, grid=(T // tm,),
        in_specs=[row(D_MODEL), row(BRANCH_WIDTH), row(BRANCH_WIDTH), row(BRANCH_WIDTH),
                  gate(0), gate(1), gate(2),
                  _resident((3, BRANCH_WIDTH, D_MODEL), lambda i: (0, 0, 0)),
                  _resident((D_MODEL, D_MODEL), lambda i: (0, 0))],
        out_specs=row(D_MODEL),
        out_shape=jax.ShapeDtypeStruct((T, D_MODEL), F32),
        compiler_params=_cparams(("parallel",)),
        name="merge",
    )(x, y_mla, y_fox, y_moba, big, big, big, w_branch, w_out)


def _dense_ffn_kernel(x_ref, g_ref, wg_ref, wu_ref, wd_ref, o_ref):
    x = x_ref[...]
    h = _rms(x, g_ref[...]).astype(BF16)
    gate = jnp.dot(h, wg_ref[...], preferred_element_type=F32)
    up = jnp.dot(h, wu_ref[...], preferred_element_type=F32)
    act = (jax.nn.silu(gate) * up).astype(BF16)
    o_ref[...] = x + jnp.dot(act, wd_ref[...], preferred_element_type=F32)


def _dense_ffn(x, g, w_gu, w_down, tm=ROW_TILE):
    T = x.shape[0]
    return pl.pallas_call(
        _dense_ffn_kernel, grid=(T // tm,),
        in_specs=[pl.BlockSpec((tm, D_MODEL), lambda i: (i, 0)),
                  _resident((1, D_MODEL), lambda i: (0, 0)),
                  _resident((D_MODEL, DENSE_FF), lambda i: (0, 0)),
                  _resident((D_MODEL, DENSE_FF), lambda i: (0, 1)),
                  _resident((DENSE_FF, D_MODEL), lambda i: (0, 0))],
        out_specs=pl.BlockSpec((tm, D_MODEL), lambda i: (i, 0)),
        out_shape=jax.ShapeDtypeStruct((T, D_MODEL), F32),
        compiler_params=_cparams(("parallel",)),
        name="dense_ffn",
    )(x, g, w_gu, w_gu, w_down)


def _router_kernel(x_ref, g_ref, wr_ref, h_ref, r_ref):
    h = _rms(x_ref[...], g_ref[...])
    h_ref[...] = h
    logits = jnp.dot(h, wr_ref[...], precision=lax.Precision.HIGHEST, preferred_element_type=F32)
    lane = lax.broadcasted_iota(jnp.int32, logits.shape, 1)
    logits = jnp.where(lane < N_EXPERTS, logits, NEG_INF)
    m1 = jnp.max(logits, axis=-1, keepdims=True)
    i1 = jnp.min(jnp.where(logits == m1, lane, LANES), axis=-1, keepdims=True)
    rest = jnp.where(lane == i1, NEG_INF, logits)
    m2 = jnp.max(rest, axis=-1, keepdims=True)
    i2 = jnp.min(jnp.where(rest == m2, lane, LANES), axis=-1, keepdims=True)
    e2 = jnp.exp(m2 - m1)
    w1 = 1.0 / (1.0 + e2)
    w2 = e2 / (1.0 + e2)
    r_ref[...] = jnp.where(lane == 0, i1.astype(F32), jnp.where(lane == 1, i2.astype(F32),
                           jnp.where(lane == 2, w1, jnp.where(lane == 3, w2, 0.0))))


def _router(x, g, w_router_pad, tm=LIGHT_ROW_TILE):
    T = x.shape[0]
    return pl.pallas_call(
        _router_kernel, grid=(T // tm,),
        in_specs=[pl.BlockSpec((tm, D_MODEL), lambda i: (i, 0)),
                  pl.BlockSpec((1, D_MODEL), lambda i: (0, 0)),
                  pl.BlockSpec((D_MODEL, LANES), lambda i: (0, 0))],
        out_specs=[pl.BlockSpec((tm, D_MODEL), lambda i: (i, 0)), pl.BlockSpec((tm, LANES), lambda i: (i, 0))],
        out_shape=[jax.ShapeDtypeStruct((T, D_MODEL), F32), jax.ShapeDtypeStruct((T, LANES), F32)],
        compiler_params=_cparams(("parallel",)),
        name="router",
    )(x, g, w_router_pad)


GATHER_WINDOW = 128
GATHER_ROWS = 32


def _gather_rows(src, idx):
    M = idx.shape[0]
    C = src.shape[1]
    mesh = plsc.VectorSubcoreMesh(core_axis_name="core", subcore_axis_name="subcore")
    per = M // (mesh.num_cores * mesh.num_subcores)
    assert per * mesh.num_cores * mesh.num_subcores == M and per % GATHER_WINDOW == 0

    @pl.kernel(out_type=jax.ShapeDtypeStruct((M, C), src.dtype), mesh=mesh, name="gather_rows",
               scratch_types=[pltpu.VMEM((GATHER_WINDOW,), jnp.int32), pltpu.VMEM((GATHER_ROWS, C), src.dtype)])
    def gather(x_hbm, i_hbm, o_hbm, idx_v, buf):
        w = lax.axis_index("core") * mesh.num_subcores + lax.axis_index("subcore")

        @pl.loop(0, per // GATHER_WINDOW)
        def _(t):
            base = w * per + t * GATHER_WINDOW
            pltpu.sync_copy(i_hbm.at[pl.ds(base, GATHER_WINDOW)], idx_v)
            for k in range(GATHER_WINDOW // GATHER_ROWS):
                pltpu.sync_copy(x_hbm.at[idx_v.at[pl.ds(k * GATHER_ROWS, GATHER_ROWS)]], buf)
                pltpu.sync_copy(buf, o_hbm.at[pl.ds(base + k * GATHER_ROWS, GATHER_ROWS)])

    return gather(src, idx)


def _scatter_rows(src, dest0, dest1, n_rows):
    T, C = src.shape
    mesh = plsc.VectorSubcoreMesh(core_axis_name="core", subcore_axis_name="subcore")
    per = T // (mesh.num_cores * mesh.num_subcores)
    assert per * mesh.num_cores * mesh.num_subcores == T and per % GATHER_WINDOW == 0

    @pl.kernel(out_type=jax.ShapeDtypeStruct((n_rows, C), src.dtype), mesh=mesh, name="scatter_rows",
               scratch_types=[pltpu.VMEM((GATHER_WINDOW,), jnp.int32), pltpu.VMEM((GATHER_WINDOW,), jnp.int32),
                              pltpu.VMEM((GATHER_ROWS, C), src.dtype)])
    def scatter(x_hbm, d0_hbm, d1_hbm, o_hbm, i0, i1, buf):
        w = lax.axis_index("core") * mesh.num_subcores + lax.axis_index("subcore")

        @pl.loop(0, per // GATHER_WINDOW)
        def _(t):
            base = w * per + t * GATHER_WINDOW
            pltpu.sync_copy(d0_hbm.at[pl.ds(base, GATHER_WINDOW)], i0)
            pltpu.sync_copy(d1_hbm.at[pl.ds(base, GATHER_WINDOW)], i1)
            for k in range(GATHER_WINDOW // GATHER_ROWS):
                pltpu.sync_copy(x_hbm.at[pl.ds(base + k * GATHER_ROWS, GATHER_ROWS)], buf)
                pltpu.sync_copy(buf, o_hbm.at[i0.at[pl.ds(k * GATHER_ROWS, GATHER_ROWS)]])
                pltpu.sync_copy(buf, o_hbm.at[i1.at[pl.ds(k * GATHER_ROWS, GATHER_ROWS)]])

    return scatter(src, dest0, dest1)


EXPERT_SPLIT = 2


def _expert_kernel(be_ref, nused_ref, valid_ref, x_ref, wgu_ref, wd_ref, o_ref):
    b = pl.program_id(0)
    used = b < nused_ref[0]
    tf = EXPERT_FF // EXPERT_SPLIT

    @pl.when(used)
    def _():
        row = lax.broadcasted_iota(jnp.int32, (x_ref.shape[0], 1), 0)
        x = jnp.where(row < valid_ref[b], x_ref[...], 0.0).astype(BF16)
        y = None
        for f in range(EXPERT_SPLIT):
            gate = jnp.dot(x, wgu_ref[0, :, f * tf:(f + 1) * tf], preferred_element_type=F32)
            up = jnp.dot(x, wgu_ref[0, :, EXPERT_FF + f * tf:EXPERT_FF + (f + 1) * tf], preferred_element_type=F32)
            act = (jax.nn.silu(gate) * up).astype(BF16)
            part = jnp.dot(act, wd_ref[0, f * tf:(f + 1) * tf, :], preferred_element_type=F32)
            y = part if y is None else y + part
        o_ref[...] = y

    @pl.when(jnp.logical_not(used))
    def _():
        o_ref[...] = jnp.zeros_like(o_ref)


def _experts(x_rows, block_e, n_used, valid, w_gu, w_down, bm):
    n_rows = x_rows.shape[0]
    grid_spec = pltpu.PrefetchScalarGridSpec(
        num_scalar_prefetch=3, grid=(n_rows // bm,),
        in_specs=[pl.BlockSpec((bm, D_MODEL), lambda b, be, nu, va: (b, 0)),
                  pl.BlockSpec((1, D_MODEL, 2 * EXPERT_FF), lambda b, be, nu, va: (be[b], 0, 0)),
                  pl.BlockSpec((1, EXPERT_FF, D_MODEL), lambda b, be, nu, va: (be[b], 0, 0))],
        out_specs=pl.BlockSpec((bm, D_MODEL), lambda b, be, nu, va: (b, 0)))
    return pl.pallas_call(
        _expert_kernel, grid_spec=grid_spec,
        out_shape=jax.ShapeDtypeStruct((n_rows, D_MODEL), F32),
        compiler_params=_cparams(("arbitrary",), EXPERT_VMEM_LIMIT),
        name="experts",
    )(block_e, n_used, valid, x_rows, w_gu, w_down)


def _combine_kernel(x_ref, y0_ref, y1_ref, r_ref, g_ref, o_ref):
    r = r_ref[...]
    x = x_ref[...] + r[:, 2:3] * y0_ref[...] + r[:, 3:4] * y1_ref[...]
    o_ref[...] = _rms(x, g_ref[...])


def _combine(x, y_slots, route, g_final, tm=LIGHT_ROW_TILE):
    T = x.shape[0]
    return pl.pallas_call(
        _combine_kernel, grid=(T // tm,),
        in_specs=[pl.BlockSpec((tm, D_MODEL), lambda i: (i, 0)),
                  pl.BlockSpec((tm, D_MODEL), lambda i: (i, 0)),
                  pl.BlockSpec((tm, D_MODEL), lambda i: (T // tm + i, 0)),
                  pl.BlockSpec((tm, LANES), lambda i: (i, 0)),
                  pl.BlockSpec((1, D_MODEL), lambda i: (0, 0))],
        out_specs=pl.BlockSpec((tm, D_MODEL), lambda i: (i, 0)),
        out_shape=jax.ShapeDtypeStruct((T, D_MODEL), F32),
        compiler_params=_cparams(("parallel",)),
        name="combine",
    )(x, y_slots, y_slots, route, g_final)


def _pad_cols(w, n):
    return jnp.pad(w, ((0, 0), (0, n - w.shape[1])))


def _split_in_weights(w_in):
    sizes = [MLA_Q_LORA, MLA_KV_LORA, MLA_ROPE_DIM, BRANCH_WIDTH, BRANCH_WIDTH, BRANCH_WIDTH, HEADS,
             BRANCH_WIDTH, BRANCH_WIDTH, BRANCH_WIDTH, 3 * D_MODEL]
    pts = np.cumsum(sizes)[:-1]
    c_q, c_kv, k_pe, fq, fk, fv, fl, mq, mk, mv, gates = jnp.split(w_in, pts, axis=1)
    zeros = lambda n: jnp.zeros((D_MODEL, n), w_in.dtype)
    pe_tile = jnp.concatenate([zeros(HEAD_DIM), k_pe, zeros(LANES - HEAD_DIM - MLA_ROPE_DIM)], axis=1)
    w_small = jnp.concatenate([c_q, c_kv, pe_tile, _pad_cols(fl, LANES)], axis=1)
    w_big = jnp.concatenate([fq, mq, fk, mk, fv, mv, gates], axis=1)
    return w_small.astype(BF16), w_big.astype(BF16)


def _mla_up_weights(w_uq, w_ukv):
    dq = HEAD_DIM + MLA_ROPE_DIM
    wq = w_uq.reshape(MLA_Q_LORA, HEADS, dq)
    wq = jnp.pad(wq, ((0, 0), (0, 0), (0, LANES - dq))).reshape(MLA_Q_LORA, HEADS * LANES)
    wkv = w_ukv.reshape(MLA_KV_LORA, HEADS, 2 * HEAD_DIM)
    wk = jnp.pad(wkv[:, :, :HEAD_DIM], ((0, 0), (0, 0), (0, LANES - HEAD_DIM))).reshape(MLA_KV_LORA, HEADS * LANES)
    wv = wkv[:, :, HEAD_DIM:].reshape(MLA_KV_LORA, BRANCH_WIDTH)
    return wq.astype(BF16), wk.astype(BF16), wv.astype(BF16)


def _chunk_rows(a, B, S, t):
    return a.reshape(B, S // t, t, a.shape[-1])


ATT_TQ = 2 * KV_CHUNK
ATT_TK = KV_CHUNK
MOBA_TQ = 2 * MOBA_BLOCK
MOE_BM = 512


def _token_mixers(x, B, S, g_mix, w_in, g_q_lat, g_kv_lat, w_uq, w_ukv, b_forget, w_branch, w_out,
                  tabs_mla, tabs_moba):
    T = B * S
    w_small, w_big = _split_in_weights(w_in)
    wq, wk, wv = _mla_up_weights(w_uq, w_ukv)
    g = g_mix[None, :]

    big, vt_fox, vt_moba = _inproj(x, g, w_big, tabs_moba, B, S)
    q_mla, k_mla, vt_mla, f_logit = _mla_prep(
        x, g, w_small, g_q_lat[None, :], g_kv_lat[None, :], wq, wk, wv, tabs_mla,
        (HEAD_DIM + MLA_ROPE_DIM) ** -0.5 * LOG2E, B, S)

    y_mla = _flash(q_mla.reshape(B, S, -1), _chunk_rows(k_mla, B, S, ATT_TK), vt_mla, ATT_TQ, ATT_TK)

    c = _cumlogf(f_logit.reshape(B, S, HEADS).transpose(0, 2, 1), b_forget[:, None])
    c_rows = c.transpose(0, 2, 1).reshape(T, HEADS)
    q_fox, k_fox = _fox_prep(big, c_rows)
    y_fox = _flash(q_fox.reshape(B, S, -1), _chunk_rows(k_fox, B, S, ATT_TK), vt_fox, ATT_TQ, ATT_TK)

    y_moba = _moba(big.reshape(B, S, -1), _chunk_rows(big, B, S, MOBA_BLOCK), vt_moba, MOBA_TQ)

    return _merge(x, y_mla.reshape(T, -1), y_fox.reshape(T, -1), y_moba.reshape(T, -1), big,
                  w_branch.astype(BF16), w_out.astype(BF16))


def _moe(x, g_ffn, w_router, w_gu, w_down, g_final):
    T = x.shape[0]
    A = 2 * T
    bm = MOE_BM
    h, route = _router(x, g_ffn[None, :], _pad_cols(w_router, LANES))
    top_e = route[:, :2].astype(jnp.int32).reshape(A)
    onehot = (top_e[:, None] == jnp.arange(N_EXPERTS)[None, :]).astype(jnp.int32)
    csum = jnp.cumsum(onehot, axis=0)
    counts = csum[-1]
    rank = jnp.sum((csum - onehot) * onehot, axis=1)
    padded = (counts + bm - 1) // bm * bm
    pad_end = jnp.cumsum(padded)
    dest = (pad_end - padded)[top_e] + rank
    n_rows = (A // bm + N_EXPERTS) * bm
    block_start = jnp.arange(n_rows // bm) * bm
    block_e = jnp.minimum(jnp.searchsorted(pad_end, block_start, side="right"), N_EXPERTS - 1).astype(jnp.int32)
    n_used = (pad_end[-1:] // bm).astype(jnp.int32)
    valid = jnp.clip((pad_end - padded + counts)[block_e] - block_start, 0, bm).astype(jnp.int32)
    dest_slots = dest.astype(jnp.int32).reshape(T, 2).T

    x_rows = _scatter_rows(h, dest_slots[0], dest_slots[1], n_rows)
    y_rows = _experts(x_rows, block_e, n_used, valid, w_gu.astype(BF16), w_down.astype(BF16), bm)
    y_slots = _gather_rows(y_rows, dest_slots.reshape(A))
    return _combine(x, y_slots, route, g_final[None, :])


def kernel(x, positions, g_mix, w_in, g_q_lat, g_kv_lat, w_uq, w_ukv, b_forget, w_branch, w_out, g_ffn,
           w_dense_gu, w_dense_down, w_router, w_exp_gu, w_exp_down, g_final):
    B, S, D = x.shape
    T = B * S
    depth = g_mix.shape[0]
    assert depth == 2 and D == D_MODEL and S % LIGHT_ROW_TILE == 0 and KV_CHUNK == MOBA_BLOCK
    pos = positions.reshape(T, 1).astype(jnp.int32)
    tabs_mla = _rope_tables(pos, *_rope_patterns(LANES, HEAD_DIM, MLA_ROPE_DIM // 2))
    tabs_moba = _rope_tables(pos, *_rope_patterns(HEAD_DIM, 0, HEAD_DIM // 2))
    x = x.reshape(T, D)
    for l in range(depth):
        x = _token_mixers(x, B, S, g_mix[l], w_in[l], g_q_lat[l], g_kv_lat[l], w_uq[l], w_ukv[l],
                          b_forget[l], w_branch[l], w_out[l], tabs_mla, tabs_moba)
        if l % 2 == 0:
            x = _dense_ffn(x, g_ffn[l][None, :], w_dense_gu[l // 2].astype(BF16),
                           w_dense_down[l // 2].astype(BF16))
        else:
            x = _moe(x, g_ffn[l], w_router[l // 2], w_exp_gu[l // 2], w_exp_down[l // 2], g_final)
    return x.reshape(B, S, D)
```

```python
import functools
import math

import jax
import jax.numpy as jnp
import numpy as np
from jax import lax
from jax.experimental import pallas as pl
from jax.experimental.pallas import tpu as pltpu
from jax.experimental.pallas import tpu_sc as plsc

F32 = jnp.float32
BF16 = jnp.bfloat16
NEG_INF = float("-inf")
M_INIT = -1e30
LOG2E = math.log2(math.e)

D_MODEL = 1024
RMS_EPS = 1e-6
ROPE_THETA = 10000.0
HEADS = 8
HEAD_DIM = 64
V_ROWS = 80
Q_SCALE = HEAD_DIM ** -0.5 * math.log2(math.e)
MLA_Q_LORA = 256
MLA_KV_LORA = 128
MLA_ROPE_DIM = 32
BRANCH_WIDTH = HEADS * HEAD_DIM
MOBA_BLOCK = 256
MOBA_TOPK = 3
DENSE_FF = 2816
N_EXPERTS = 8
EXPERT_FF = 3584

LANES = 128
VMEM_LIMIT = 48 * 1024 * 1024
EXPERT_VMEM_LIMIT = 62 * 1024 * 1024

COL_FQ, COL_MQ, COL_FK, COL_MK, COL_GATES = (n * BRANCH_WIDTH for n in range(5))
BIG_COLS = COL_GATES + 3 * D_MODEL
BIG_TN = 2 * BRANCH_WIDTH
W_TILES = BIG_COLS // BIG_TN + 1
V_TILE = 2
KV_CHUNK = 256
FLASH_HEADS = 4
MOBA_STEP_HEADS = 2
SMALL_COLS = MLA_Q_LORA + MLA_KV_LORA + 2 * LANES


ROW_TILE = 512
LIGHT_ROW_TILE = 1024


def _cparams(sem, vmem_limit=VMEM_LIMIT):
    return pltpu.CompilerParams(dimension_semantics=sem, vmem_limit_bytes=vmem_limit)


def _resident(shape, index_map):
    return pl.BlockSpec(shape, index_map, pipeline_mode=pl.Buffered(1))


def _rms(x, g):
    return x * lax.rsqrt(jnp.mean(x * x, axis=-1, keepdims=True) + RMS_EPS) * g


def _rope_table_kernel(pos_ref, f_ref, mc_ref, m1_ref, m2_ref, c_ref, s1_ref, s2_ref):
    ang = pos_ref[...].astype(F32) * f_ref[...]
    cos = jnp.cos(ang)
    sin = jnp.sin(ang)
    mc = mc_ref[...]
    c_ref[...] = cos * mc + (1.0 - mc)
    s1_ref[...] = sin * m1_ref[...]
    s2_ref[...] = sin * m2_ref[...]


def _rope_tables(pos, freq, mc, m1, m2, tm=LIGHT_ROW_TILE):
    T = pos.shape[0]
    row = pl.BlockSpec((tm, 1), lambda i: (i, 0))
    pat = pl.BlockSpec((1, LANES), lambda i: (0, 0))
    out = pl.BlockSpec((tm, LANES), lambda i: (i, 0))
    shp = jax.ShapeDtypeStruct((T, LANES), F32)
    return pl.pallas_call(
        _rope_table_kernel, grid=(T // tm,),
        in_specs=[row, pat, pat, pat, pat], out_specs=[out, out, out],
        out_shape=[shp, shp, shp], compiler_params=_cparams(("parallel",)),
        name="rope_tables",
    )(pos, freq, mc, m1, m2)


def _rope_patterns(group, x1_lo, half, x2_lo=None):
    d = 2 * half
    x2_lo = x1_lo + half if x2_lo is None else x2_lo
    inv_freq = jnp.exp(-math.log(ROPE_THETA) * jnp.arange(half, dtype=F32) * 2.0 / d)
    lane = np.arange(LANES) % group
    in_x1 = (lane >= x1_lo) & (lane < x1_lo + half)
    in_x2 = (lane >= x2_lo) & (lane < x2_lo + half)
    k = np.where(in_x1, lane - x1_lo, np.where(in_x2, lane - x2_lo, 0))
    freq = jnp.where(jnp.asarray(in_x1 | in_x2), inv_freq[k], 0.0)[None, :].astype(F32)
    mc = jnp.asarray((in_x1 | in_x2).astype(np.float32))[None, :]
    m1 = jnp.asarray(-(in_x1.astype(np.float32)))[None, :]
    m2 = jnp.asarray(in_x2.astype(np.float32))[None, :]
    return freq, mc, m1, m2


def _apply_rope(x, c, s1, s2, half):
    n = x.shape[-1]
    reps = n // LANES
    c, s1, s2 = (jnp.tile(t, (1, reps)) if reps > 1 else t for t in (c, s1, s2))
    return x * c + pltpu.roll(x, n - half, 1) * s1 + pltpu.roll(x, half, 1) * s2


def _apply_rope_paired(x, c, s):
    reps = x.shape[-1] // LANES
    rot = [pltpu.roll(x[:, g * LANES:(g + 1) * LANES], LANES // 2, 1) for g in range(reps)]
    rot = jnp.concatenate(rot, axis=1) if reps > 1 else rot[0]
    c, s = (jnp.tile(t, (1, reps)) if reps > 1 else t for t in (c, s))
    return x * c + rot * s


def _store_value_tiles(vt_ref, a):
    row = lax.broadcasted_iota(jnp.int32, (V_ROWS - HEAD_DIM, KV_CHUNK), 0)
    pad = jnp.where(row == 0, 1.0, 0.0).astype(BF16)
    for c in range(a.shape[0] // KV_CHUNK):
        at = a[c * KV_CHUNK:(c + 1) * KV_CHUNK, :].T.astype(BF16)
        for h in range(HEADS):
            vt_ref[0, c, h * V_ROWS:h * V_ROWS + HEAD_DIM, :] = at[h * HEAD_DIM:(h + 1) * HEAD_DIM]
            vt_ref[0, c, h * V_ROWS + HEAD_DIM:(h + 1) * V_ROWS, :] = pad


def _value_tile_spec(tm, S, grid_rank):
    per = S // tm
    imap = (lambda i: (i // per, i % per, 0, 0)) if grid_rank == 1 else (lambda i, j: (i // per, i % per, 0, 0))
    return pl.BlockSpec((1, tm // KV_CHUNK, HEADS * V_ROWS, KV_CHUNK), imap)


def _value_tile_shape(B, S):
    return jax.ShapeDtypeStruct((B, S // KV_CHUNK, HEADS * V_ROWS, KV_CHUNK), BF16)


def _inproj_kernel(x_ref, g_ref, w_ref, c_ref, s1_ref, s2_ref, o_ref, vf_ref, vm_ref):
    h = _rms(x_ref[...], g_ref[...]).astype(BF16)
    c, s1, s2 = c_ref[...], s1_ref[...], s2_ref[...]

    def tile(t):
        return jnp.dot(h, w_ref[:, t * BIG_TN:(t + 1) * BIG_TN], preferred_element_type=F32)

    for t in range(V_TILE):
        a = tile(t) * Q_SCALE if t == 0 else tile(t)
        o_ref[:, t * BIG_TN:t * BIG_TN + BRANCH_WIDTH] = a[:, :BRANCH_WIDTH].astype(BF16)
        o_ref[:, t * BIG_TN + BRANCH_WIDTH:(t + 1) * BIG_TN] = _apply_rope(
            a[:, BRANCH_WIDTH:], c, s1, s2, HEAD_DIM // 2).astype(BF16)
    a = tile(V_TILE)
    _store_value_tiles(vf_ref, a[:, :BRANCH_WIDTH])
    _store_value_tiles(vm_ref, a[:, BRANCH_WIDTH:])
    for t in range(V_TILE + 1, W_TILES):
        o_ref[:, (t - 1) * BIG_TN:t * BIG_TN] = tile(t).astype(BF16)


def _inproj(x, g, w_big, tabs, B, S, tm=ROW_TILE):
    T = x.shape[0]
    tab = pl.BlockSpec((tm, LANES), lambda i: (i, 0))
    vspec = _value_tile_spec(tm, S, 1)
    return pl.pallas_call(
        _inproj_kernel, grid=(T // tm,),
        in_specs=[pl.BlockSpec((tm, D_MODEL), lambda i: (i, 0)),
                  _resident((1, D_MODEL), lambda i: (0, 0)),
                  _resident((D_MODEL, W_TILES * BIG_TN), lambda i: (0, 0)),
                  tab, tab, tab],
        out_specs=[pl.BlockSpec((tm, BIG_COLS), lambda i: (i, 0)), vspec, vspec],
        out_shape=[jax.ShapeDtypeStruct((T, BIG_COLS), BF16), _value_tile_shape(B, S), _value_tile_shape(B, S)],
        compiler_params=_cparams(("parallel",)),
        name="inproj",
    )(x, g, w_big, *tabs)


def _mla_prep_kernel(x_ref, g_ref, ws_ref, gq_ref, gkv_ref, wq_ref, wk_ref, wv_ref,
                     c_ref, s1_ref, s2_ref, q_ref, k_ref, v_ref, fl_ref, *, scale):
    h = _rms(x_ref[...], g_ref[...]).astype(BF16)
    small = jnp.dot(h, ws_ref[...], preferred_element_type=F32)
    c_q = small[:, :MLA_Q_LORA]
    c_kv = small[:, MLA_Q_LORA:MLA_Q_LORA + MLA_KV_LORA]
    k_pe = small[:, MLA_Q_LORA + MLA_KV_LORA:MLA_Q_LORA + MLA_KV_LORA + LANES]
    fl_ref[...] = small[:, SMALL_COLS - LANES:SMALL_COLS - LANES + HEADS]
    c, s = c_ref[...], s1_ref[...] + s2_ref[...]
    qn = _rms(c_q, gq_ref[...]).astype(BF16)
    q = jnp.dot(qn, wq_ref[...], preferred_element_type=F32) * scale
    q_ref[...] = _apply_rope_paired(q, c, s).astype(BF16)
    kvn = _rms(c_kv, gkv_ref[...]).astype(BF16)
    k_nope = jnp.dot(kvn, wk_ref[...], preferred_element_type=F32)
    k_rot = _apply_rope_paired(k_pe, c, s)
    k_ref[...] = (k_nope + jnp.tile(k_rot, (1, HEADS))).astype(BF16)
    _store_value_tiles(v_ref, jnp.dot(kvn, wv_ref[...], preferred_element_type=F32))


def _mla_prep(x, g, w_small, g_q, g_kv, wq, wk, wv, tabs, scale, B, S, tm=LIGHT_ROW_TILE):
    T = x.shape[0]
    full = lambda shape: pl.BlockSpec(shape, lambda i: (0,) * len(shape))
    row = lambda n: pl.BlockSpec((tm, n), lambda i: (i, 0))
    qk = HEADS * LANES
    return pl.pallas_call(
        functools.partial(_mla_prep_kernel, scale=scale), grid=(T // tm,),
        in_specs=[row(D_MODEL), full((1, D_MODEL)), full((D_MODEL, SMALL_COLS)),
                  full((1, MLA_Q_LORA)), full((1, MLA_KV_LORA)),
                  full((MLA_Q_LORA, qk)), full((MLA_KV_LORA, qk)), full((MLA_KV_LORA, BRANCH_WIDTH)),
                  row(LANES), row(LANES), row(LANES)],
        out_specs=[row(qk), row(qk), _value_tile_spec(tm, S, 1), row(HEADS)],
        out_shape=[jax.ShapeDtypeStruct((T, qk), BF16), jax.ShapeDtypeStruct((T, qk), BF16),
                   _value_tile_shape(B, S), jax.ShapeDtypeStruct((T, HEADS), F32)],
        compiler_params=_cparams(("parallel",)),
        name="mla_prep",
    )(x, g, w_small, g_q, g_kv, wq, wk, wv, *tabs)


def _cumlogf_kernel(fl_ref, b_ref, c_ref):
    z = fl_ref[0] + b_ref[...]
    x = jnp.minimum(z, 0.0) - jnp.log1p(jnp.exp(-jnp.abs(z)))
    n = x.shape[-1]
    lane = lax.broadcasted_iota(jnp.int32, x.shape, 1)
    d = 1
    while d < n:
        x = x + jnp.where(lane >= d, pltpu.roll(x, d, 1), 0.0)
        d *= 2
    c_ref[0] = x


def _cumlogf(fl_t, b_col):
    B, H, S = fl_t.shape
    return pl.pallas_call(
        _cumlogf_kernel, grid=(B,),
        in_specs=[pl.BlockSpec((1, H, S), lambda b: (b, 0, 0)), pl.BlockSpec((H, 1), lambda b: (0, 0))],
        out_specs=pl.BlockSpec((1, H, S), lambda b: (b, 0, 0)),
        out_shape=jax.ShapeDtypeStruct((B, H, S), F32),
        compiler_params=_cparams(("parallel",)),
        name="cumlogf",
    )(fl_t, b_col)


def _split3(c):
    hi = c.astype(BF16)
    r = c - hi.astype(F32)
    mid = r.astype(BF16)
    lo = (r - mid.astype(F32)).astype(BF16)
    return hi.astype(F32), mid.astype(F32), lo.astype(F32)


def _fox_prep_kernel(q_ref, k_ref, c_ref, qo_ref, ko_ref):
    tm = q_ref.shape[0]
    lane = lax.broadcasted_iota(jnp.int32, (tm, LANES), 1)
    c = c_ref[...] * LOG2E
    for hp in range(HEADS // 2):
        q2 = q_ref[:, hp * LANES:(hp + 1) * LANES].astype(F32)
        k2 = k_ref[:, hp * LANES:(hp + 1) * LANES].astype(F32)
        for hh in range(2):
            h = 2 * hp + hh
            terms = [jnp.broadcast_to(t, (tm, LANES)) for t in _split3(c[:, h:h + 1])]
            qh = q2 if hh == 0 else pltpu.roll(q2, HEAD_DIM, 1)
            kh = k2 if hh == 0 else pltpu.roll(k2, HEAD_DIM, 1)
            q_aug = jnp.where(lane < HEAD_DIM + 3, 1.0, 0.0)
            k_aug = jnp.where((lane >= HEAD_DIM + 3) & (lane < HEAD_DIM + 6), 1.0, 0.0)
            for n, t in enumerate(terms):
                q_aug = jnp.where(lane == HEAD_DIM + 3 + n, t, q_aug)
                k_aug = jnp.where(lane == HEAD_DIM + n, -t, k_aug)
            q_aug = jnp.where(lane < HEAD_DIM, qh, q_aug)
            k_aug = jnp.where(lane < HEAD_DIM, kh, k_aug)
            qo_ref[:, h * LANES:(h + 1) * LANES] = q_aug.astype(BF16)
            ko_ref[:, h * LANES:(h + 1) * LANES] = k_aug.astype(BF16)


def _fox_prep(big, c_rows, tm=LIGHT_ROW_TILE):
    T = big.shape[0]
    qk = HEADS * LANES
    return pl.pallas_call(
        _fox_prep_kernel, grid=(T // tm,),
        in_specs=[pl.BlockSpec((tm, BRANCH_WIDTH), lambda i: (i, COL_FQ // BRANCH_WIDTH)),
                  pl.BlockSpec((tm, BRANCH_WIDTH), lambda i: (i, COL_FK // BRANCH_WIDTH)),
                  pl.BlockSpec((tm, HEADS), lambda i: (i, 0))],
        out_specs=[pl.BlockSpec((tm, qk), lambda i: (i, 0)), pl.BlockSpec((tm, qk), lambda i: (i, 0))],
        out_shape=[jax.ShapeDtypeStruct((T, qk), BF16), jax.ShapeDtypeStruct((T, qk), BF16)],
        compiler_params=_cparams(("parallel",)),
        name="fox_prep",
    )(big, big, c_rows)


def _nt_dot(a, b):
    return lax.dot_general(a, b, (((1,), (1,)), ((), ())), preferred_element_type=F32)


def _softmax_step(st, m, acc, vt):
    m_new = jnp.maximum(m, jnp.max(st, axis=0, keepdims=True))
    upd = jnp.dot(vt, jnp.exp2(st - m_new).astype(BF16), preferred_element_type=F32)
    return m_new, jnp.exp2(m - m_new) * acc + upd


def _softmax_init(tq):
    return jnp.full((1, tq), M_INIT, F32), jnp.zeros((V_ROWS, tq), F32)


def _softmax_finish(acc):
    return (acc[:HEAD_DIM] / acc[HEAD_DIM:HEAD_DIM + 1]).T


def _attend(npairs, qk, val, past, diag, tq, sa, sb):
    heads = range(len(sa))

    def put(dst, j):
        for hh in heads:
            dst[hh][...] = qk(hh, j)

    def advance(carry, src, j, fn):
        return tuple(_softmax_step(fn(hh, j, src[hh][...]), *carry[hh], val(hh, j)) for hh in heads)

    put(sa, 0)

    def body(jj, carry):
        j0 = 2 * jj
        put(sb, j0 + 1)
        carry = advance(carry, sa, j0, past)
        put(sa, j0 + 2)
        return advance(carry, sb, j0 + 1, past)

    carry = lax.fori_loop(0, npairs, body, tuple(_softmax_init(tq) for _ in heads))
    j0 = 2 * npairs
    put(sb, j0 + 1)
    carry = advance(carry, sa, j0, lambda hh, j, st: diag(hh, 0, st))
    carry = advance(carry, sb, j0 + 1, lambda hh, j, st: diag(hh, 1, st))
    return jnp.concatenate([_softmax_finish(acc) for _, acc in carry], axis=1)


def _score_scratch(tq, tk, nh):
    return [pltpu.VMEM((tk, tq), F32) for _ in range(2 * nh)]


def _flash_kernel(q_ref, k_ref, vt_ref, o_ref, *scores, tq, tk, nh):
    i = pl.program_id(2)
    assert tq == 2 * tk
    krow = lax.broadcasted_iota(jnp.int32, (tk, tq), 0)
    qcol = lax.broadcasted_iota(jnp.int32, (tk, tq), 1)

    def qk(hh, j):
        return _nt_dot(k_ref[0, j, :, hh * LANES:(hh + 1) * LANES], q_ref[0, :, hh * LANES:(hh + 1) * LANES])

    def val(hh, j):
        return vt_ref[0, j, hh * V_ROWS:(hh + 1) * V_ROWS, :]

    def diag(hh, d, st):
        return jnp.where(d * tk + krow <= qcol, st, NEG_INF)

    out = _attend(i, qk, val, lambda hh, j, st: st, diag, tq, scores[:nh], scores[nh:])
    o_ref[0] = out.astype(BF16)


def _flash(q, k, vt, tq, tk, nh=FLASH_HEADS):
    B, S, _ = q.shape
    nk = S // tk
    return pl.pallas_call(
        functools.partial(_flash_kernel, tq=tq, tk=tk, nh=nh), grid=(B, HEADS // nh, S // tq),
        in_specs=[pl.BlockSpec((1, tq, nh * LANES), lambda b, h, i: (b, i, h)),
                  pl.BlockSpec((1, nk, tk, nh * LANES), lambda b, h, i: (b, 0, 0, h)),
                  pl.BlockSpec((1, nk, nh * V_ROWS, tk), lambda b, h, i: (b, 0, h, 0))],
        out_specs=pl.BlockSpec((1, tq, nh * HEAD_DIM), lambda b, h, i: (b, i, h)),
        out_shape=jax.ShapeDtypeStruct((B, S, BRANCH_WIDTH), BF16),
        scratch_shapes=_score_scratch(tq, tk, nh),
        compiler_params=_cparams(("parallel", "parallel", "arbitrary")),
        name="flash",
    )(q, k, vt)


def _moba_kernel(q_ref, k_ref, vt_ref, o_ref, kmean_ref, bias_ref, qm_ref, *scores, nblk, tq, nh):
    i = pl.program_id(2)
    blk = MOBA_BLOCK
    shift = blk.bit_length() - 1
    r = tq // blk

    @pl.when(i == 0)
    def _():
        for n in range(nblk):
            kmean_ref[n:n + 1, :] = jnp.mean(k_ref[0, n].astype(F32), axis=0, keepdims=True)

    lane = lax.broadcasted_iota(jnp.int32, (tq, nh * HEAD_DIM), 1)
    blk_id = lax.broadcasted_iota(jnp.int32, (nblk, tq), 0)
    own = i * r + (lax.broadcasted_iota(jnp.int32, (nblk, tq), 1) >> shift)
    q2 = q_ref[0]
    for hh in range(nh):
        in_head = (lane >= hh * HEAD_DIM) & (lane < (hh + 1) * HEAD_DIM)
        q = jnp.where(in_head, q2, jnp.zeros_like(q2))
        qm_ref[hh] = q
        g3 = _nt_dot(jnp.concatenate([t.astype(BF16) for t in _split3(kmean_ref[...])], axis=0), q)
        g = g3[:nblk] + g3[nblk:2 * nblk] + g3[2 * nblk:]
        g = jnp.where(blk_id < own, g, NEG_INF)
        bias = jnp.full((nblk, tq), NEG_INF, F32)
        for _ in range(MOBA_TOPK):
            mx = jnp.max(g, axis=0, keepdims=True)
            first = jnp.min(jnp.where(g == mx, blk_id, nblk), axis=0, keepdims=True)
            pick = (blk_id == first) & (mx > NEG_INF)
            bias = jnp.where(pick, 0.0, bias)
            g = jnp.where(pick, NEG_INF, g)
        bias_ref[hh] = bias

    krow = lax.broadcasted_iota(jnp.int32, (blk, tq), 0)
    qcol = lax.broadcasted_iota(jnp.int32, (blk, tq), 1)

    def qk(hh, n):
        return _nt_dot(k_ref[0, n], qm_ref[hh])

    def val(hh, n):
        return vt_ref[0, n, hh * V_ROWS:(hh + 1) * V_ROWS, :]

    def past(hh, n, st):
        return st + bias_ref[hh, pl.ds(n, 1), :]

    def diag(hh, d, st):
        own_causal = ((qcol >> shift) == d) & (krow <= (qcol & (blk - 1)))
        return jnp.where(own_causal, st, past(hh, i * r + d, st))

    out = _attend(i, qk, val, past, diag, tq, scores[:nh], scores[nh:])
    o_ref[0] = out.astype(BF16)


def _moba(q, k, vt, tq, nh=MOBA_STEP_HEADS):
    B, S, _ = q.shape
    nblk = S // MOBA_BLOCK
    assert tq == 2 * MOBA_BLOCK and MOBA_BLOCK & (MOBA_BLOCK - 1) == 0
    w = nh * HEAD_DIM
    qc = COL_MQ // w
    kc = COL_MK // w
    return pl.pallas_call(
        functools.partial(_moba_kernel, nblk=nblk, tq=tq, nh=nh), grid=(B, HEADS // nh, S // tq),
        in_specs=[pl.BlockSpec((1, tq, w), lambda b, h, i: (b, i, qc + h)),
                  pl.BlockSpec((1, nblk, MOBA_BLOCK, w), lambda b, h, i: (b, 0, 0, kc + h)),
                  pl.BlockSpec((1, nblk, nh * V_ROWS, MOBA_BLOCK), lambda b, h, i: (b, 0, h, 0))],
        out_specs=pl.BlockSpec((1, tq, w), lambda b, h, i: (b, i, h)),
        out_shape=jax.ShapeDtypeStruct((B, S, BRANCH_WIDTH), BF16),
        scratch_shapes=[pltpu.VMEM((nblk, w), F32), pltpu.VMEM((nh, nblk, tq), F32),
                        pltpu.VMEM((nh, tq, w), BF16)] + _score_scratch(tq, MOBA_BLOCK, nh),
        compiler_params=_cparams(("parallel", "parallel", "arbitrary")),
        name="moba",
    )(q, k, vt)


def _merge_kernel(x_ref, ya_ref, yb_ref, yc_ref, ga_ref, gb_ref, gc_ref, wb_ref, wo_ref, o_ref):
    merged = None
    for n, (y_ref, g_ref) in enumerate(((ya_ref, ga_ref), (yb_ref, gb_ref), (yc_ref, gc_ref))):
        proj = jnp.dot(y_ref[...], wb_ref[n], preferred_element_type=F32)
        term = jax.nn.sigmoid(g_ref[...].astype(F32)) * proj
        merged = term if merged is None else merged + term
    o_ref[...] = x_ref[...] + jnp.dot(merged.astype(BF16), wo_ref[...], preferred_element_type=F32)


def _merge(x, y_mla, y_fox, y_moba, big, w_branch, w_out, tm=ROW_TILE):
    T = x.shape[0]
    g0 = COL_GATES // D_MODEL
    row = lambda n: pl.BlockSpec((tm, n), lambda i: (i, 0))
    gate = lambda n: pl.BlockSpec((tm, D_MODEL), lambda i: (i, g0 + n))
    return pl.pallas_call(
        _merge_kernel, grid=(T // tm,),
        in_specs=[row(D_MODEL), row(BRANCH_WIDTH), row(BRANCH_WIDTH), row(BRANCH_WIDTH),
                  gate(0), gate(1), gate(2),
                  _resident((3, BRANCH_WIDTH, D_MODEL), lambda i: (0, 0, 0)),
                  _resident((D_MODEL, D_MODEL), lambda i: (0, 0))],
        out_specs=row(D_MODEL),
        out_shape=jax.ShapeDtypeStruct((T, D_MODEL), F32),
        compiler_params=_cparams(("parallel",)),
        name="merge",
    )(x, y_mla, y_fox, y_moba, big, big, big, w_branch, w_out)


def _dense_ffn_kernel(x_ref, g_ref, wg_ref, wu_ref, wd_ref, o_ref):
    x = x_ref[...]
    h = _rms(x, g_ref[...]).astype(BF16)
    gate = jnp.dot(h, wg_ref[...], preferred_element_type=F32)
    up = jnp.dot(h, wu_ref[...], preferred_element_type=F32)
    act = (jax.nn.silu(gate) * up).astype(BF16)
    o_ref[...] = x + jnp.dot(act, wd_ref[...], preferred_element_type=F32)


def _dense_ffn(x, g, w_gu, w_down, tm=ROW_TILE):
    T = x.shape[0]
    return pl.pallas_call(
        _dense_ffn_kernel, grid=(T // tm,),
        in_specs=[pl.BlockSpec((tm, D_MODEL), lambda i: (i, 0)),
                  _resident((1, D_MODEL), lambda i: (0, 0)),
                  _resident((D_MODEL, DENSE_FF), lambda i: (0, 0)),
                  _resident((D_MODEL, DENSE_FF), lambda i: (0, 1)),
                  _resident((DENSE_FF, D_MODEL), lambda i: (0, 0))],
        out_specs=pl.BlockSpec((tm, D_MODEL), lambda i: (i, 0)),
        out_shape=jax.ShapeDtypeStruct((T, D_MODEL), F32),
        compiler_params=_cparams(("parallel",)),
        name="dense_ffn",
    )(x, g, w_gu, w_gu, w_down)


def _router_kernel(x_ref, g_ref, wr_ref, h_ref, r_ref):
    h = _rms(x_ref[...], g_ref[...])
    h_ref[...] = h
    logits = jnp.dot(h, wr_ref[...], precision=lax.Precision.HIGHEST, preferred_element_type=F32)
    lane = lax.broadcasted_iota(jnp.int32, logits.shape, 1)
    logits = jnp.where(lane < N_EXPERTS, logits, NEG_INF)
    m1 = jnp.max(logits, axis=-1, keepdims=True)
    i1 = jnp.min(jnp.where(logits == m1, lane, LANES), axis=-1, keepdims=True)
    rest = jnp.where(lane == i1, NEG_INF, logits)
    m2 = jnp.max(rest, axis=-1, keepdims=True)
    i2 = jnp.min(jnp.where(rest == m2, lane, LANES), axis=-1, keepdims=True)
    e2 = jnp.exp(m2 - m1)
    w1 = 1.0 / (1.0 + e2)
    w2 = e2 / (1.0 + e2)
    r_ref[...] = jnp.where(lane == 0, i1.astype(F32), jnp.where(lane == 1, i2.astype(F32),
                           jnp.where(lane == 2, w1, jnp.where(lane == 3, w2, 0.0))))


def _router(x, g, w_router_pad, tm=LIGHT_ROW_TILE):
    T = x.shape[0]
    return pl.pallas_call(
        _router_kernel, grid=(T // tm,),
        in_specs=[pl.BlockSpec((tm, D_MODEL), lambda i: (i, 0)),
                  pl.BlockSpec((1, D_MODEL), lambda i: (0, 0)),
                  pl.BlockSpec((D_MODEL, LANES), lambda i: (0, 0))],
        out_specs=[pl.BlockSpec((tm, D_MODEL), lambda i: (i, 0)), pl.BlockSpec((tm, LANES), lambda i: (i, 0))],
        out_shape=[jax.ShapeDtypeStruct((T, D_MODEL), F32), jax.ShapeDtypeStruct((T, LANES), F32)],
        compiler_params=_cparams(("parallel",)),
        name="router",
    )(x, g, w_router_pad)


GATHER_WINDOW = 128
GATHER_ROWS = 32


def _gather_rows(src, idx):
    M = idx.shape[0]
    C = src.shape[1]
    mesh = plsc.VectorSubcoreMesh(core_axis_name="core", subcore_axis_name="subcore")
    per = M // (mesh.num_cores * mesh.num_subcores)
    assert per * mesh.num_cores * mesh.num_subcores == M and per % GATHER_WINDOW == 0

    @pl.kernel(out_type=jax.ShapeDtypeStruct((M, C), src.dtype), mesh=mesh, name="gather_rows",
               scratch_types=[pltpu.VMEM((GATHER_WINDOW,), jnp.int32), pltpu.VMEM((GATHER_ROWS, C), src.dtype)])
    def gather(x_hbm, i_hbm, o_hbm, idx_v, buf):
        w = lax.axis_index("core") * mesh.num_subcores + lax.axis_index("subcore")

        @pl.loop(0, per // GATHER_WINDOW)
        def _(t):
            base = w * per + t * GATHER_WINDOW
            pltpu.sync_copy(i_hbm.at[pl.ds(base, GATHER_WINDOW)], idx_v)
            for k in range(GATHER_WINDOW // GATHER_ROWS):
                pltpu.sync_copy(x_hbm.at[idx_v.at[pl.ds(k * GATHER_ROWS, GATHER_ROWS)]], buf)
                pltpu.sync_copy(buf, o_hbm.at[pl.ds(base + k * GATHER_ROWS, GATHER_ROWS)])

    return gather(src, idx)


def _scatter_rows(src, dest0, dest1, n_rows):
    T, C = src.shape
    mesh = plsc.VectorSubcoreMesh(core_axis_name="core", subcore_axis_name="subcore")
    per = T // (mesh.num_cores * mesh.num_subcores)
    assert per * mesh.num_cores * mesh.num_subcores == T and per % GATHER_WINDOW == 0

    @pl.kernel(out_type=jax.ShapeDtypeStruct((n_rows, C), src.dtype), mesh=mesh, name="scatter_rows",
               scratch_types=[pltpu.VMEM((GATHER_WINDOW,), jnp.int32), pltpu.VMEM((GATHER_WINDOW,), jnp.int32),
                              pltpu.VMEM((GATHER_ROWS, C), src.dtype)])
    def scatter(x_hbm, d0_hbm, d1_hbm, o_hbm, i0, i1, buf):
        w = lax.axis_index("core") * mesh.num_subcores + lax.axis_index("subcore")

        @pl.loop(0, per // GATHER_WINDOW)
        def _(t):
            base = w * per + t * GATHER_WINDOW
            pltpu.sync_copy(d0_hbm.at[pl.ds(base, GATHER_WINDOW)], i0)
            pltpu.sync_copy(d1_hbm.at[pl.ds(base, GATHER_WINDOW)], i1)
            for k in range(GATHER_WINDOW // GATHER_ROWS):
                pltpu.sync_copy(x_hbm.at[pl.ds(base + k * GATHER_ROWS, GATHER_ROWS)], buf)
                pltpu.sync_copy(buf, o_hbm.at[i0.at[pl.ds(k * GATHER_ROWS, GATHER_ROWS)]])
                pltpu.sync_copy(buf, o_hbm.at[i1.at[pl.ds(k * GATHER_ROWS, GATHER_ROWS)]])

    return scatter(src, dest0, dest1)


EXPERT_SPLIT = 2


def _expert_kernel(be_ref, nused_ref, valid_ref, x_ref, wgu_ref, wd_ref, o_ref):
    b = pl.program_id(0)
    used = b < nused_ref[0]
    tf = EXPERT_FF // EXPERT_SPLIT

    @pl.when(used)
    def _():
        row = lax.broadcasted_iota(jnp.int32, (x_ref.shape[0], 1), 0)
        x = jnp.where(row < valid_ref[b], x_ref[...], 0.0).astype(BF16)
        y = None
        for f in range(EXPERT_SPLIT):
            gate = jnp.dot(x, wgu_ref[0, :, f * tf:(f + 1) * tf], preferred_element_type=F32)
            up = jnp.dot(x, wgu_ref[0, :, EXPERT_FF + f * tf:EXPERT_FF + (f + 1) * tf], preferred_element_type=F32)
            act = (jax.nn.silu(gate) * up).astype(BF16)
            part = jnp.dot(act, wd_ref[0, f * tf:(f + 1) * tf, :], preferred_element_type=F32)
            y = part if y is None else y + part
        o_ref[...] = y

    @pl.when(jnp.logical_not(used))
    def _():
        o_ref[...] = jnp.zeros_like(o_ref)


def _experts(x_rows, block_e, n_used, valid, w_gu, w_down, bm):
    n_rows = x_rows.shape[0]
    grid_spec = pltpu.PrefetchScalarGridSpec(
        num_scalar_prefetch=3, grid=(n_rows // bm,),
        in_specs=[pl.BlockSpec((bm, D_MODEL), lambda b, be, nu, va: (b, 0)),
                  pl.BlockSpec((1, D_MODEL, 2 * EXPERT_FF), lambda b, be, nu, va: (be[b], 0, 0)),
                  pl.BlockSpec((1, EXPERT_FF, D_MODEL), lambda b, be, nu, va: (be[b], 0, 0))],
        out_specs=pl.BlockSpec((bm, D_MODEL), lambda b, be, nu, va: (b, 0)))
    return pl.pallas_call(
        _expert_kernel, grid_spec=grid_spec,
        out_shape=jax.ShapeDtypeStruct((n_rows, D_MODEL), F32),
        compiler_params=_cparams(("arbitrary",), EXPERT_VMEM_LIMIT),
        name="experts",
    )(block_e, n_used, valid, x_rows, w_gu, w_down)


CAST_ROWS = 256


def _cast_kernel(w_ref, o_ref):
    o_ref[...] = w_ref[...].astype(BF16)


def _cast_bf16(w):
    E, R, C = w.shape
    spec = pl.BlockSpec((1, CAST_ROWS, C), lambda e, r: (e, r, 0))
    return pl.pallas_call(
        _cast_kernel, grid=(E, R // CAST_ROWS), in_specs=[spec], out_specs=spec,
        out_shape=jax.ShapeDtypeStruct(w.shape, BF16),
        compiler_params=_cparams(("parallel", "parallel")),
        name="cast_bf16",
    )(w)


def _combine_kernel(x_ref, y0_ref, y1_ref, r_ref, g_ref, o_ref):
    r = r_ref[...]
    x = x_ref[...] + r[:, 2:3] * y0_ref[...] + r[:, 3:4] * y1_ref[...]
    o_ref[...] = _rms(x, g_ref[...])


def _combine(x, y_slots, route, g_final, tm=LIGHT_ROW_TILE):
    T = x.shape[0]
    return pl.pallas_call(
        _combine_kernel, grid=(T // tm,),
        in_specs=[pl.BlockSpec((tm, D_MODEL), lambda i: (i, 0)),
                  pl.BlockSpec((tm, D_MODEL), lambda i: (i, 0)),
                  pl.BlockSpec((tm, D_MODEL), lambda i: (T // tm + i, 0)),
                  pl.BlockSpec((tm, LANES), lambda i: (i, 0)),
                  pl.BlockSpec((1, D_MODEL), lambda i: (0, 0))],
        out_specs=pl.BlockSpec((tm, D_MODEL), lambda i: (i, 0)),
        out_shape=jax.ShapeDtypeStruct((T, D_MODEL), F32),
        compiler_params=_cparams(("parallel",)),
        name="combine",
    )(x, y_slots, y_slots, route, g_final)


MLA_X1_LANE, MLA_X2_LANE = 0, LANES // 2


def _mla_lanes(nope, rope):
    half = MLA_ROPE_DIM // 2
    split = LANES // 2 - half
    pad = jnp.zeros(nope.shape[:-1] + (LANES // 2 - half - (HEAD_DIM - split),), nope.dtype)
    return jnp.concatenate([rope[..., :half], nope[..., :split], rope[..., half:], nope[..., split:], pad], axis=-1)


def _pad_cols(w, n):
    return jnp.pad(w, ((0, 0), (0, n - w.shape[1])))


def _split_in_weights(w_in):
    sizes = [MLA_Q_LORA, MLA_KV_LORA, MLA_ROPE_DIM, BRANCH_WIDTH, BRANCH_WIDTH, BRANCH_WIDTH, HEADS,
             BRANCH_WIDTH, BRANCH_WIDTH, BRANCH_WIDTH, 3 * D_MODEL]
    pts = np.cumsum(sizes)[:-1]
    c_q, c_kv, k_pe, fq, fk, fv, fl, mq, mk, mv, gates = jnp.split(w_in, pts, axis=1)
    pe_tile = _mla_lanes(jnp.zeros((D_MODEL, HEAD_DIM), w_in.dtype), k_pe)
    w_small = jnp.concatenate([c_q, c_kv, pe_tile, _pad_cols(fl, LANES)], axis=1)
    w_big = jnp.concatenate([fq, mq, fk, mk, fv, mv, gates], axis=1)
    return w_small.astype(BF16), w_big.astype(BF16)


def _mla_up_weights(w_uq, w_ukv):
    wq = w_uq.reshape(MLA_Q_LORA, HEADS, HEAD_DIM + MLA_ROPE_DIM)
    wq = _mla_lanes(wq[:, :, :HEAD_DIM], wq[:, :, HEAD_DIM:]).reshape(MLA_Q_LORA, HEADS * LANES)
    wkv = w_ukv.reshape(MLA_KV_LORA, HEADS, 2 * HEAD_DIM)
    wk = _mla_lanes(wkv[:, :, :HEAD_DIM], jnp.zeros((MLA_KV_LORA, HEADS, MLA_ROPE_DIM), w_ukv.dtype))
    wv = wkv[:, :, HEAD_DIM:].reshape(MLA_KV_LORA, BRANCH_WIDTH)
    return wq.astype(BF16), wk.reshape(MLA_KV_LORA, HEADS * LANES).astype(BF16), wv.astype(BF16)


def _chunk_rows(a, B, S, t):
    return a.reshape(B, S // t, t, a.shape[-1])


ATT_TQ = 2 * KV_CHUNK
ATT_TK = KV_CHUNK
MOBA_TQ = 2 * MOBA_BLOCK
MOE_BM = 512


def _token_mixers(x, B, S, g_mix, w_in, g_q_lat, g_kv_lat, w_uq, w_ukv, b_forget, w_branch, w_out,
                  tabs_mla, tabs_moba):
    T = B * S
    w_small, w_big = _split_in_weights(w_in)
    wq, wk, wv = _mla_up_weights(w_uq, w_ukv)
    g = g_mix[None, :]

    big, vt_fox, vt_moba = _inproj(x, g, w_big, tabs_moba, B, S)
    q_mla, k_mla, vt_mla, f_logit = _mla_prep(
        x, g, w_small, g_q_lat[None, :], g_kv_lat[None, :], wq, wk, wv, tabs_mla,
        (HEAD_DIM + MLA_ROPE_DIM) ** -0.5 * LOG2E, B, S)

    y_mla = _flash(q_mla.reshape(B, S, -1), _chunk_rows(k_mla, B, S, ATT_TK), vt_mla, ATT_TQ, ATT_TK)

    c = _cumlogf(f_logit.reshape(B, S, HEADS).transpose(0, 2, 1), b_forget[:, None])
    c_rows = c.transpose(0, 2, 1).reshape(T, HEADS)
    q_fox, k_fox = _fox_prep(big, c_rows)
    y_fox = _flash(q_fox.reshape(B, S, -1), _chunk_rows(k_fox, B, S, ATT_TK), vt_fox, ATT_TQ, ATT_TK)

    y_moba = _moba(big.reshape(B, S, -1), _chunk_rows(big, B, S, MOBA_BLOCK), vt_moba, MOBA_TQ)

    return _merge(x, y_mla.reshape(T, -1), y_fox.reshape(T, -1), y_moba.reshape(T, -1), big,
                  w_branch.astype(BF16), w_out.astype(BF16))


def _moe(x, g_ffn, w_router, w_gu, w_down, g_final):
    T = x.shape[0]
    A = 2 * T
    bm = MOE_BM
    h, route = _router(x, g_ffn[None, :], _pad_cols(w_router, LANES))
    top_e = route[:, :2].astype(jnp.int32).reshape(A)
    onehot = (top_e[:, None] == jnp.arange(N_EXPERTS)[None, :]).astype(jnp.int32)
    csum = jnp.cumsum(onehot, axis=0)
    counts = csum[-1]
    rank = jnp.sum((csum - onehot) * onehot, axis=1)
    padded = (counts + bm - 1) // bm * bm
    pad_end = jnp.cumsum(padded)
    dest = (pad_end - padded)[top_e] + rank
    n_rows = (A // bm + N_EXPERTS) * bm
    block_start = jnp.arange(n_rows // bm) * bm
    block_e = jnp.minimum(jnp.searchsorted(pad_end, block_start, side="right"), N_EXPERTS - 1).astype(jnp.int32)
    n_used = (pad_end[-1:] // bm).astype(jnp.int32)
    valid = jnp.clip((pad_end - padded + counts)[block_e] - block_start, 0, bm).astype(jnp.int32)
    dest_slots = dest.astype(jnp.int32).reshape(T, 2).T

    x_rows = _scatter_rows(h, dest_slots[0], dest_slots[1], n_rows)
    y_rows = _experts(x_rows, block_e, n_used, valid, _cast_bf16(w_gu), _cast_bf16(w_down), bm)
    y_slots = _gather_rows(y_rows, dest_slots.reshape(A))
    return _combine(x, y_slots, route, g_final[None, :])


def kernel(x, positions, g_mix, w_in, g_q_lat, g_kv_lat, w_uq, w_ukv, b_forget, w_branch, w_out, g_ffn,
           w_dense_gu, w_dense_down, w_router, w_exp_gu, w_exp_down, g_final):
    B, S, D = x.shape
    T = B * S
    depth = g_mix.shape[0]
    assert depth == 2 and D == D_MODEL and S % LIGHT_ROW_TILE == 0 and KV_CHUNK == MOBA_BLOCK
    pos = positions.reshape(T, 1).astype(jnp.int32)
    tabs_mla = _rope_tables(pos, *_rope_patterns(LANES, MLA_X1_LANE, MLA_ROPE_DIM // 2, MLA_X2_LANE))
    tabs_moba = _rope_tables(pos, *_rope_patterns(HEAD_DIM, 0, HEAD_DIM // 2))
    x = x.reshape(T, D)
    for l in range(depth):
        x = _token_mixers(x, B, S, g_mix[l], w_in[l], g_q_lat[l], g_kv_lat[l], w_uq[l], w_ukv[l],
                          b_forget[l], w_branch[l], w_out[l], tabs_mla, tabs_moba)
        if l % 2 == 0:
            x = _dense_ffn(x, g_ffn[l][None, :], w_dense_gu[l // 2].astype(BF16),
                           w_dense_down[l // 2].astype(BF16))
        else:
            x = _moe(x, g_ffn[l], w_router[l // 2], w_exp_gu[l // 2], w_exp_down[l // 2], g_final)
    return x.reshape(B, S, D)
```

```python
import functools
import math

import jax
import jax.numpy as jnp
import numpy as np
from jax import lax
from jax.experimental import pallas as pl
from jax.experimental.pallas import tpu as pltpu
from jax.experimental.pallas import tpu_sc as plsc

F32 = jnp.float32
BF16 = jnp.bfloat16
NEG_INF = float("-inf")
M_INIT = -1e30
LOG2E = math.log2(math.e)

D_MODEL = 1024
RMS_EPS = 1e-6
ROPE_THETA = 10000.0
HEADS = 8
HEAD_DIM = 64
V_ROWS = 80
Q_SCALE = HEAD_DIM ** -0.5 * math.log2(math.e)
MLA_Q_LORA = 256
MLA_KV_LORA = 128
MLA_ROPE_DIM = 32
BRANCH_WIDTH = HEADS * HEAD_DIM
MOBA_BLOCK = 256
MOBA_TOPK = 3
DENSE_FF = 2816
N_EXPERTS = 8
EXPERT_FF = 3584

LANES = 128
VMEM_LIMIT = 48 * 1024 * 1024
EXPERT_VMEM_LIMIT = 62 * 1024 * 1024

COL_FQ, COL_MQ, COL_FK, COL_MK, COL_GATES = (n * BRANCH_WIDTH for n in range(5))
BIG_COLS = COL_GATES + 3 * D_MODEL
BIG_TN = 2 * BRANCH_WIDTH
W_TILES = BIG_COLS // BIG_TN + 1
V_TILE = 2
KV_CHUNK = 256
FLASH_HEADS = 4
MOBA_STEP_HEADS = 2
SMALL_COLS = MLA_Q_LORA + MLA_KV_LORA + 2 * LANES


ROW_TILE = 512
LIGHT_ROW_TILE = 1024


def _cparams(sem, vmem_limit=VMEM_LIMIT):
    return pltpu.CompilerParams(dimension_semantics=sem, vmem_limit_bytes=vmem_limit)


def _resident(shape, index_map):
    return pl.BlockSpec(shape, index_map, pipeline_mode=pl.Buffered(1))


def _rms(x, g):
    return x * lax.rsqrt(jnp.mean(x * x, axis=-1, keepdims=True) + RMS_EPS) * g


def _rope_table_kernel(pos_ref, f_ref, mc_ref, m1_ref, m2_ref, c_ref, s1_ref, s2_ref):
    ang = pos_ref[...].astype(F32) * f_ref[...]
    cos = jnp.cos(ang)
    sin = jnp.sin(ang)
    mc = mc_ref[...]
    c_ref[...] = cos * mc + (1.0 - mc)
    s1_ref[...] = sin * m1_ref[...]
    s2_ref[...] = sin * m2_ref[...]


def _rope_tables(pos, freq, mc, m1, m2, tm=LIGHT_ROW_TILE):
    T = pos.shape[0]
    row = pl.BlockSpec((tm, 1), lambda i: (i, 0))
    pat = pl.BlockSpec((1, LANES), lambda i: (0, 0))
    out = pl.BlockSpec((tm, LANES), lambda i: (i, 0))
    shp = jax.ShapeDtypeStruct((T, LANES), F32)
    return pl.pallas_call(
        _rope_table_kernel, grid=(T // tm,),
        in_specs=[row, pat, pat, pat, pat], out_specs=[out, out, out],
        out_shape=[shp, shp, shp], compiler_params=_cparams(("parallel",)),
        name="rope_tables",
    )(pos, freq, mc, m1, m2)


def _rope_patterns(group, x1_lo, half, x2_lo=None):
    d = 2 * half
    x2_lo = x1_lo + half if x2_lo is None else x2_lo
    inv_freq = jnp.exp(-math.log(ROPE_THETA) * jnp.arange(half, dtype=F32) * 2.0 / d)
    lane = np.arange(LANES) % group
    in_x1 = (lane >= x1_lo) & (lane < x1_lo + half)
    in_x2 = (lane >= x2_lo) & (lane < x2_lo + half)
    k = np.where(in_x1, lane - x1_lo, np.where(in_x2, lane - x2_lo, 0))
    freq = jnp.where(jnp.asarray(in_x1 | in_x2), inv_freq[k], 0.0)[None, :].astype(F32)
    mc = jnp.asarray((in_x1 | in_x2).astype(np.float32))[None, :]
    m1 = jnp.asarray(-(in_x1.astype(np.float32)))[None, :]
    m2 = jnp.asarray(in_x2.astype(np.float32))[None, :]
    return freq, mc, m1, m2


def _apply_rope(x, c, s1, s2, half):
    n = x.shape[-1]
    reps = n // LANES
    c, s1, s2 = (jnp.tile(t, (1, reps)) if reps > 1 else t for t in (c, s1, s2))
    return x * c + pltpu.roll(x, n - half, 1) * s1 + pltpu.roll(x, half, 1) * s2


def _apply_rope_paired(x, c, s):
    reps = x.shape[-1] // LANES
    rot = [pltpu.roll(x[:, g * LANES:(g + 1) * LANES], LANES // 2, 1) for g in range(reps)]
    rot = jnp.concatenate(rot, axis=1) if reps > 1 else rot[0]
    c, s = (jnp.tile(t, (1, reps)) if reps > 1 else t for t in (c, s))
    return x * c + rot * s


def _store_value_tiles(vt_ref, a):
    row = lax.broadcasted_iota(jnp.int32, (V_ROWS - HEAD_DIM, KV_CHUNK), 0)
    pad = jnp.where(row == 0, 1.0, 0.0).astype(BF16)
    for c in range(a.shape[0] // KV_CHUNK):
        at = a[c * KV_CHUNK:(c + 1) * KV_CHUNK, :].T.astype(BF16)
        for h in range(HEADS):
            vt_ref[0, c, h * V_ROWS:h * V_ROWS + HEAD_DIM, :] = at[h * HEAD_DIM:(h + 1) * HEAD_DIM]
            vt_ref[0, c, h * V_ROWS + HEAD_DIM:(h + 1) * V_ROWS, :] = pad


def _value_tile_spec(tm, S, grid_rank):
    per = S // tm
    imap = (lambda i: (i // per, i % per, 0, 0)) if grid_rank == 1 else (lambda i, j: (i // per, i % per, 0, 0))
    return pl.BlockSpec((1, tm // KV_CHUNK, HEADS * V_ROWS, KV_CHUNK), imap)


def _value_tile_shape(B, S):
    return jax.ShapeDtypeStruct((B, S // KV_CHUNK, HEADS * V_ROWS, KV_CHUNK), BF16)


def _inproj_kernel(x_ref, g_ref, w_ref, c_ref, s1_ref, s2_ref, o_ref, vf_ref, vm_ref):
    h = _rms(x_ref[...], g_ref[...]).astype(BF16)
    c, s1, s2 = c_ref[...], s1_ref[...], s2_ref[...]

    def tile(t):
        return jnp.dot(h, w_ref[:, t * BIG_TN:(t + 1) * BIG_TN], preferred_element_type=F32)

    for t in range(V_TILE):
        a = tile(t) * Q_SCALE if t == 0 else tile(t)
        o_ref[:, t * BIG_TN:t * BIG_TN + BRANCH_WIDTH] = a[:, :BRANCH_WIDTH].astype(BF16)
        o_ref[:, t * BIG_TN + BRANCH_WIDTH:(t + 1) * BIG_TN] = _apply_rope(
            a[:, BRANCH_WIDTH:], c, s1, s2, HEAD_DIM // 2).astype(BF16)
    a = tile(V_TILE)
    _store_value_tiles(vf_ref, a[:, :BRANCH_WIDTH])
    _store_value_tiles(vm_ref, a[:, BRANCH_WIDTH:])
    for t in range(V_TILE + 1, W_TILES):
        o_ref[:, (t - 1) * BIG_TN:t * BIG_TN] = tile(t).astype(BF16)


def _inproj(x, g, w_big, tabs, B, S, tm=ROW_TILE):
    T = x.shape[0]
    tab = pl.BlockSpec((tm, LANES), lambda i: (i, 0))
    vspec = _value_tile_spec(tm, S, 1)
    return pl.pallas_call(
        _inproj_kernel, grid=(T // tm,),
        in_specs=[pl.BlockSpec((tm, D_MODEL), lambda i: (i, 0)),
                  _resident((1, D_MODEL), lambda i: (0, 0)),
                  _resident((D_MODEL, W_TILES * BIG_TN), lambda i: (0, 0)),
                  tab, tab, tab],
        out_specs=[pl.BlockSpec((tm, BIG_COLS), lambda i: (i, 0)), vspec, vspec],
        out_shape=[jax.ShapeDtypeStruct((T, BIG_COLS), BF16), _value_tile_shape(B, S), _value_tile_shape(B, S)],
        compiler_params=_cparams(("parallel",)),
        name="inproj",
    )(x, g, w_big, *tabs)


def _mla_prep_kernel(x_ref, g_ref, ws_ref, gq_ref, gkv_ref, wq_ref, wk_ref, wv_ref,
                     c_ref, s1_ref, s2_ref, q_ref, k_ref, v_ref, fl_ref, *, scale):
    h = _rms(x_ref[...], g_ref[...]).astype(BF16)
    small = jnp.dot(h, ws_ref[...], preferred_element_type=F32)
    c_q = small[:, :MLA_Q_LORA]
    c_kv = small[:, MLA_Q_LORA:MLA_Q_LORA + MLA_KV_LORA]
    k_pe = small[:, MLA_Q_LORA + MLA_KV_LORA:MLA_Q_LORA + MLA_KV_LORA + LANES]
    fl_ref[...] = small[:, SMALL_COLS - LANES:SMALL_COLS - LANES + HEADS]
    c, s = c_ref[...], s1_ref[...] + s2_ref[...]
    qn = _rms(c_q, gq_ref[...]).astype(BF16)
    q = jnp.dot(qn, wq_ref[...], preferred_element_type=F32) * scale
    q_ref[...] = _apply_rope_paired(q, c, s).astype(BF16)
    kvn = _rms(c_kv, gkv_ref[...]).astype(BF16)
    k_nope = jnp.dot(kvn, wk_ref[...], preferred_element_type=F32)
    k_rot = _apply_rope_paired(k_pe, c, s)
    k_ref[...] = (k_nope + jnp.tile(k_rot, (1, HEADS))).astype(BF16)
    _store_value_tiles(v_ref, jnp.dot(kvn, wv_ref[...], preferred_element_type=F32))


def _mla_prep(x, g, w_small, g_q, g_kv, wq, wk, wv, tabs, scale, B, S, tm=LIGHT_ROW_TILE):
    T = x.shape[0]
    full = lambda shape: pl.BlockSpec(shape, lambda i: (0,) * len(shape))
    row = lambda n: pl.BlockSpec((tm, n), lambda i: (i, 0))
    qk = HEADS * LANES
    return pl.pallas_call(
        functools.partial(_mla_prep_kernel, scale=scale), grid=(T // tm,),
        in_specs=[row(D_MODEL), full((1, D_MODEL)), full((D_MODEL, SMALL_COLS)),
                  full((1, MLA_Q_LORA)), full((1, MLA_KV_LORA)),
                  full((MLA_Q_LORA, qk)), full((MLA_KV_LORA, qk)), full((MLA_KV_LORA, BRANCH_WIDTH)),
                  row(LANES), row(LANES), row(LANES)],
        out_specs=[row(qk), row(qk), _value_tile_spec(tm, S, 1), row(HEADS)],
        out_shape=[jax.ShapeDtypeStruct((T, qk), BF16), jax.ShapeDtypeStruct((T, qk), BF16),
                   _value_tile_shape(B, S), jax.ShapeDtypeStruct((T, HEADS), F32)],
        compiler_params=_cparams(("parallel",)),
        name="mla_prep",
    )(x, g, w_small, g_q, g_kv, wq, wk, wv, *tabs)


def _cumlogf_kernel(fl_ref, b_ref, c_ref):
    z = fl_ref[0] + b_ref[...]
    x = jnp.minimum(z, 0.0) - jnp.log1p(jnp.exp(-jnp.abs(z)))
    n = x.shape[-1]
    lane = lax.broadcasted_iota(jnp.int32, x.shape, 1)
    d = 1
    while d < n:
        x = x + jnp.where(lane >= d, pltpu.roll(x, d, 1), 0.0)
        d *= 2
    c_ref[0] = x


def _cumlogf(fl_t, b_col):
    B, H, S = fl_t.shape
    return pl.pallas_call(
        _cumlogf_kernel, grid=(B,),
        in_specs=[pl.BlockSpec((1, H, S), lambda b: (b, 0, 0)), pl.BlockSpec((H, 1), lambda b: (0, 0))],
        out_specs=pl.BlockSpec((1, H, S), lambda b: (b, 0, 0)),
        out_shape=jax.ShapeDtypeStruct((B, H, S), F32),
        compiler_params=_cparams(("parallel",)),
        name="cumlogf",
    )(fl_t, b_col)


def _split3(c):
    hi = c.astype(BF16)
    r = c - hi.astype(F32)
    mid = r.astype(BF16)
    lo = (r - mid.astype(F32)).astype(BF16)
    return hi.astype(F32), mid.astype(F32), lo.astype(F32)


def _fox_prep_kernel(q_ref, k_ref, c_ref, qo_ref, ko_ref):
    tm = q_ref.shape[0]
    lane = lax.broadcasted_iota(jnp.int32, (tm, LANES), 1)
    c = c_ref[...] * LOG2E
    for hp in range(HEADS // 2):
        q2 = q_ref[:, hp * LANES:(hp + 1) * LANES].astype(F32)
        k2 = k_ref[:, hp * LANES:(hp + 1) * LANES].astype(F32)
        for hh in range(2):
            h = 2 * hp + hh
            terms = [jnp.broadcast_to(t, (tm, LANES)) for t in _split3(c[:, h:h + 1])]
            qh = q2 if hh == 0 else pltpu.roll(q2, HEAD_DIM, 1)
            kh = k2 if hh == 0 else pltpu.roll(k2, HEAD_DIM, 1)
            q_aug = jnp.where(lane < HEAD_DIM + 3, 1.0, 0.0)
            k_aug = jnp.where((lane >= HEAD_DIM + 3) & (lane < HEAD_DIM + 6), 1.0, 0.0)
            for n, t in enumerate(terms):
                q_aug = jnp.where(lane == HEAD_DIM + 3 + n, t, q_aug)
                k_aug = jnp.where(lane == HEAD_DIM + n, -t, k_aug)
            q_aug = jnp.where(lane < HEAD_DIM, qh, q_aug)
            k_aug = jnp.where(lane < HEAD_DIM, kh, k_aug)
            qo_ref[:, h * LANES:(h + 1) * LANES] = q_aug.astype(BF16)
            ko_ref[:, h * LANES:(h + 1) * LANES] = k_aug.astype(BF16)


def _fox_prep(big, c_rows, tm=LIGHT_ROW_TILE):
    T = big.shape[0]
    qk = HEADS * LANES
    return pl.pallas_call(
        _fox_prep_kernel, grid=(T // tm,),
        in_specs=[pl.BlockSpec((tm, BRANCH_WIDTH), lambda i: (i, COL_FQ // BRANCH_WIDTH)),
                  pl.BlockSpec((tm, BRANCH_WIDTH), lambda i: (i, COL_FK // BRANCH_WIDTH)),
                  pl.BlockSpec((tm, HEADS), lambda i: (i, 0))],
        out_specs=[pl.BlockSpec((tm, qk), lambda i: (i, 0)), pl.BlockSpec((tm, qk), lambda i: (i, 0))],
        out_shape=[jax.ShapeDtypeStruct((T, qk), BF16), jax.ShapeDtypeStruct((T, qk), BF16)],
        compiler_params=_cparams(("parallel",)),
        name="fox_prep",
    )(big, big, c_rows)


def _nt_dot(a, b):
    return lax.dot_general(a, b, (((1,), (1,)), ((), ())), preferred_element_type=F32)


def _softmax_step(st, m, acc, vt):
    m_new = jnp.maximum(m, jnp.max(st, axis=0, keepdims=True))
    upd = jnp.dot(vt, jnp.exp2(st - m_new).astype(BF16), preferred_element_type=F32)
    return m_new, jnp.exp2(m - m_new) * acc + upd


def _softmax_init(tq):
    return jnp.full((1, tq), M_INIT, F32), jnp.zeros((V_ROWS, tq), F32)


def _softmax_finish(acc):
    return (acc[:HEAD_DIM] / acc[HEAD_DIM:HEAD_DIM + 1]).T


def _attend(npairs, qk, val, past, diag, tq, sa, sb):
    heads = range(len(sa))

    def put(dst, j):
        for hh in heads:
            dst[hh][...] = qk(hh, j)

    def advance(carry, src, j, fn):
        return tuple(_softmax_step(fn(hh, j, src[hh][...]), *carry[hh], val(hh, j)) for hh in heads)

    put(sa, 0)

    def body(jj, carry):
        j0 = 2 * jj
        put(sb, j0 + 1)
        carry = advance(carry, sa, j0, past)
        put(sa, j0 + 2)
        return advance(carry, sb, j0 + 1, past)

    carry = lax.fori_loop(0, npairs, body, tuple(_softmax_init(tq) for _ in heads))
    j0 = 2 * npairs
    put(sb, j0 + 1)
    carry = advance(carry, sa, j0, lambda hh, j, st: diag(hh, 0, st))
    carry = advance(carry, sb, j0 + 1, lambda hh, j, st: diag(hh, 1, st))
    return jnp.concatenate([_softmax_finish(acc) for _, acc in carry], axis=1)


def _score_scratch(tq, tk, nh):
    return [pltpu.VMEM((tk, tq), F32) for _ in range(2 * nh)]


def _flash_kernel(q_ref, k_ref, vt_ref, o_ref, *scores, tq, tk, nh):
    i = pl.program_id(2)
    assert tq == 2 * tk
    krow = lax.broadcasted_iota(jnp.int32, (tk, tq), 0)
    qcol = lax.broadcasted_iota(jnp.int32, (tk, tq), 1)

    def qk(hh, j):
        return _nt_dot(k_ref[0, j, :, hh * LANES:(hh + 1) * LANES], q_ref[0, :, hh * LANES:(hh + 1) * LANES])

    def val(hh, j):
        return vt_ref[0, j, hh * V_ROWS:(hh + 1) * V_ROWS, :]

    def diag(hh, d, st):
        return jnp.where(d * tk + krow <= qcol, st, NEG_INF)

    out = _attend(i, qk, val, lambda hh, j, st: st, diag, tq, scores[:nh], scores[nh:])
    o_ref[0] = out.astype(BF16)


def _flash(q, k, vt, tq, tk, nh=FLASH_HEADS):
    B, S, _ = q.shape
    nk = S // tk
    return pl.pallas_call(
        functools.partial(_flash_kernel, tq=tq, tk=tk, nh=nh), grid=(B, HEADS // nh, S // tq),
        in_specs=[pl.BlockSpec((1, tq, nh * LANES), lambda b, h, i: (b, i, h)),
                  pl.BlockSpec((1, nk, tk, nh * LANES), lambda b, h, i: (b, 0, 0, h)),
                  pl.BlockSpec((1, nk, nh * V_ROWS, tk), lambda b, h, i: (b, 0, h, 0))],
        out_specs=pl.BlockSpec((1, tq, nh * HEAD_DIM), lambda b, h, i: (b, i, h)),
        out_shape=jax.ShapeDtypeStruct((B, S, BRANCH_WIDTH), BF16),
        scratch_shapes=_score_scratch(tq, tk, nh),
        compiler_params=_cparams(("parallel", "parallel", "arbitrary")),
        name="flash",
    )(q, k, vt)


def _moba_kernel(q_ref, k_ref, vt_ref, o_ref, kmean_ref, bias_ref, qm_ref, *scores, nblk, tq, nh):
    i = pl.program_id(2)
    blk = MOBA_BLOCK
    shift = blk.bit_length() - 1
    r = tq // blk

    @pl.when(i == 0)
    def _():
        for n in range(nblk):
            kmean_ref[n:n + 1, :] = jnp.mean(k_ref[0, n].astype(F32), axis=0, keepdims=True)

    lane = lax.broadcasted_iota(jnp.int32, (tq, nh * HEAD_DIM), 1)
    blk_id = lax.broadcasted_iota(jnp.int32, (nblk, tq), 0)
    own = i * r + (lax.broadcasted_iota(jnp.int32, (nblk, tq), 1) >> shift)
    q2 = q_ref[0]
    for hh in range(nh):
        in_head = (lane >= hh * HEAD_DIM) & (lane < (hh + 1) * HEAD_DIM)
        q = jnp.where(in_head, q2, jnp.zeros_like(q2))
        qm_ref[hh] = q
        g3 = _nt_dot(jnp.concatenate([t.astype(BF16) for t in _split3(kmean_ref[...])], axis=0), q)
        g = g3[:nblk] + g3[nblk:2 * nblk] + g3[2 * nblk:]
        g = jnp.where(blk_id < own, g, NEG_INF)
        bias = jnp.full((nblk, tq), NEG_INF, F32)
        for _ in range(MOBA_TOPK):
            mx = jnp.max(g, axis=0, keepdims=True)
            first = jnp.min(jnp.where(g == mx, blk_id, nblk), axis=0, keepdims=True)
            pick = (blk_id == first) & (mx > NEG_INF)
            bias = jnp.where(pick, 0.0, bias)
            g = jnp.where(pick, NEG_INF, g)
        bias_ref[hh] = bias

    krow = lax.broadcasted_iota(jnp.int32, (blk, tq), 0)
    qcol = lax.broadcasted_iota(jnp.int32, (blk, tq), 1)

    def qk(hh, n):
        return _nt_dot(k_ref[0, n], qm_ref[hh])

    def val(hh, n):
        return vt_ref[0, n, hh * V_ROWS:(hh + 1) * V_ROWS, :]

    def past(hh, n, st):
        return st + bias_ref[hh, pl.ds(n, 1), :]

    def diag(hh, d, st):
        own_causal = ((qcol >> shift) == d) & (krow <= (qcol & (blk - 1)))
        return jnp.where(own_causal, st, past(hh, i * r + d, st))

    out = _attend(i, qk, val, past, diag, tq, scores[:nh], scores[nh:])
    o_ref[0] = out.astype(BF16)


def _moba(q, k, vt, tq, nh=MOBA_STEP_HEADS):
    B, S, _ = q.shape
    nblk = S // MOBA_BLOCK
    assert tq == 2 * MOBA_BLOCK and MOBA_BLOCK & (MOBA_BLOCK - 1) == 0
    w = nh * HEAD_DIM
    qc = COL_MQ // w
    kc = COL_MK // w
    return pl.pallas_call(
        functools.partial(_moba_kernel, nblk=nblk, tq=tq, nh=nh), grid=(B, HEADS // nh, S // tq),
        in_specs=[pl.BlockSpec((1, tq, w), lambda b, h, i: (b, i, qc + h)),
                  pl.BlockSpec((1, nblk, MOBA_BLOCK, w), lambda b, h, i: (b, 0, 0, kc + h)),
                  pl.BlockSpec((1, nblk, nh * V_ROWS, MOBA_BLOCK), lambda b, h, i: (b, 0, h, 0))],
        out_specs=pl.BlockSpec((1, tq, w), lambda b, h, i: (b, i, h)),
        out_shape=jax.ShapeDtypeStruct((B, S, BRANCH_WIDTH), BF16),
        scratch_shapes=[pltpu.VMEM((nblk, w), F32), pltpu.VMEM((nh, nblk, tq), F32),
                        pltpu.VMEM((nh, tq, w), BF16)] + _score_scratch(tq, MOBA_BLOCK, nh),
        compiler_params=_cparams(("parallel", "parallel", "arbitrary")),
        name="moba",
    )(q, k, vt)


def _merge_kernel(x_ref, ya_ref, yb_ref, yc_ref, ga_ref, gb_ref, gc_ref, wb_ref, wo_ref, o_ref):
    merged = None
    for n, (y_ref, g_ref) in enumerate(((ya_ref, ga_ref), (yb_ref, gb_ref), (yc_ref, gc_ref))):
        proj = jnp.dot(y_ref[...], wb_ref[n], preferred_element_type=F32)
        term = jax.nn.sigmoid(g_ref[...].astype(F32)) * proj
        merged = term if merged is None else merged + term
    o_ref[...] = x_ref[...] + jnp.dot(merged.astype(BF16), wo_ref[...], preferred_element_type=F32)


def _merge(x, y_mla, y_fox, y_moba, big, w_branch, w_out, tm=ROW_TILE):
    T = x.shape[0]
    g0 = COL_GATES // D_MODEL
    row = lambda n: pl.BlockSpec((tm, n), lambda i: (i, 0))
    gate = lambda n: pl.BlockSpec((tm, D_MODEL), lambda i: (i, g0 + n))
    return pl.pallas_call(
        _merge_kernel, grid=(T // tm,),
        in_specs=[row(D_MODEL), row(BRANCH_WIDTH), row(BRANCH_WIDTH), row(BRANCH_WIDTH),
                  gate(0), gate(1), gate(2),
                  _resident((3, BRANCH_WIDTH, D_MODEL), lambda i: (0, 0, 0)),
                  _resident((D_MODEL, D_MODEL), lambda i: (0, 0))],
        out_specs=row(D_MODEL),
        out_shape=jax.ShapeDtypeStruct((T, D_MODEL), F32),
        compiler_params=_cparams(("parallel",)),
        name="merge",
    )(x, y_mla, y_fox, y_moba, big, big, big, w_branch, w_out)


def _dense_ffn_kernel(x_ref, g_ref, wg_ref, wu_ref, wd_ref, o_ref):
    x = x_ref[...]
    h = _rms(x, g_ref[...]).astype(BF16)
    gate = jnp.dot(h, wg_ref[...], preferred_element_type=F32)
    up = jnp.dot(h, wu_ref[...], preferred_element_type=F32)
    act = (jax.nn.silu(gate) * up).astype(BF16)
    o_ref[...] = x + jnp.dot(act, wd_ref[...], preferred_element_type=F32)


def _dense_ffn(x, g, w_gu, w_down, tm=ROW_TILE):
    T = x.shape[0]
    return pl.pallas_call(
        _dense_ffn_kernel, grid=(T // tm,),
        in_specs=[pl.BlockSpec((tm, D_MODEL), lambda i: (i, 0)),
                  _resident((1, D_MODEL), lambda i: (0, 0)),
                  _resident((D_MODEL, DENSE_FF), lambda i: (0, 0)),
                  _resident((D_MODEL, DENSE_FF), lambda i: (0, 1)),
                  _resident((DENSE_FF, D_MODEL), lambda i: (0, 0))],
        out_specs=pl.BlockSpec((tm, D_MODEL), lambda i: (i, 0)),
        out_shape=jax.ShapeDtypeStruct((T, D_MODEL), F32),
        compiler_params=_cparams(("parallel",)),
        name="dense_ffn",
    )(x, g, w_gu, w_gu, w_down)


def _router_kernel(x_ref, g_ref, wr_ref, h_ref, r_ref):
    h = _rms(x_ref[...], g_ref[...])
    h_ref[...] = h
    logits = jnp.dot(h, wr_ref[...], precision=lax.Precision.HIGHEST, preferred_element_type=F32)
    lane = lax.broadcasted_iota(jnp.int32, logits.shape, 1)
    logits = jnp.where(lane < N_EXPERTS, logits, NEG_INF)
    m1 = jnp.max(logits, axis=-1, keepdims=True)
    i1 = jnp.min(jnp.where(logits == m1, lane, LANES), axis=-1, keepdims=True)
    rest = jnp.where(lane == i1, NEG_INF, logits)
    m2 = jnp.max(rest, axis=-1, keepdims=True)
    i2 = jnp.min(jnp.where(rest == m2, lane, LANES), axis=-1, keepdims=True)
    e2 = jnp.exp(m2 - m1)
    w1 = 1.0 / (1.0 + e2)
    w2 = e2 / (1.0 + e2)
    r_ref[...] = jnp.where(lane == 0, i1.astype(F32), jnp.where(lane == 1, i2.astype(F32),
                           jnp.where(lane == 2, w1, jnp.where(lane == 3, w2, 0.0))))


def _router(x, g, w_router_pad, tm=LIGHT_ROW_TILE):
    T = x.shape[0]
    return pl.pallas_call(
        _router_kernel, grid=(T // tm,),
        in_specs=[pl.BlockSpec((tm, D_MODEL), lambda i: (i, 0)),
                  pl.BlockSpec((1, D_MODEL), lambda i: (0, 0)),
                  pl.BlockSpec((D_MODEL, LANES), lambda i: (0, 0))],
        out_specs=[pl.BlockSpec((tm, D_MODEL), lambda i: (i, 0)), pl.BlockSpec((tm, LANES), lambda i: (i, 0))],
        out_shape=[jax.ShapeDtypeStruct((T, D_MODEL), F32), jax.ShapeDtypeStruct((T, LANES), F32)],
        compiler_params=_cparams(("parallel",)),
        name="router",
    )(x, g, w_router_pad)


GATHER_WINDOW = 128
GATHER_ROWS = 64


def _gather_rows(src, idx):
    M = idx.shape[0]
    C = src.shape[1]
    mesh = plsc.VectorSubcoreMesh(core_axis_name="core", subcore_axis_name="subcore")
    per = M // (mesh.num_cores * mesh.num_subcores)
    assert per * mesh.num_cores * mesh.num_subcores == M and per % GATHER_WINDOW == 0

    @pl.kernel(out_type=jax.ShapeDtypeStruct((M, C), src.dtype), mesh=mesh, name="gather_rows",
               scratch_types=[pltpu.VMEM((GATHER_WINDOW,), jnp.int32), pltpu.VMEM((GATHER_ROWS, C), src.dtype)])
    def gather(x_hbm, i_hbm, o_hbm, idx_v, buf):
        w = lax.axis_index("core") * mesh.num_subcores + lax.axis_index("subcore")

        @pl.loop(0, per // GATHER_WINDOW)
        def _(t):
            base = w * per + t * GATHER_WINDOW
            pltpu.sync_copy(i_hbm.at[pl.ds(base, GATHER_WINDOW)], idx_v)
            for k in range(GATHER_WINDOW // GATHER_ROWS):
                pltpu.sync_copy(x_hbm.at[idx_v.at[pl.ds(k * GATHER_ROWS, GATHER_ROWS)]], buf)
                pltpu.sync_copy(buf, o_hbm.at[pl.ds(base + k * GATHER_ROWS, GATHER_ROWS)])

    return gather(src, idx)


def _scatter_rows(src, dest0, dest1, n_rows):
    T, C = src.shape
    mesh = plsc.VectorSubcoreMesh(core_axis_name="core", subcore_axis_name="subcore")
    per = T // (mesh.num_cores * mesh.num_subcores)
    assert per * mesh.num_cores * mesh.num_subcores == T and per % GATHER_WINDOW == 0

    @pl.kernel(out_type=jax.ShapeDtypeStruct((n_rows, C), src.dtype), mesh=mesh, name="scatter_rows",
               scratch_types=[pltpu.VMEM((GATHER_WINDOW,), jnp.int32), pltpu.VMEM((GATHER_WINDOW,), jnp.int32),
                              pltpu.VMEM((GATHER_ROWS, C), src.dtype)])
    def scatter(x_hbm, d0_hbm, d1_hbm, o_hbm, i0, i1, buf):
        w = lax.axis_index("core") * mesh.num_subcores + lax.axis_index("subcore")

        @pl.loop(0, per // GATHER_WINDOW)
        def _(t):
            base = w * per + t * GATHER_WINDOW
            pltpu.sync_copy(d0_hbm.at[pl.ds(base, GATHER_WINDOW)], i0)
            pltpu.sync_copy(d1_hbm.at[pl.ds(base, GATHER_WINDOW)], i1)
            for k in range(GATHER_WINDOW // GATHER_ROWS):
                pltpu.sync_copy(x_hbm.at[pl.ds(base + k * GATHER_ROWS, GATHER_ROWS)], buf)
                pltpu.sync_copy(buf, o_hbm.at[i0.at[pl.ds(k * GATHER_ROWS, GATHER_ROWS)]])
                pltpu.sync_copy(buf, o_hbm.at[i1.at[pl.ds(k * GATHER_ROWS, GATHER_ROWS)]])

    return scatter(src, dest0, dest1)


EXPERT_SPLIT = 2


def _expert_kernel(be_ref, nused_ref, valid_ref, x_ref, wgu_ref, wd_ref, o_ref):
    b = pl.program_id(0)
    used = b < nused_ref[0]
    tf = EXPERT_FF // EXPERT_SPLIT

    @pl.when(used)
    def _():
        row = lax.broadcasted_iota(jnp.int32, (x_ref.shape[0], 1), 0)
        x = jnp.where(row < valid_ref[b], x_ref[...], 0.0).astype(BF16)
        y = None
        for f in range(EXPERT_SPLIT):
            gate = jnp.dot(x, wgu_ref[0, :, f * tf:(f + 1) * tf], preferred_element_type=F32)
            up = jnp.dot(x, wgu_ref[0, :, EXPERT_FF + f * tf:EXPERT_FF + (f + 1) * tf], preferred_element_type=F32)
            act = (jax.nn.silu(gate) * up).astype(BF16)
            part = jnp.dot(act, wd_ref[0, f * tf:(f + 1) * tf, :], preferred_element_type=F32)
            y = part if y is None else y + part
        o_ref[...] = y

    @pl.when(jnp.logical_not(used))
    def _():
        o_ref[...] = jnp.zeros_like(o_ref)


def _experts(x_rows, block_e, n_used, valid, w_gu, w_down, bm):
    n_rows = x_rows.shape[0]
    grid_spec = pltpu.PrefetchScalarGridSpec(
        num_scalar_prefetch=3, grid=(n_rows // bm,),
        in_specs=[pl.BlockSpec((bm, D_MODEL), lambda b, be, nu, va: (b, 0)),
                  pl.BlockSpec((1, D_MODEL, 2 * EXPERT_FF), lambda b, be, nu, va: (be[b], 0, 0)),
                  pl.BlockSpec((1, EXPERT_FF, D_MODEL), lambda b, be, nu, va: (be[b], 0, 0))],
        out_specs=pl.BlockSpec((bm, D_MODEL), lambda b, be, nu, va: (b, 0)))
    return pl.pallas_call(
        _expert_kernel, grid_spec=grid_spec,
        out_shape=jax.ShapeDtypeStruct((n_rows, D_MODEL), F32),
        compiler_params=_cparams(("arbitrary",), EXPERT_VMEM_LIMIT),
        name="experts",
    )(block_e, n_used, valid, x_rows, w_gu, w_down)


def _combine_kernel(x_ref, y0_ref, y1_ref, r_ref, g_ref, o_ref):
    r = r_ref[...]
    x = x_ref[...] + r[:, 2:3] * y0_ref[...] + r[:, 3:4] * y1_ref[...]
    o_ref[...] = _rms(x, g_ref[...])


def _combine(x, y_slots, route, g_final, tm=LIGHT_ROW_TILE):
    T = x.shape[0]
    return pl.pallas_call(
        _combine_kernel, grid=(T // tm,),
        in_specs=[pl.BlockSpec((tm, D_MODEL), lambda i: (i, 0)),
                  pl.BlockSpec((tm, D_MODEL), lambda i: (i, 0)),
                  pl.BlockSpec((tm, D_MODEL), lambda i: (T // tm + i, 0)),
                  pl.BlockSpec((tm, LANES), lambda i: (i, 0)),
                  pl.BlockSpec((1, D_MODEL), lambda i: (0, 0))],
        out_specs=pl.BlockSpec((tm, D_MODEL), lambda i: (i, 0)),
        out_shape=jax.ShapeDtypeStruct((T, D_MODEL), F32),
        compiler_params=_cparams(("parallel",)),
        name="combine",
    )(x, y_slots, y_slots, route, g_final)


MLA_X1_LANE, MLA_X2_LANE = 0, LANES // 2


def _mla_lanes(nope, rope):
    half = MLA_ROPE_DIM // 2
    split = LANES // 2 - half
    pad = jnp.zeros(nope.shape[:-1] + (LANES // 2 - half - (HEAD_DIM - split),), nope.dtype)
    return jnp.concatenate([rope[..., :half], nope[..., :split], rope[..., half:], nope[..., split:], pad], axis=-1)


def _pad_cols(w, n):
    return jnp.pad(w, ((0, 0), (0, n - w.shape[1])))


def _split_in_weights(w_in):
    sizes = [MLA_Q_LORA, MLA_KV_LORA, MLA_ROPE_DIM, BRANCH_WIDTH, BRANCH_WIDTH, BRANCH_WIDTH, HEADS,
             BRANCH_WIDTH, BRANCH_WIDTH, BRANCH_WIDTH, 3 * D_MODEL]
    pts = np.cumsum(sizes)[:-1]
    c_q, c_kv, k_pe, fq, fk, fv, fl, mq, mk, mv, gates = jnp.split(w_in, pts, axis=1)
    pe_tile = _mla_lanes(jnp.zeros((D_MODEL, HEAD_DIM), w_in.dtype), k_pe)
    w_small = jnp.concatenate([c_q, c_kv, pe_tile, _pad_cols(fl, LANES)], axis=1)
    w_big = jnp.concatenate([fq, mq, fk, mk, fv, mv, gates], axis=1)
    return w_small.astype(BF16), w_big.astype(BF16)


def _mla_up_weights(w_uq, w_ukv):
    wq = w_uq.reshape(MLA_Q_LORA, HEADS, HEAD_DIM + MLA_ROPE_DIM)
    wq = _mla_lanes(wq[:, :, :HEAD_DIM], wq[:, :, HEAD_DIM:]).reshape(MLA_Q_LORA, HEADS * LANES)
    wkv = w_ukv.reshape(MLA_KV_LORA, HEADS, 2 * HEAD_DIM)
    wk = _mla_lanes(wkv[:, :, :HEAD_DIM], jnp.zeros((MLA_KV_LORA, HEADS, MLA_ROPE_DIM), w_ukv.dtype))
    wv = wkv[:, :, HEAD_DIM:].reshape(MLA_KV_LORA, BRANCH_WIDTH)
    return wq.astype(BF16), wk.reshape(MLA_KV_LORA, HEADS * LANES).astype(BF16), wv.astype(BF16)


def _chunk_rows(a, B, S, t):
    return a.reshape(B, S // t, t, a.shape[-1])


ATT_TQ = 2 * KV_CHUNK
ATT_TK = KV_CHUNK
MOBA_TQ = 2 * MOBA_BLOCK
MOE_BM = 512


def _token_mixers(x, B, S, g_mix, w_in, g_q_lat, g_kv_lat, w_uq, w_ukv, b_forget, w_branch, w_out,
                  tabs_mla, tabs_moba):
    T = B * S
    w_small, w_big = _split_in_weights(w_in)
    wq, wk, wv = _mla_up_weights(w_uq, w_ukv)
    g = g_mix[None, :]

    big, vt_fox, vt_moba = _inproj(x, g, w_big, tabs_moba, B, S)
    q_mla, k_mla, vt_mla, f_logit = _mla_prep(
        x, g, w_small, g_q_lat[None, :], g_kv_lat[None, :], wq, wk, wv, tabs_mla,
        (HEAD_DIM + MLA_ROPE_DIM) ** -0.5 * LOG2E, B, S)

    y_mla = _flash(q_mla.reshape(B, S, -1), _chunk_rows(k_mla, B, S, ATT_TK), vt_mla, ATT_TQ, ATT_TK)

    c = _cumlogf(f_logit.reshape(B, S, HEADS).transpose(0, 2, 1), b_forget[:, None])
    c_rows = c.transpose(0, 2, 1).reshape(T, HEADS)
    q_fox, k_fox = _fox_prep(big, c_rows)
    y_fox = _flash(q_fox.reshape(B, S, -1), _chunk_rows(k_fox, B, S, ATT_TK), vt_fox, ATT_TQ, ATT_TK)

    y_moba = _moba(big.reshape(B, S, -1), _chunk_rows(big, B, S, MOBA_BLOCK), vt_moba, MOBA_TQ)

    return _merge(x, y_mla.reshape(T, -1), y_fox.reshape(T, -1), y_moba.reshape(T, -1), big,
                  w_branch.astype(BF16), w_out.astype(BF16))


def _moe(x, g_ffn, w_router, w_gu, w_down, g_final):
    T = x.shape[0]
    A = 2 * T
    bm = MOE_BM
    h, route = _router(x, g_ffn[None, :], _pad_cols(w_router, LANES))
    top_e = route[:, :2].astype(jnp.int32).reshape(A)
    onehot = (top_e[:, None] == jnp.arange(N_EXPERTS)[None, :]).astype(jnp.int32)
    csum = jnp.cumsum(onehot, axis=0)
    counts = csum[-1]
    rank = jnp.sum((csum - onehot) * onehot, axis=1)
    padded = (counts + bm - 1) // bm * bm
    pad_end = jnp.cumsum(padded)
    dest = (pad_end - padded)[top_e] + rank
    n_rows = (A // bm + N_EXPERTS) * bm
    block_start = jnp.arange(n_rows // bm) * bm
    block_e = jnp.minimum(jnp.searchsorted(pad_end, block_start, side="right"), N_EXPERTS - 1).astype(jnp.int32)
    n_used = (pad_end[-1:] // bm).astype(jnp.int32)
    valid = jnp.clip((pad_end - padded + counts)[block_e] - block_start, 0, bm).astype(jnp.int32)
    dest_slots = dest.astype(jnp.int32).reshape(T, 2).T

    x_rows = _scatter_rows(h, dest_slots[0], dest_slots[1], n_rows)
    y_rows = _experts(x_rows, block_e, n_used, valid, w_gu.astype(BF16), w_down.astype(BF16), bm)
    y_slots = _gather_rows(y_rows, dest_slots.reshape(A))
    return _combine(x, y_slots, route, g_final[None, :])


def kernel(x, positions, g_mix, w_in, g_q_lat, g_kv_lat, w_uq, w_ukv, b_forget, w_branch, w_out, g_ffn,
           w_dense_gu, w_dense_down, w_router, w_exp_gu, w_exp_down, g_final):
    B, S, D = x.shape
    T = B * S
    depth = g_mix.shape[0]
    assert depth == 2 and D == D_MODEL and S % LIGHT_ROW_TILE == 0 and KV_CHUNK == MOBA_BLOCK
    pos = positions.reshape(T, 1).astype(jnp.int32)
    tabs_mla = _rope_tables(pos, *_rope_patterns(LANES, MLA_X1_LANE, MLA_ROPE_DIM // 2, MLA_X2_LANE))
    tabs_moba = _rope_tables(pos, *_rope_patterns(HEAD_DIM, 0, HEAD_DIM // 2))
    x = x.reshape(T, D)
    for l in range(depth):
        x = _token_mixers(x, B, S, g_mix[l], w_in[l], g_q_lat[l], g_kv_lat[l], w_uq[l], w_ukv[l],
                          b_forget[l], w_branch[l], w_out[l], tabs_mla, tabs_moba)
        if l % 2 == 0:
            x = _dense_ffn(x, g_ffn[l][None, :], w_dense_gu[l // 2].astype(BF16),
                           w_dense_down[l // 2].astype(BF16))
        else:
            x = _moe(x, g_ffn[l], w_router[l // 2], w_exp_gu[l // 2], w_exp_down[l // 2], g_final)
    return x.reshape(B, S, D)
```

```python
import functools
import math

import jax
import jax.numpy as jnp
import numpy as np
from jax import lax
from jax.experimental import pallas as pl
from jax.experimental.pallas import tpu as pltpu
from jax.experimental.pallas import tpu_sc as plsc

F32 = jnp.float32
BF16 = jnp.bfloat16
NEG_INF = float("-inf")
M_INIT = -1e30
LOG2E = math.log2(math.e)

D_MODEL = 1024
RMS_EPS = 1e-6
ROPE_THETA = 10000.0
HEADS = 8
HEAD_DIM = 64
V_ROWS = 80
Q_SCALE = HEAD_DIM ** -0.5 * math.log2(math.e)
MLA_Q_LORA = 256
MLA_KV_LORA = 128
MLA_ROPE_DIM = 32
BRANCH_WIDTH = HEADS * HEAD_DIM
MOBA_BLOCK = 256
MOBA_TOPK = 3
DENSE_FF = 2816
N_EXPERTS = 8
EXPERT_FF = 3584

LANES = 128
VMEM_LIMIT = 48 * 1024 * 1024
EXPERT_VMEM_LIMIT = 62 * 1024 * 1024

COL_FQ, COL_MQ, COL_FK, COL_MK, COL_GATES = (n * BRANCH_WIDTH for n in range(5))
BIG_COLS = COL_GATES + 3 * D_MODEL
BIG_TN = 2 * BRANCH_WIDTH
W_TILES = BIG_COLS // BIG_TN + 1
V_TILE = 2
KV_CHUNK = 256
FLASH_HEADS = 4
MOBA_STEP_HEADS = 2
SMALL_COLS = MLA_Q_LORA + MLA_KV_LORA + 2 * LANES


ROW_TILE = 512
LIGHT_ROW_TILE = 1024


def _cparams(sem, vmem_limit=VMEM_LIMIT):
    return pltpu.CompilerParams(dimension_semantics=sem, vmem_limit_bytes=vmem_limit)


def _resident(shape, index_map):
    return pl.BlockSpec(shape, index_map, pipeline_mode=pl.Buffered(1))


def _rms(x, g):
    return x * lax.rsqrt(jnp.mean(x * x, axis=-1, keepdims=True) + RMS_EPS) * g


def _rope_table_kernel(pos_ref, f_ref, mc_ref, m1_ref, m2_ref, c_ref, s1_ref, s2_ref):
    ang = pos_ref[...].astype(F32) * f_ref[...]
    cos = jnp.cos(ang)
    sin = jnp.sin(ang)
    mc = mc_ref[...]
    c_ref[...] = cos * mc + (1.0 - mc)
    s1_ref[...] = sin * m1_ref[...]
    s2_ref[...] = sin * m2_ref[...]


def _rope_tables(pos, freq, mc, m1, m2, tm=LIGHT_ROW_TILE):
    T = pos.shape[0]
    row = pl.BlockSpec((tm, 1), lambda i: (i, 0))
    pat = pl.BlockSpec((1, LANES), lambda i: (0, 0))
    out = pl.BlockSpec((tm, LANES), lambda i: (i, 0))
    shp = jax.ShapeDtypeStruct((T, LANES), F32)
    return pl.pallas_call(
        _rope_table_kernel, grid=(T // tm,),
        in_specs=[row, pat, pat, pat, pat], out_specs=[out, out, out],
        out_shape=[shp, shp, shp], compiler_params=_cparams(("parallel",)),
        name="rope_tables",
    )(pos, freq, mc, m1, m2)


def _rope_patterns(group, x1_lo, half, x2_lo=None):
    d = 2 * half
    x2_lo = x1_lo + half if x2_lo is None else x2_lo
    inv_freq = jnp.exp(-math.log(ROPE_THETA) * jnp.arange(half, dtype=F32) * 2.0 / d)
    lane = np.arange(LANES) % group
    in_x1 = (lane >= x1_lo) & (lane < x1_lo + half)
    in_x2 = (lane >= x2_lo) & (lane < x2_lo + half)
    k = np.where(in_x1, lane - x1_lo, np.where(in_x2, lane - x2_lo, 0))
    freq = jnp.where(jnp.asarray(in_x1 | in_x2), inv_freq[k], 0.0)[None, :].astype(F32)
    mc = jnp.asarray((in_x1 | in_x2).astype(np.float32))[None, :]
    m1 = jnp.asarray(-(in_x1.astype(np.float32)))[None, :]
    m2 = jnp.asarray(in_x2.astype(np.float32))[None, :]
    return freq, mc, m1, m2


def _apply_rope(x, c, s1, s2, half):
    n = x.shape[-1]
    reps = n // LANES
    c, s1, s2 = (jnp.tile(t, (1, reps)) if reps > 1 else t for t in (c, s1, s2))
    return x * c + pltpu.roll(x, n - half, 1) * s1 + pltpu.roll(x, half, 1) * s2


def _apply_rope_paired(x, c, s):
    reps = x.shape[-1] // LANES
    rot = [pltpu.roll(x[:, g * LANES:(g + 1) * LANES], LANES // 2, 1) for g in range(reps)]
    rot = jnp.concatenate(rot, axis=1) if reps > 1 else rot[0]
    c, s = (jnp.tile(t, (1, reps)) if reps > 1 else t for t in (c, s))
    return x * c + rot * s


def _store_value_tiles(vt_ref, a):
    row = lax.broadcasted_iota(jnp.int32, (V_ROWS - HEAD_DIM, KV_CHUNK), 0)
    pad = jnp.where(row == 0, 1.0, 0.0).astype(BF16)
    for c in range(a.shape[0] // KV_CHUNK):
        at = a[c * KV_CHUNK:(c + 1) * KV_CHUNK, :].T.astype(BF16)
        for h in range(HEADS):
            vt_ref[0, c, h * V_ROWS:h * V_ROWS + HEAD_DIM, :] = at[h * HEAD_DIM:(h + 1) * HEAD_DIM]
            vt_ref[0, c, h * V_ROWS + HEAD_DIM:(h + 1) * V_ROWS, :] = pad


def _value_tile_spec(tm, S, grid_rank):
    per = S // tm
    imap = (lambda i: (i // per, i % per, 0, 0)) if grid_rank == 1 else (lambda i, j: (i // per, i % per, 0, 0))
    return pl.BlockSpec((1, tm // KV_CHUNK, HEADS * V_ROWS, KV_CHUNK), imap)


def _value_tile_shape(B, S):
    return jax.ShapeDtypeStruct((B, S // KV_CHUNK, HEADS * V_ROWS, KV_CHUNK), BF16)


def _inproj_kernel(x_ref, g_ref, w_ref, c_ref, s1_ref, s2_ref, o_ref, vf_ref, vm_ref):
    h = _rms(x_ref[...], g_ref[...]).astype(BF16)
    c, s1, s2 = c_ref[...], s1_ref[...], s2_ref[...]

    def tile(t):
        return jnp.dot(h, w_ref[:, t * BIG_TN:(t + 1) * BIG_TN], preferred_element_type=F32)

    for t in range(V_TILE):
        a = tile(t) * Q_SCALE if t == 0 else tile(t)
        o_ref[:, t * BIG_TN:t * BIG_TN + BRANCH_WIDTH] = a[:, :BRANCH_WIDTH].astype(BF16)
        o_ref[:, t * BIG_TN + BRANCH_WIDTH:(t + 1) * BIG_TN] = _apply_rope(
            a[:, BRANCH_WIDTH:], c, s1, s2, HEAD_DIM // 2).astype(BF16)
    a = tile(V_TILE)
    _store_value_tiles(vf_ref, a[:, :BRANCH_WIDTH])
    _store_value_tiles(vm_ref, a[:, BRANCH_WIDTH:])
    for t in range(V_TILE + 1, W_TILES):
        o_ref[:, (t - 1) * BIG_TN:t * BIG_TN] = tile(t).astype(BF16)


def _inproj(x, g, w_big, tabs, B, S, tm=ROW_TILE):
    T = x.shape[0]
    tab = pl.BlockSpec((tm, LANES), lambda i: (i, 0))
    vspec = _value_tile_spec(tm, S, 1)
    return pl.pallas_call(
        _inproj_kernel, grid=(T // tm,),
        in_specs=[pl.BlockSpec((tm, D_MODEL), lambda i: (i, 0)),
                  _resident((1, D_MODEL), lambda i: (0, 0)),
                  _resident((D_MODEL, W_TILES * BIG_TN), lambda i: (0, 0)),
                  tab, tab, tab],
        out_specs=[pl.BlockSpec((tm, BIG_COLS), lambda i: (i, 0)), vspec, vspec],
        out_shape=[jax.ShapeDtypeStruct((T, BIG_COLS), BF16), _value_tile_shape(B, S), _value_tile_shape(B, S)],
        compiler_params=_cparams(("parallel",)),
        name="inproj",
    )(x, g, w_big, *tabs)


def _mla_prep_kernel(x_ref, g_ref, ws_ref, gq_ref, gkv_ref, wq_ref, wk_ref, wv_ref,
                     c_ref, s1_ref, s2_ref, q_ref, k_ref, v_ref, fl_ref, *, scale):
    h = _rms(x_ref[...], g_ref[...]).astype(BF16)
    small = jnp.dot(h, ws_ref[...], preferred_element_type=F32)
    c_q = small[:, :MLA_Q_LORA]
    c_kv = small[:, MLA_Q_LORA:MLA_Q_LORA + MLA_KV_LORA]
    k_pe = small[:, MLA_Q_LORA + MLA_KV_LORA:MLA_Q_LORA + MLA_KV_LORA + LANES]
    fl_ref[...] = small[:, SMALL_COLS - LANES:SMALL_COLS - LANES + HEADS]
    c, s = c_ref[...], s1_ref[...] + s2_ref[...]
    qn = _rms(c_q, gq_ref[...]).astype(BF16)
    q = jnp.dot(qn, wq_ref[...], preferred_element_type=F32) * scale
    q_ref[...] = _apply_rope_paired(q, c, s).astype(BF16)
    kvn = _rms(c_kv, gkv_ref[...]).astype(BF16)
    k_nope = jnp.dot(kvn, wk_ref[...], preferred_element_type=F32)
    k_rot = _apply_rope_paired(k_pe, c, s)
    k_ref[...] = (k_nope + jnp.tile(k_rot, (1, HEADS))).astype(BF16)
    _store_value_tiles(v_ref, jnp.dot(kvn, wv_ref[...], preferred_element_type=F32))


def _mla_prep(x, g, w_small, g_q, g_kv, wq, wk, wv, tabs, scale, B, S, tm=LIGHT_ROW_TILE):
    T = x.shape[0]
    full = lambda shape: pl.BlockSpec(shape, lambda i: (0,) * len(shape))
    row = lambda n: pl.BlockSpec((tm, n), lambda i: (i, 0))
    qk = HEADS * LANES
    return pl.pallas_call(
        functools.partial(_mla_prep_kernel, scale=scale), grid=(T // tm,),
        in_specs=[row(D_MODEL), full((1, D_MODEL)), full((D_MODEL, SMALL_COLS)),
                  full((1, MLA_Q_LORA)), full((1, MLA_KV_LORA)),
                  full((MLA_Q_LORA, qk)), full((MLA_KV_LORA, qk)), full((MLA_KV_LORA, BRANCH_WIDTH)),
                  row(LANES), row(LANES), row(LANES)],
        out_specs=[row(qk), row(qk), _value_tile_spec(tm, S, 1), row(HEADS)],
        out_shape=[jax.ShapeDtypeStruct((T, qk), BF16), jax.ShapeDtypeStruct((T, qk), BF16),
                   _value_tile_shape(B, S), jax.ShapeDtypeStruct((T, HEADS), F32)],
        compiler_params=_cparams(("parallel",)),
        name="mla_prep",
    )(x, g, w_small, g_q, g_kv, wq, wk, wv, *tabs)


def _cumlogf_kernel(fl_ref, b_ref, c_ref):
    z = fl_ref[0] + b_ref[...]
    x = jnp.minimum(z, 0.0) - jnp.log1p(jnp.exp(-jnp.abs(z)))
    n = x.shape[-1]
    lane = lax.broadcasted_iota(jnp.int32, x.shape, 1)
    d = 1
    while d < n:
        x = x + jnp.where(lane >= d, pltpu.roll(x, d, 1), 0.0)
        d *= 2
    c_ref[0] = x


def _cumlogf(fl_t, b_col):
    B, H, S = fl_t.shape
    return pl.pallas_call(
        _cumlogf_kernel, grid=(B,),
        in_specs=[pl.BlockSpec((1, H, S), lambda b: (b, 0, 0)), pl.BlockSpec((H, 1), lambda b: (0, 0))],
        out_specs=pl.BlockSpec((1, H, S), lambda b: (b, 0, 0)),
        out_shape=jax.ShapeDtypeStruct((B, H, S), F32),
        compiler_params=_cparams(("parallel",)),
        name="cumlogf",
    )(fl_t, b_col)


def _split3(c):
    hi = c.astype(BF16)
    r = c - hi.astype(F32)
    mid = r.astype(BF16)
    lo = (r - mid.astype(F32)).astype(BF16)
    return hi.astype(F32), mid.astype(F32), lo.astype(F32)


def _fox_prep_kernel(q_ref, k_ref, c_ref, qo_ref, ko_ref):
    tm = q_ref.shape[0]
    lane = lax.broadcasted_iota(jnp.int32, (tm, LANES), 1)
    c = c_ref[...] * LOG2E
    for hp in range(HEADS // 2):
        q2 = q_ref[:, hp * LANES:(hp + 1) * LANES].astype(F32)
        k2 = k_ref[:, hp * LANES:(hp + 1) * LANES].astype(F32)
        for hh in range(2):
            h = 2 * hp + hh
            terms = [jnp.broadcast_to(t, (tm, LANES)) for t in _split3(c[:, h:h + 1])]
            qh = q2 if hh == 0 else pltpu.roll(q2, HEAD_DIM, 1)
            kh = k2 if hh == 0 else pltpu.roll(k2, HEAD_DIM, 1)
            q_aug = jnp.where(lane < HEAD_DIM + 3, 1.0, 0.0)
            k_aug = jnp.where((lane >= HEAD_DIM + 3) & (lane < HEAD_DIM + 6), 1.0, 0.0)
            for n, t in enumerate(terms):
                q_aug = jnp.where(lane == HEAD_DIM + 3 + n, t, q_aug)
                k_aug = jnp.where(lane == HEAD_DIM + n, -t, k_aug)
            q_aug = jnp.where(lane < HEAD_DIM, qh, q_aug)
            k_aug = jnp.where(lane < HEAD_DIM, kh, k_aug)
            qo_ref[:, h * LANES:(h + 1) * LANES] = q_aug.astype(BF16)
            ko_ref[:, h * LANES:(h + 1) * LANES] = k_aug.astype(BF16)


def _fox_prep(big, c_rows, tm=LIGHT_ROW_TILE):
    T = big.shape[0]
    qk = HEADS * LANES
    return pl.pallas_call(
        _fox_prep_kernel, grid=(T // tm,),
        in_specs=[pl.BlockSpec((tm, BRANCH_WIDTH), lambda i: (i, COL_FQ // BRANCH_WIDTH)),
                  pl.BlockSpec((tm, BRANCH_WIDTH), lambda i: (i, COL_FK // BRANCH_WIDTH)),
                  pl.BlockSpec((tm, HEADS), lambda i: (i, 0))],
        out_specs=[pl.BlockSpec((tm, qk), lambda i: (i, 0)), pl.BlockSpec((tm, qk), lambda i: (i, 0))],
        out_shape=[jax.ShapeDtypeStruct((T, qk), BF16), jax.ShapeDtypeStruct((T, qk), BF16)],
        compiler_params=_cparams(("parallel",)),
        name="fox_prep",
    )(big, big, c_rows)


def _nt_dot(a, b):
    return lax.dot_general(a, b, (((1,), (1,)), ((), ())), preferred_element_type=F32)


def _softmax_step(st, m, acc, vt):
    m_new = jnp.maximum(m, jnp.max(st, axis=0, keepdims=True))
    upd = jnp.dot(vt, jnp.exp2(st - m_new).astype(BF16), preferred_element_type=F32)
    return m_new, jnp.exp2(m - m_new) * acc + upd


def _softmax_init(tq):
    return jnp.full((1, tq), M_INIT, F32), jnp.zeros((V_ROWS, tq), F32)


def _softmax_finish(acc):
    return (acc[:HEAD_DIM] / acc[HEAD_DIM:HEAD_DIM + 1]).T


def _attend(npairs, qk, val, past, diag, tq, sa, sb):
    heads = range(len(sa))

    def put(dst, j):
        for hh in heads:
            dst[hh][:, :tq] = qk(hh, j)

    def advance(carry, src, j, fn):
        return tuple(_softmax_step(fn(hh, j, src[hh][:, :tq]), *carry[hh], val(hh, j)) for hh in heads)

    put(sa, 0)

    def body(jj, carry):
        j0 = 2 * jj
        put(sb, j0 + 1)
        carry = advance(carry, sa, j0, past)
        put(sa, j0 + 2)
        return advance(carry, sb, j0 + 1, past)

    carry = lax.fori_loop(0, npairs, body, tuple(_softmax_init(tq) for _ in heads))
    j0 = 2 * npairs
    put(sb, j0 + 1)
    carry = advance(carry, sa, j0, lambda hh, j, st: diag(hh, 0, st))
    carry = advance(carry, sb, j0 + 1, lambda hh, j, st: diag(hh, 1, st))
    return jnp.concatenate([_softmax_finish(acc) for _, acc in carry], axis=1)


def _score_scratch(tq, tk, nh):
    return [pltpu.VMEM((tk, tq + LANES), F32) for _ in range(2 * nh)]


def _flash_kernel(q_ref, k_ref, vt_ref, o_ref, *scores, tq, tk, nh):
    i = pl.program_id(2)
    assert tq == 2 * tk
    krow = lax.broadcasted_iota(jnp.int32, (tk, tq), 0)
    qcol = lax.broadcasted_iota(jnp.int32, (tk, tq), 1)

    def qk(hh, j):
        return _nt_dot(k_ref[0, j, :, hh * LANES:(hh + 1) * LANES], q_ref[0, :, hh * LANES:(hh + 1) * LANES])

    def val(hh, j):
        return vt_ref[0, j, hh * V_ROWS:(hh + 1) * V_ROWS, :]

    def diag(hh, d, st):
        return jnp.where(d * tk + krow <= qcol, st, NEG_INF)

    out = _attend(i, qk, val, lambda hh, j, st: st, diag, tq, scores[:nh], scores[nh:])
    o_ref[0] = out.astype(BF16)


def _flash(q, k, vt, tq, tk, nh=FLASH_HEADS):
    B, S, _ = q.shape
    nk = S // tk
    return pl.pallas_call(
        functools.partial(_flash_kernel, tq=tq, tk=tk, nh=nh), grid=(B, HEADS // nh, S // tq),
        in_specs=[pl.BlockSpec((1, tq, nh * LANES), lambda b, h, i: (b, i, h)),
                  pl.BlockSpec((1, nk, tk, nh * LANES), lambda b, h, i: (b, 0, 0, h)),
                  pl.BlockSpec((1, nk, nh * V_ROWS, tk), lambda b, h, i: (b, 0, h, 0))],
        out_specs=pl.BlockSpec((1, tq, nh * HEAD_DIM), lambda b, h, i: (b, i, h)),
        out_shape=jax.ShapeDtypeStruct((B, S, BRANCH_WIDTH), BF16),
        scratch_shapes=_score_scratch(tq, tk, nh),
        compiler_params=_cparams(("parallel", "parallel", "arbitrary")),
        name="flash",
    )(q, k, vt)


def _moba_kernel(q_ref, k_ref, vt_ref, o_ref, kmean_ref, bias_ref, qm_ref, *scores, nblk, tq, nh):
    i = pl.program_id(2)
    blk = MOBA_BLOCK
    shift = blk.bit_length() - 1
    r = tq // blk

    @pl.when(i == 0)
    def _():
        for n in range(nblk):
            kmean_ref[n:n + 1, :] = jnp.mean(k_ref[0, n].astype(F32), axis=0, keepdims=True)

    lane = lax.broadcasted_iota(jnp.int32, (tq, nh * HEAD_DIM), 1)
    blk_id = lax.broadcasted_iota(jnp.int32, (nblk, tq), 0)
    own = i * r + (lax.broadcasted_iota(jnp.int32, (nblk, tq), 1) >> shift)
    q2 = q_ref[0]
    for hh in range(nh):
        in_head = (lane >= hh * HEAD_DIM) & (lane < (hh + 1) * HEAD_DIM)
        q = jnp.where(in_head, q2, jnp.zeros_like(q2))
        qm_ref[hh] = q
        g3 = _nt_dot(jnp.concatenate([t.astype(BF16) for t in _split3(kmean_ref[...])], axis=0), q)
        g = g3[:nblk] + g3[nblk:2 * nblk] + g3[2 * nblk:]
        g = jnp.where(blk_id < own, g, NEG_INF)
        bias = jnp.full((nblk, tq), NEG_INF, F32)
        for _ in range(MOBA_TOPK):
            mx = jnp.max(g, axis=0, keepdims=True)
            first = jnp.min(jnp.where(g == mx, blk_id, nblk), axis=0, keepdims=True)
            pick = (blk_id == first) & (mx > NEG_INF)
            bias = jnp.where(pick, 0.0, bias)
            g = jnp.where(pick, NEG_INF, g)
        bias_ref[hh] = bias

    krow = lax.broadcasted_iota(jnp.int32, (blk, tq), 0)
    qcol = lax.broadcasted_iota(jnp.int32, (blk, tq), 1)

    def qk(hh, n):
        return _nt_dot(k_ref[0, n], qm_ref[hh])

    def val(hh, n):
        return vt_ref[0, n, hh * V_ROWS:(hh + 1) * V_ROWS, :]

    def past(hh, n, st):
        return st + bias_ref[hh, pl.ds(n, 1), :]

    def diag(hh, d, st):
        own_causal = ((qcol >> shift) == d) & (krow <= (qcol & (blk - 1)))
        return jnp.where(own_causal, st, past(hh, i * r + d, st))

    out = _attend(i, qk, val, past, diag, tq, scores[:nh], scores[nh:])
    o_ref[0] = out.astype(BF16)


def _moba(q, k, vt, tq, nh=MOBA_STEP_HEADS):
    B, S, _ = q.shape
    nblk = S // MOBA_BLOCK
    assert tq == 2 * MOBA_BLOCK and MOBA_BLOCK & (MOBA_BLOCK - 1) == 0
    w = nh * HEAD_DIM
    qc = COL_MQ // w
    kc = COL_MK // w
    return pl.pallas_call(
        functools.partial(_moba_kernel, nblk=nblk, tq=tq, nh=nh), grid=(B, HEADS // nh, S // tq),
        in_specs=[pl.BlockSpec((1, tq, w), lambda b, h, i: (b, i, qc + h)),
                  pl.BlockSpec((1, nblk, MOBA_BLOCK, w), lambda b, h, i: (b, 0, 0, kc + h)),
                  pl.BlockSpec((1, nblk, nh * V_ROWS, MOBA_BLOCK), lambda b, h, i: (b, 0, h, 0))],
        out_specs=pl.BlockSpec((1, tq, w), lambda b, h, i: (b, i, h)),
        out_shape=jax.ShapeDtypeStruct((B, S, BRANCH_WIDTH), BF16),
        scratch_shapes=[pltpu.VMEM((nblk, w), F32), pltpu.VMEM((nh, nblk, tq), F32),
                        pltpu.VMEM((nh, tq, w), BF16)] + _score_scratch(tq, MOBA_BLOCK, nh),
        compiler_params=_cparams(("parallel", "parallel", "arbitrary")),
        name="moba",
    )(q, k, vt)


def _merge_kernel(x_ref, ya_ref, yb_ref, yc_ref, ga_ref, gb_ref, gc_ref, wb_ref, wo_ref, o_ref):
    merged = None
    for n, (y_ref, g_ref) in enumerate(((ya_ref, ga_ref), (yb_ref, gb_ref), (yc_ref, gc_ref))):
        proj = jnp.dot(y_ref[...], wb_ref[n], preferred_element_type=F32)
        term = jax.nn.sigmoid(g_ref[...].astype(F32)) * proj
        merged = term if merged is None else merged + term
    o_ref[...] = x_ref[...] + jnp.dot(merged.astype(BF16), wo_ref[...], preferred_element_type=F32)


def _merge(x, y_mla, y_fox, y_moba, big, w_branch, w_out, tm=ROW_TILE):
    T = x.shape[0]
    g0 = COL_GATES // D_MODEL
    row = lambda n: pl.BlockSpec((tm, n), lambda i: (i, 0))
    gate = lambda n: pl.BlockSpec((tm, D_MODEL), lambda i: (i, g0 + n))
    return pl.pallas_call(
        _merge_kernel, grid=(T // tm,),
        in_specs=[row(D_MODEL), row(BRANCH_WIDTH), row(BRANCH_WIDTH), row(BRANCH_WIDTH),
                  gate(0), gate(1), gate(2),
                  _resident((3, BRANCH_WIDTH, D_MODEL), lambda i: (0, 0, 0)),
                  _resident((D_MODEL, D_MODEL), lambda i: (0, 0))],
        out_specs=row(D_MODEL),
        out_shape=jax.ShapeDtypeStruct((T, D_MODEL), F32),
        compiler_params=_cparams(("parallel",)),
        name="merge",
    )(x, y_mla, y_fox, y_moba, big, big, big, w_branch, w_out)


def _dense_ffn_kernel(x_ref, g_ref, wg_ref, wu_ref, wd_ref, o_ref):
    x = x_ref[...]
    h = _rms(x, g_ref[...]).astype(BF16)
    gate = jnp.dot(h, wg_ref[...], preferred_element_type=F32)
    up = jnp.dot(h, wu_ref[...], preferred_element_type=F32)
    act = (jax.nn.silu(gate) * up).astype(BF16)
    o_ref[...] = x + jnp.dot(act, wd_ref[...], preferred_element_type=F32)


def _dense_ffn(x, g, w_gu, w_down, tm=ROW_TILE):
    T = x.shape[0]
    return pl.pallas_call(
        _dense_ffn_kernel, grid=(T // tm,),
        in_specs=[pl.BlockSpec((tm, D_MODEL), lambda i: (i, 0)),
                  _resident((1, D_MODEL), lambda i: (0, 0)),
                  _resident((D_MODEL, DENSE_FF), lambda i: (0, 0)),
                  _resident((D_MODEL, DENSE_FF), lambda i: (0, 1)),
                  _resident((DENSE_FF, D_MODEL), lambda i: (0, 0))],
        out_specs=pl.BlockSpec((tm, D_MODEL), lambda i: (i, 0)),
        out_shape=jax.ShapeDtypeStruct((T, D_MODEL), F32),
        compiler_params=_cparams(("parallel",)),
        name="dense_ffn",
    )(x, g, w_gu, w_gu, w_down)


def _router_kernel(x_ref, g_ref, wr_ref, h_ref, r_ref):
    h = _rms(x_ref[...], g_ref[...])
    h_ref[...] = h
    logits = jnp.dot(h, wr_ref[...], precision=lax.Precision.HIGHEST, preferred_element_type=F32)
    lane = lax.broadcasted_iota(jnp.int32, logits.shape, 1)
    logits = jnp.where(lane < N_EXPERTS, logits, NEG_INF)
    m1 = jnp.max(logits, axis=-1, keepdims=True)
    i1 = jnp.min(jnp.where(logits == m1, lane, LANES), axis=-1, keepdims=True)
    rest = jnp.where(lane == i1, NEG_INF, logits)
    m2 = jnp.max(rest, axis=-1, keepdims=True)
    i2 = jnp.min(jnp.where(rest == m2, lane, LANES), axis=-1, keepdims=True)
    e2 = jnp.exp(m2 - m1)
    w1 = 1.0 / (1.0 + e2)
    w2 = e2 / (1.0 + e2)
    r_ref[...] = jnp.where(lane == 0, i1.astype(F32), jnp.where(lane == 1, i2.astype(F32),
                           jnp.where(lane == 2, w1, jnp.where(lane == 3, w2, 0.0))))


def _router(x, g, w_router_pad, tm=LIGHT_ROW_TILE):
    T = x.shape[0]
    return pl.pallas_call(
        _router_kernel, grid=(T // tm,),
        in_specs=[pl.BlockSpec((tm, D_MODEL), lambda i: (i, 0)),
                  pl.BlockSpec((1, D_MODEL), lambda i: (0, 0)),
                  pl.BlockSpec((D_MODEL, LANES), lambda i: (0, 0))],
        out_specs=[pl.BlockSpec((tm, D_MODEL), lambda i: (i, 0)), pl.BlockSpec((tm, LANES), lambda i: (i, 0))],
        out_shape=[jax.ShapeDtypeStruct((T, D_MODEL), F32), jax.ShapeDtypeStruct((T, LANES), F32)],
        compiler_params=_cparams(("parallel",)),
        name="router",
    )(x, g, w_router_pad)


GATHER_WINDOW = 128
GATHER_ROWS = 64


def _gather_rows(src, idx):
    M = idx.shape[0]
    C = src.shape[1]
    mesh = plsc.VectorSubcoreMesh(core_axis_name="core", subcore_axis_name="subcore")
    per = M // (mesh.num_cores * mesh.num_subcores)
    assert per * mesh.num_cores * mesh.num_subcores == M and per % GATHER_WINDOW == 0

    @pl.kernel(out_type=jax.ShapeDtypeStruct((M, C), src.dtype), mesh=mesh, name="gather_rows",
               scratch_types=[pltpu.VMEM((GATHER_WINDOW,), jnp.int32), pltpu.VMEM((GATHER_ROWS, C), src.dtype)])
    def gather(x_hbm, i_hbm, o_hbm, idx_v, buf):
        w = lax.axis_index("core") * mesh.num_subcores + lax.axis_index("subcore")

        @pl.loop(0, per // GATHER_WINDOW)
        def _(t):
            base = w * per + t * GATHER_WINDOW
            pltpu.sync_copy(i_hbm.at[pl.ds(base, GATHER_WINDOW)], idx_v)
            for k in range(GATHER_WINDOW // GATHER_ROWS):
                pltpu.sync_copy(x_hbm.at[idx_v.at[pl.ds(k * GATHER_ROWS, GATHER_ROWS)]], buf)
                pltpu.sync_copy(buf, o_hbm.at[pl.ds(base + k * GATHER_ROWS, GATHER_ROWS)])

    return gather(src, idx)


def _scatter_rows(src, dest0, dest1, n_rows):
    T, C = src.shape
    mesh = plsc.VectorSubcoreMesh(core_axis_name="core", subcore_axis_name="subcore")
    per = T // (mesh.num_cores * mesh.num_subcores)
    assert per * mesh.num_cores * mesh.num_subcores == T and per % GATHER_WINDOW == 0

    @pl.kernel(out_type=jax.ShapeDtypeStruct((n_rows, C), src.dtype), mesh=mesh, name="scatter_rows",
               scratch_types=[pltpu.VMEM((GATHER_WINDOW,), jnp.int32), pltpu.VMEM((GATHER_WINDOW,), jnp.int32),
                              pltpu.VMEM((GATHER_ROWS, C), src.dtype)])
    def scatter(x_hbm, d0_hbm, d1_hbm, o_hbm, i0, i1, buf):
        w = lax.axis_index("core") * mesh.num_subcores + lax.axis_index("subcore")

        @pl.loop(0, per // GATHER_WINDOW)
        def _(t):
            base = w * per + t * GATHER_WINDOW
            pltpu.sync_copy(d0_hbm.at[pl.ds(base, GATHER_WINDOW)], i0)
            pltpu.sync_copy(d1_hbm.at[pl.ds(base, GATHER_WINDOW)], i1)
            for k in range(GATHER_WINDOW // GATHER_ROWS):
                pltpu.sync_copy(x_hbm.at[pl.ds(base + k * GATHER_ROWS, GATHER_ROWS)], buf)
                pltpu.sync_copy(buf, o_hbm.at[i0.at[pl.ds(k * GATHER_ROWS, GATHER_ROWS)]])
                pltpu.sync_copy(buf, o_hbm.at[i1.at[pl.ds(k * GATHER_ROWS, GATHER_ROWS)]])

    return scatter(src, dest0, dest1)


EXPERT_SPLIT = 2


def _expert_kernel(be_ref, nused_ref, valid_ref, x_ref, wgu_ref, wd_ref, o_ref):
    b = pl.program_id(0)
    used = b < nused_ref[0]
    tf = EXPERT_FF // EXPERT_SPLIT

    @pl.when(used)
    def _():
        row = lax.broadcasted_iota(jnp.int32, (x_ref.shape[0], 1), 0)
        x = jnp.where(row < valid_ref[b], x_ref[...], 0.0).astype(BF16)
        y = None
        for f in range(EXPERT_SPLIT):
            gate = jnp.dot(x, wgu_ref[0, :, f * tf:(f + 1) * tf], preferred_element_type=F32)
            up = jnp.dot(x, wgu_ref[0, :, EXPERT_FF + f * tf:EXPERT_FF + (f + 1) * tf], preferred_element_type=F32)
            act = (jax.nn.silu(gate) * up).astype(BF16)
            part = jnp.dot(act, wd_ref[0, f * tf:(f + 1) * tf, :], preferred_element_type=F32)
            y = part if y is None else y + part
        o_ref[...] = y

    @pl.when(jnp.logical_not(used))
    def _():
        o_ref[...] = jnp.zeros_like(o_ref)


def _experts(x_rows, block_e, n_used, valid, w_gu, w_down, bm):
    n_rows = x_rows.shape[0]
    grid_spec = pltpu.PrefetchScalarGridSpec(
        num_scalar_prefetch=3, grid=(n_rows // bm,),
        in_specs=[pl.BlockSpec((bm, D_MODEL), lambda b, be, nu, va: (b, 0)),
                  pl.BlockSpec((1, D_MODEL, 2 * EXPERT_FF), lambda b, be, nu, va: (be[b], 0, 0)),
                  pl.BlockSpec((1, EXPERT_FF, D_MODEL), lambda b, be, nu, va: (be[b], 0, 0))],
        out_specs=pl.BlockSpec((bm, D_MODEL), lambda b, be, nu, va: (b, 0)))
    return pl.pallas_call(
        _expert_kernel, grid_spec=grid_spec,
        out_shape=jax.ShapeDtypeStruct((n_rows, D_MODEL), F32),
        compiler_params=_cparams(("arbitrary",), EXPERT_VMEM_LIMIT),
        name="experts",
    )(block_e, n_used, valid, x_rows, w_gu, w_down)


def _combine_kernel(x_ref, y0_ref, y1_ref, r_ref, g_ref, o_ref):
    r = r_ref[...]
    x = x_ref[...] + r[:, 2:3] * y0_ref[...] + r[:, 3:4] * y1_ref[...]
    o_ref[...] = _rms(x, g_ref[...])


def _combine(x, y_slots, route, g_final, tm=LIGHT_ROW_TILE):
    T = x.shape[0]
    return pl.pallas_call(
        _combine_kernel, grid=(T // tm,),
        in_specs=[pl.BlockSpec((tm, D_MODEL), lambda i: (i, 0)),
                  pl.BlockSpec((tm, D_MODEL), lambda i: (i, 0)),
                  pl.BlockSpec((tm, D_MODEL), lambda i: (T // tm + i, 0)),
                  pl.BlockSpec((tm, LANES), lambda i: (i, 0)),
                  pl.BlockSpec((1, D_MODEL), lambda i: (0, 0))],
        out_specs=pl.BlockSpec((tm, D_MODEL), lambda i: (i, 0)),
        out_shape=jax.ShapeDtypeStruct((T, D_MODEL), F32),
        compiler_params=_cparams(("parallel",)),
        name="combine",
    )(x, y_slots, y_slots, route, g_final)


MLA_X1_LANE, MLA_X2_LANE = 0, LANES // 2


def _mla_lanes(nope, rope):
    half = MLA_ROPE_DIM // 2
    split = LANES // 2 - half
    pad = jnp.zeros(nope.shape[:-1] + (LANES // 2 - half - (HEAD_DIM - split),), nope.dtype)
    return jnp.concatenate([rope[..., :half], nope[..., :split], rope[..., half:], nope[..., split:], pad], axis=-1)


def _pad_cols(w, n):
    return jnp.pad(w, ((0, 0), (0, n - w.shape[1])))


def _split_in_weights(w_in):
    sizes = [MLA_Q_LORA, MLA_KV_LORA, MLA_ROPE_DIM, BRANCH_WIDTH, BRANCH_WIDTH, BRANCH_WIDTH, HEADS,
             BRANCH_WIDTH, BRANCH_WIDTH, BRANCH_WIDTH, 3 * D_MODEL]
    pts = np.cumsum(sizes)[:-1]
    c_q, c_kv, k_pe, fq, fk, fv, fl, mq, mk, mv, gates = jnp.split(w_in, pts, axis=1)
    pe_tile = _mla_lanes(jnp.zeros((D_MODEL, HEAD_DIM), w_in.dtype), k_pe)
    w_small = jnp.concatenate([c_q, c_kv, pe_tile, _pad_cols(fl, LANES)], axis=1)
    w_big = jnp.concatenate([fq, mq, fk, mk, fv, mv, gates], axis=1)
    return w_small.astype(BF16), w_big.astype(BF16)


def _mla_up_weights(w_uq, w_ukv):
    wq = w_uq.reshape(MLA_Q_LORA, HEADS, HEAD_DIM + MLA_ROPE_DIM)
    wq = _mla_lanes(wq[:, :, :HEAD_DIM], wq[:, :, HEAD_DIM:]).reshape(MLA_Q_LORA, HEADS * LANES)
    wkv = w_ukv.reshape(MLA_KV_LORA, HEADS, 2 * HEAD_DIM)
    wk = _mla_lanes(wkv[:, :, :HEAD_DIM], jnp.zeros((MLA_KV_LORA, HEADS, MLA_ROPE_DIM), w_ukv.dtype))
    wv = wkv[:, :, HEAD_DIM:].reshape(MLA_KV_LORA, BRANCH_WIDTH)
    return wq.astype(BF16), wk.reshape(MLA_KV_LORA, HEADS * LANES).astype(BF16), wv.astype(BF16)


def _chunk_rows(a, B, S, t):
    return a.reshape(B, S // t, t, a.shape[-1])


ATT_TQ = 2 * KV_CHUNK
ATT_TK = KV_CHUNK
MOBA_TQ = 2 * MOBA_BLOCK
MOE_BM = 512


def _token_mixers(x, B, S, g_mix, w_in, g_q_lat, g_kv_lat, w_uq, w_ukv, b_forget, w_branch, w_out,
                  tabs_mla, tabs_moba):
    T = B * S
    w_small, w_big = _split_in_weights(w_in)
    wq, wk, wv = _mla_up_weights(w_uq, w_ukv)
    g = g_mix[None, :]

    big, vt_fox, vt_moba = _inproj(x, g, w_big, tabs_moba, B, S)
    q_mla, k_mla, vt_mla, f_logit = _mla_prep(
        x, g, w_small, g_q_lat[None, :], g_kv_lat[None, :], wq, wk, wv, tabs_mla,
        (HEAD_DIM + MLA_ROPE_DIM) ** -0.5 * LOG2E, B, S)

    y_mla = _flash(q_mla.reshape(B, S, -1), _chunk_rows(k_mla, B, S, ATT_TK), vt_mla, ATT_TQ, ATT_TK)

    c = _cumlogf(f_logit.reshape(B, S, HEADS).transpose(0, 2, 1), b_forget[:, None])
    c_rows = c.transpose(0, 2, 1).reshape(T, HEADS)
    q_fox, k_fox = _fox_prep(big, c_rows)
    y_fox = _flash(q_fox.reshape(B, S, -1), _chunk_rows(k_fox, B, S, ATT_TK), vt_fox, ATT_TQ, ATT_TK)

    y_moba = _moba(big.reshape(B, S, -1), _chunk_rows(big, B, S, MOBA_BLOCK), vt_moba, MOBA_TQ)

    return _merge(x, y_mla.reshape(T, -1), y_fox.reshape(T, -1), y_moba.reshape(T, -1), big,
                  w_branch.astype(BF16), w_out.astype(BF16))


def _moe(x, g_ffn, w_router, w_gu, w_down, g_final):
    T = x.shape[0]
    A = 2 * T
    bm = MOE_BM
    h, route = _router(x, g_ffn[None, :], _pad_cols(w_router, LANES))
    top_e = route[:, :2].astype(jnp.int32).reshape(A)
    onehot = (top_e[:, None] == jnp.arange(N_EXPERTS)[None, :]).astype(jnp.int32)
    csum = jnp.cumsum(onehot, axis=0)
    counts = csum[-1]
    rank = jnp.sum((csum - onehot) * onehot, axis=1)
    padded = (counts + bm - 1) // bm * bm
    pad_end = jnp.cumsum(padded)
    dest = (pad_end - padded)[top_e] + rank
    n_rows = (A // bm + N_EXPERTS) * bm
    block_start = jnp.arange(n_rows // bm) * bm
    block_e = jnp.minimum(jnp.searchsorted(pad_end, block_start, side="right"), N_EXPERTS - 1).astype(jnp.int32)
    n_used = (pad_end[-1:] // bm).astype(jnp.int32)
    valid = jnp.clip((pad_end - padded + counts)[block_e] - block_start, 0, bm).astype(jnp.int32)
    dest_slots = dest.astype(jnp.int32).reshape(T, 2).T

    x_rows = _scatter_rows(h, dest_slots[0], dest_slots[1], n_rows)
    y_rows = _experts(x_rows, block_e, n_used, valid, w_gu.astype(BF16), w_down.astype(BF16), bm)
    y_slots = _gather_rows(y_rows, dest_slots.reshape(A))
    return _combine(x, y_slots, route, g_final[None, :])


def kernel(x, positions, g_mix, w_in, g_q_lat, g_kv_lat, w_uq, w_ukv, b_forget, w_branch, w_out, g_ffn,
           w_dense_gu, w_dense_down, w_router, w_exp_gu, w_exp_down, g_final):
    B, S, D = x.shape
    T = B * S
    depth = g_mix.shape[0]
    assert depth == 2 and D == D_MODEL and S % LIGHT_ROW_TILE == 0 and KV_CHUNK == MOBA_BLOCK
    pos = positions.reshape(T, 1).astype(jnp.int32)
    tabs_mla = _rope_tables(pos, *_rope_patterns(LANES, MLA_X1_LANE, MLA_ROPE_DIM // 2, MLA_X2_LANE))
    tabs_moba = _rope_tables(pos, *_rope_patterns(HEAD_DIM, 0, HEAD_DIM // 2))
    x = x.reshape(T, D)
    for l in range(depth):
        x = _token_mixers(x, B, S, g_mix[l], w_in[l], g_q_lat[l], g_kv_lat[l], w_uq[l], w_ukv[l],
                          b_forget[l], w_branch[l], w_out[l], tabs_mla, tabs_moba)
        if l % 2 == 0:
            x = _dense_ffn(x, g_ffn[l][None, :], w_dense_gu[l // 2].astype(BF16),
                           w_dense_down[l // 2].astype(BF16))
        else:
            x = _moe(x, g_ffn[l], w_router[l // 2], w_exp_gu[l // 2], w_exp_down[l // 2], g_final)
    return x.reshape(B, S, D)
```
